```python
import math
import jax, jax.numpy as jnp
from jax import lax
import numpy as np

D_MODEL = 2048
BATCH = 16
SEQ = 256
DEPTH = 4
DEC_BATCH = 8
DEC_SEQ = 2048
PAST_LEN = 256

GRID_W = 64
N_EVEN = (DEPTH + 1) // 2
N_ODD = DEPTH // 2
EPS = 1e-6
CONV_W = 4
CONV_LEFT = CONV_W // 2

D_RNN = D_MODEL // 2
LRU_BLOCKS = 8
LRU_BW = D_RNN // LRU_BLOCKS
LRU_C = 8.0
DA_HEADS = 8
DA_HALF = 64
DA_VDIM = 2 * DA_HALF
D_ATTN = DA_HEADS * DA_VDIM
EVEN_SPLITS = (D_RNN, 2 * D_RNN, 2 * D_RNN + D_ATTN, 2 * D_RNN + 2 * D_ATTN)
EVEN_IN = 2 * D_RNN + 3 * D_ATTN
EVEN_MIX = D_RNN + D_ATTN
Q_BLOCK = 128
ROPE_THETA = 10000.0

D_INNER = 2 * D_MODEL
SSD_HEADDIM = 64
SSD_HEADS = D_INNER // SSD_HEADDIM
SSD_GROUPS = 8
SSD_HPG = SSD_HEADS // SSD_GROUPS
SSD_STATE = 128
SSD_CHUNK = 128
D_XBC = D_INNER + 2 * SSD_GROUPS * SSD_STATE
ODD_IN = D_INNER + D_XBC + 2 * SSD_HEADS

D_FF = -(-8 * D_MODEL // 768) * 256

kernel_name = "hybrid_diffusion_rglru_diffattn_ssd_step"


def rms_norm(x, g):
    xf = x.astype(jnp.float32)
    y = xf * lax.rsqrt(jnp.mean(xf * xf, axis=-1, keepdims=True) + EPS)
    return (y * g.astype(jnp.float32)).astype(x.dtype)


def ada_params(cond, w, b):
    m = jax.nn.silu(cond) @ w + b
    return m.reshape(cond.shape[0], 1, 6, D_MODEL)


def modulate(x, g, shift, scale):
    return rms_norm(x, g) * (1.0 + scale) + shift


def depthwise_conv(x, w, b):
    L = x.shape[1]
    xp = jnp.pad(x, ((0, 0), (CONV_LEFT, CONV_W - 1 - CONV_LEFT), (0, 0)))
    out = b
    for t in range(CONV_W):
        out = out + xp[:, t:t + L] * w[t]
    return out


def axial_rope_tables(L):
    rows = L // GRID_W
    row = jnp.repeat(jnp.arange(rows), GRID_W).astype(jnp.float32)
    col = jnp.tile(jnp.arange(GRID_W), rows).astype(jnp.float32)
    quarter = DA_HALF // 4
    inv = ROPE_THETA ** (-jnp.arange(quarter, dtype=jnp.float32) / quarter)
    ang_r = row[:, None] * inv
    ang_c = col[:, None] * inv
    ang = jnp.concatenate([ang_r, ang_r, ang_c, ang_c], axis=-1)
    return jnp.cos(ang), jnp.sin(ang)


def apply_axial_rope(x, cos, sin):
    q = DA_HALF // 4
    xf = x.astype(jnp.float32)
    x1, x2, x3, x4 = xf[..., :q], xf[..., q:2 * q], xf[..., 2 * q:3 * q], xf[..., 3 * q:]
    rot = jnp.concatenate([-x2, x1, -x4, x3], axis=-1)
    return (xf * cos[:, None, None, :] + rot * sin[:, None, None, :]).astype(x.dtype)


def _lin_combine(l, r):
    al, bl = l
    ar, br = r
    return al * ar, ar * bl + br


def rglru_scan_dir(xc, w_r, b_r, w_i, b_i, lam, h0):
    Bsz, L, _ = xc.shape
    xb = xc.reshape(Bsz, L, LRU_BLOCKS, LRU_BW)
    r = jax.nn.sigmoid(jnp.einsum('blki,kij->blkj', xb, w_r.astype(jnp.float32)).reshape(Bsz, L, D_RNN) + b_r)
    gi = jax.nn.sigmoid(jnp.einsum('blki,kij->blkj', xb, w_i.astype(jnp.float32)).reshape(Bsz, L, D_RNN) + b_i)
    log_a = -LRU_C * r * jax.nn.softplus(-lam.astype(jnp.float32))
    a = jnp.exp(log_a)
    bt = jnp.sqrt(-jnp.expm1(2.0 * log_a)) * (gi * xc)
    bt = bt.at[:, 0].add(a[:, 0] * h0)
    _, hs = lax.associative_scan(_lin_combine, (a, bt), axis=1)
    return hs


def rglru_bidir(xc, w_r, b_r, w_i, b_i, lam, h0):
    hf = rglru_scan_dir(xc, w_r[0], b_r[0], w_i[0], b_i[0], lam[0], h0[:, 0])
    hb = jnp.flip(rglru_scan_dir(jnp.flip(xc, 1), w_r[1], b_r[1], w_i[1], b_i[1], lam[1], h0[:, 1]), 1)
    return hf + hb, jnp.stack([hf[:, -1], hb[:, 0]], axis=1)


def diff_attention(q, k, v, lam):
    s = jnp.einsum('bqhmd,bkhmd->bhmqk', q, k).astype(jnp.float32) * (DA_HALF ** -0.5)
    p = jax.nn.softmax(s, axis=-1)
    w = p[:, :, 0] - lam * p[:, :, 1]
    return jnp.einsum('bhqk,bkhd->bqhd', w.astype(v.dtype), v)


def even_mixer(h, ctx, rope, lam_init, w_in, w_out, conv_w, conv_b, w_r, b_r, w_i, b_i, lru_lam,
               q_norm, k_norm, da_lam, subln):
    Bsz, L, _ = h.shape
    f32 = jnp.float32
    u = h @ w_in
    gate, xr, q, k, v = jnp.split(u, EVEN_SPLITS, axis=-1)
    xc = depthwise_conv(xr, conv_w, conv_b).astype(f32)
    h0 = jnp.zeros((Bsz, 2, D_RNN), f32) if ctx is None else ctx[2].astype(f32)
    rec, s_fin = rglru_bidir(xc, w_r, b_r, w_i, b_i, lru_lam, h0)
    rec = (rec * jax.nn.gelu(gate.astype(f32))).astype(h.dtype)
    q = rms_norm(q.reshape(Bsz, L, DA_HEADS, 2, DA_HALF), q_norm)
    k = rms_norm(k.reshape(Bsz, L, DA_HEADS, 2, DA_HALF), k_norm)
    v = v.reshape(Bsz, L, DA_HEADS, DA_VDIM)
    if ctx is None:
        k_all, v_all = k, v
    else:
        cos, sin = rope
        q = apply_axial_rope(q, cos, sin)
        k_lat = apply_axial_rope(k, cos, sin)
        k_ctx = ctx[0].reshape(Bsz, -1, DA_HEADS, 2, DA_HALF).astype(k.dtype)
        k_all = jnp.concatenate([k_lat, k_ctx], axis=1)
        v_all = jnp.concatenate([v, ctx[1].astype(v.dtype)], axis=1)
    dl = da_lam.astype(f32)
    lam = jnp.exp(jnp.sum(dl[0] * dl[1])) - jnp.exp(jnp.sum(dl[2] * dl[3])) + lam_init
    qb = q.reshape(Bsz, L // Q_BLOCK, Q_BLOCK, DA_HEADS, 2, DA_HALF).swapaxes(0, 1)
    att = lax.map(lambda qq: diff_attention(qq, k_all, v_all, lam), qb)
    att = att.swapaxes(0, 1).reshape(Bsz, L, DA_HEADS, DA_VDIM)
    att = (rms_norm(att, subln) * (1.0 - lam_init)).reshape(Bsz, L, D_ATTN).astype(h.dtype)
    out = jnp.concatenate([rec, att], axis=-1) @ w_out
    return out, (k.reshape(Bsz, L, DA_HEADS, 2 * DA_HALF), v, s_fin)


def segsum(a):
    T = a.shape[-1]
    cs = jnp.cumsum(a, axis=-1)
    diff = cs[..., :, None] - cs[..., None, :]
    return jnp.where(jnp.tril(jnp.ones((T, T), bool)), diff, -jnp.inf)


def ssd_chunked(X, A, Bm, Cm, h0):
    b, l, _, p = X.shape
    c, q = l // SSD_CHUNK, SSD_CHUNK
    g, e, n = SSD_GROUPS, SSD_HPG, SSD_STATE
    X = X.reshape(b, c, q, g, e, p)
    A = A.reshape(b, c, q, g, e).transpose(0, 3, 4, 1, 2)
    Bm = Bm.reshape(b, c, q, g, n)
    Cm = Cm.reshape(b, c, q, g, n)
    A_cs = jnp.cumsum(A, axis=-1)
    Lmat = jnp.exp(segsum(A))
    CB = jnp.einsum('bclgn,bcsgn->bgcls', Cm, Bm)
    Y_diag = jnp.einsum('bgcls,bgecls,bcsgep->bclgep', CB, Lmat, X)
    decay_states = jnp.exp(A_cs[..., -1:] - A_cs)
    states = jnp.einsum('bclgn,bgecl,bclgep->bcgepn', Bm, decay_states, X)
    states = jnp.concatenate([h0.reshape(b, 1, g, e, p, n), states], axis=1)
    chunk_tot = jnp.pad(A_cs[..., -1], ((0, 0), (0, 0), (0, 0), (1, 0)))
    decay_chunk = jnp.exp(segsum(chunk_tot))
    states = jnp.einsum('bgezc,bcgepn->bzgepn', decay_chunk, states)
    prev_states, final = states[:, :-1], states[:, -1]
    Y_off = jnp.einsum('bclgn,bcgepn,bgecl->bclgep', Cm, prev_states, jnp.exp(A_cs))
    Y = (Y_diag + Y_off).reshape(b, l, g * e, p)
    return Y, final.reshape(b, g * e, p, n)


def odd_mixer(h, h0, w_in, conv_w, conv_b, dt_bias, a_log, d_skip, norm_w, w_out):
    Bsz, L, _ = h.shape
    f32 = jnp.float32
    u = h @ w_in
    z, xbc, dt = jnp.split(u, (D_INNER, D_INNER + D_XBC), axis=-1)
    xbc = jax.nn.silu(depthwise_conv(xbc, conv_w, conv_b)).astype(f32)
    xs, bm, cm = jnp.split(xbc, (D_INNER, D_INNER + SSD_GROUPS * SSD_STATE), axis=-1)
    xs = xs.reshape(Bsz, L, SSD_HEADS, SSD_HEADDIM)
    bm = bm.reshape(Bsz, L, SSD_GROUPS, SSD_STATE)
    cm = cm.reshape(Bsz, L, SSD_GROUPS, SSD_STATE)
    dt = jax.nn.softplus(dt.astype(f32).reshape(Bsz, L, 2, SSD_HEADS) + dt_bias.astype(f32))
    a = -jnp.exp(a_log.astype(f32))
    if h0 is None:
        h0 = jnp.zeros((Bsz, 2, SSD_HEADS, SSD_HEADDIM, SSD_STATE), f32)
    h0 = h0.astype(f32)
    yf, sf = ssd_chunked(xs * dt[:, :, 0, :, None], a[0] * dt[:, :, 0], bm, cm, h0[:, 0])
    yb, sb = ssd_chunked(jnp.flip(xs * dt[:, :, 1, :, None], 1), jnp.flip(a[1] * dt[:, :, 1], 1),
                         jnp.flip(bm, 1), jnp.flip(cm, 1), h0[:, 1])
    y = yf + jnp.flip(yb, 1) + d_skip.astype(f32)[:, None] * xs
    y = y.reshape(Bsz, L, D_INNER) * jax.nn.silu(z.astype(f32))
    yg = y.reshape(Bsz, L, SSD_GROUPS, D_INNER // SSD_GROUPS)
    yg = yg * lax.rsqrt(jnp.mean(yg * yg, axis=-1, keepdims=True) + EPS)
    yg = yg.reshape(Bsz, L, D_INNER) * norm_w.astype(f32)
    return yg.astype(h.dtype) @ w_out, jnp.stack([sf, sb], axis=1)


def swiglu(h, w_in, w_out):
    g, u = jnp.split(h @ w_in, 2, axis=-1)
    return (jax.nn.silu(g) * u) @ w_out


def setup_inputs(seed: int = 0) -> dict:
    key = jax.random.key(seed)
    ks = jax.random.split(key, 40)
    f32 = jnp.float32

    def nrm(k, shape, scale):
        return jax.random.normal(k, shape, f32) * scale

    def gain(k, shape):
        return 1.0 + 0.02 * jax.random.normal(k, shape, f32)

    u_lru = jax.random.uniform(ks[13], (N_EVEN, 2, D_RNN), f32, 0.9, 0.999)
    a_base = u_lru ** (1.0 / LRU_C)
    lru_lambda = jnp.log(a_base) - jnp.log1p(-a_base)
    dt0 = jnp.exp(jax.random.uniform(ks[25], (N_ODD, 2, SSD_HEADS), f32, math.log(1e-3), math.log(1e-1)))
    ssd_dt_bias = dt0 + jnp.log(-jnp.expm1(-dt0))
    ssd_a_log = jnp.log(jax.random.uniform(ks[26], (N_ODD, 2, SSD_HEADS), f32, 1.0, 16.0))
    return {
        "x_prompt": nrm(ks[0], (BATCH, SEQ, D_MODEL), 1.0),
        "x_sample": nrm(ks[1], (DEC_BATCH, DEC_SEQ, D_MODEL), 1.0),
        "c": nrm(ks[2], (DEC_BATCH, D_MODEL), 1.0),
        "cache_attn_k": nrm(ks[3], (DEC_BATCH, N_EVEN, PAST_LEN, DA_HEADS, 2 * DA_HALF), 1.0),
        "cache_attn_v": nrm(ks[4], (DEC_BATCH, N_EVEN, PAST_LEN, DA_HEADS, DA_VDIM), 1.0),
        "state_lru": nrm(ks[5], (DEC_BATCH, N_EVEN, 2, D_RNN), 0.5),
        "state_ssd": nrm(ks[6], (DEC_BATCH, N_ODD, 2, SSD_HEADS, SSD_HEADDIM, SSD_STATE), 0.1),
        "c_ctx": nrm(ks[7], (D_MODEL,), 1.0),
        "w_ada": nrm(ks[8], (DEPTH, D_MODEL, 6 * D_MODEL), 0.5 * D_MODEL ** -0.5),
        "b_ada": nrm(ks[9], (DEPTH, 6 * D_MODEL), 0.02),
        "norm_g": gain(ks[10], (DEPTH, 2, D_MODEL)),
        "lru_conv_w": nrm(ks[11], (N_EVEN, CONV_W, D_RNN), CONV_W ** -0.5),
        "lru_conv_b": nrm(ks[12], (N_EVEN, D_RNN), 0.02),
        "lru_w_r": nrm(ks[14], (N_EVEN, 2, LRU_BLOCKS, LRU_BW, LRU_BW), LRU_BW ** -0.5),
        "lru_b_r": nrm(ks[15], (N_EVEN, 2, D_RNN), 0.02),
        "lru_w_i": nrm(ks[16], (N_EVEN, 2, LRU_BLOCKS, LRU_BW, LRU_BW), LRU_BW ** -0.5),
        "lru_b_i": nrm(ks[17], (N_EVEN, 2, D_RNN), 0.02),
        "lru_lambda": lru_lambda,
        "even_w_in": nrm(ks[18], (N_EVEN, D_MODEL, EVEN_IN), D_MODEL ** -0.5),
        "even_w_out": nrm(ks[19], (N_EVEN, EVEN_MIX, D_MODEL), EVEN_MIX ** -0.5),
        "da_q_norm": gain(ks[20], (N_EVEN, 2, DA_HALF)),
        "da_k_norm": gain(ks[21], (N_EVEN, 2, DA_HALF)),
        "da_lambda": nrm(ks[22], (N_EVEN, 4, DA_HALF), 0.1),
        "da_subln": gain(ks[23], (N_EVEN, DA_VDIM)),
        "ssd_w_in": nrm(ks[24], (N_ODD, D_MODEL, ODD_IN), D_MODEL ** -0.5),
        "ssd_conv_w": nrm(ks[27], (N_ODD, CONV_W, D_XBC), CONV_W ** -0.5),
        "ssd_conv_b": nrm(ks[28], (N_ODD, D_XBC), 0.02),
        "ssd_dt_bias": ssd_dt_bias,
        "ssd_a_log": ssd_a_log,
        "ssd_d": gain(ks[29], (N_ODD, SSD_HEADS)),
        "ssd_norm_w": gain(ks[30], (N_ODD, D_INNER)),
        "ssd_w_out": nrm(ks[31], (N_ODD, D_INNER, D_MODEL), D_INNER ** -0.5),
        "ffn_w_in": nrm(ks[32], (DEPTH, D_MODEL, 2 * D_FF), D_MODEL ** -0.5),
        "ffn_w_out": nrm(ks[33], (DEPTH, D_FF, D_MODEL), D_FF ** -0.5),
    }


def reference(x_prompt, x_sample, c, cache_attn_k, cache_attn_v, state_lru, state_ssd, c_ctx,
              w_ada, b_ada, norm_g, lru_conv_w, lru_conv_b, lru_w_r, lru_b_r, lru_w_i, lru_b_i,
              lru_lambda, even_w_in, even_w_out, da_q_norm, da_k_norm, da_lambda, da_subln,
              ssd_w_in, ssd_conv_w, ssd_conv_b, ssd_dt_bias, ssd_a_log, ssd_d, ssd_norm_w,
              ssd_w_out, ffn_w_in, ffn_w_out):
    rope = axial_rope_tables(x_sample.shape[1])
    yp, ys = x_prompt, x_sample
    new_k, new_v, new_lru, new_ssd = [], [], [], []
    for i in range(DEPTH):
        j = i // 2
        mp = ada_params(c_ctx[None, :], w_ada[i], b_ada[i])
        ms = ada_params(c, w_ada[i], b_ada[i])
        hp = modulate(yp, norm_g[i, 0], mp[:, :, 0], mp[:, :, 1])
        hs = modulate(ys, norm_g[i, 0], ms[:, :, 0], ms[:, :, 1])
        if i % 2 == 0:
            lam_init = 0.8 - 0.6 * math.exp(-0.3 * i)
            ew = (even_w_in[j], even_w_out[j], lru_conv_w[j], lru_conv_b[j], lru_w_r[j], lru_b_r[j],
                  lru_w_i[j], lru_b_i[j], lru_lambda[j], da_q_norm[j], da_k_norm[j], da_lambda[j],
                  da_subln[j])
            op, (kc, vc, sc) = even_mixer(hp, None, None, lam_init, *ew)
            ctx = (cache_attn_k[:, j], cache_attn_v[:, j], state_lru[:, j])
            os_, _ = even_mixer(hs, ctx, rope, lam_init, *ew)
            new_k.append(kc)
            new_v.append(vc)
            new_lru.append(sc)
        else:
            ow = (ssd_w_in[j], ssd_conv_w[j], ssd_conv_b[j], ssd_dt_bias[j], ssd_a_log[j], ssd_d[j],
                  ssd_norm_w[j], ssd_w_out[j])
            op, sc = odd_mixer(hp, None, *ow)
            os_, _ = odd_mixer(hs, state_ssd[:, j], *ow)
            new_ssd.append(sc)
        yp = yp + mp[:, :, 2] * op
        ys = ys + ms[:, :, 2] * os_
        hp = modulate(yp, norm_g[i, 1], mp[:, :, 3], mp[:, :, 4])
        hs = modulate(ys, norm_g[i, 1], ms[:, :, 3], ms[:, :, 4])
        yp = yp + mp[:, :, 5] * swiglu(hp, ffn_w_in[i], ffn_w_out[i])
        ys = ys + ms[:, :, 5] * swiglu(hs, ffn_w_in[i], ffn_w_out[i])
    return (yp, ys, jnp.stack(new_k, axis=1), jnp.stack(new_v, axis=1),
            jnp.stack(new_lru, axis=1), jnp.stack(new_ssd, axis=1))
```

```python
import functools
import math

import jax
import jax.numpy as jnp
from jax import lax
from jax.experimental import pallas as pl
from jax.experimental.pallas import tpu as pltpu

F32 = jnp.float32
BF16 = jnp.bfloat16

EPS = 1e-6
GRID_W = 64
CONV_W = 4
CONV_LEFT = CONV_W // 2
LRU_BW = 128
LRU_C = 8.0
DA_HALF = 64
DA_VDIM = 2 * DA_HALF
ROPE_THETA = 10000.0
SSD_HEADDIM = 64
SSD_GROUPS = 8
SSD_HPG = 8
SSD_STATE = 128
SSD_CHUNK = 128
N_COND = 16

LANE = 128
SUBLANE = 8
VMEM_LIMIT = 56 * 1024 * 1024

ROW_TILE = 1024
MOD_ROWS = 32
CONV_ROWS = 256
PAD = SUBLANE


def _cparams(*sem):
    return pltpu.CompilerParams(dimension_semantics=sem, vmem_limit_bytes=VMEM_LIMIT)


def _ada_kernel(c_ref, w_ref, b_ref, o_ref):
    s = jax.nn.silu(c_ref[...]).astype(BF16)
    o_ref[...] = jnp.dot(s, w_ref[...].astype(BF16), preferred_element_type=F32) + b_ref[...]


def _ada(cond, w_ada, b_ada):
    depth, d, n = w_ada.shape
    tn = 1024
    return pl.pallas_call(
        _ada_kernel,
        grid=(depth, n // tn),
        in_specs=[
            pl.BlockSpec((N_COND, d), lambda l, j: (0, 0)),
            pl.BlockSpec((None, d, tn), lambda l, j: (l, 0, j)),
            pl.BlockSpec((None, 1, tn), lambda l, j: (l, 0, j)),
        ],
        out_specs=pl.BlockSpec((None, N_COND, tn), lambda l, j: (l, 0, j)),
        out_shape=jax.ShapeDtypeStruct((depth, N_COND, n), F32),
        compiler_params=_cparams("parallel", "parallel"),
        name="ada",
    )(cond, w_ada, b_ada.reshape(depth, 1, n))


def _modulate_into(x_ref, g_ref, mod_ref, h_ref, shift_idx, scale_idx):
    gs = g_ref[...] * (1.0 + mod_ref[scale_idx:scale_idx + 1, :])
    shift = mod_ref[shift_idx:shift_idx + 1, :]

    def body(r, carry):
        rows = pl.ds(pl.multiple_of(r * MOD_ROWS, MOD_ROWS), MOD_ROWS)
        x = x_ref[rows, :]
        ms = jnp.mean(x * x, axis=-1, keepdims=True)
        h_ref[rows, :] = (x * lax.rsqrt(ms + EPS) * gs + shift).astype(BF16)
        return carry

    lax.fori_loop(0, x_ref.shape[0] // MOD_ROWS, body, 0)


def _inproj_kernel(*refs, mode, shift_idx, scale_idx):
    if mode == "plain":
        x_ref, g_ref, mod_ref, w_ref, o_ref, h_ref = refs
    elif mode == "glu":
        x_ref, g_ref, mod_ref, wg_ref, wu_ref, o_ref, h_ref = refs
    else:
        x_ref, g_ref, mod_ref, w_ref, wdt_ref, o_ref, dt_ref, h_ref = refs

    @pl.when(pl.program_id(1) == 0)
    def _():
        _modulate_into(x_ref, g_ref, mod_ref, h_ref, shift_idx, scale_idx)
        if mode == "dt":
            dt_ref[...] = jnp.dot(h_ref[...], wdt_ref[...], preferred_element_type=F32)

    h = h_ref[...]
    if mode == "glu":
        g = jnp.dot(h, wg_ref[...], preferred_element_type=F32)
        u = jnp.dot(h, wu_ref[...], preferred_element_type=F32)
        o_ref[...] = (jax.nn.silu(g) * u).astype(o_ref.dtype)
    else:
        o_ref[...] = jnp.dot(h, w_ref[...], preferred_element_type=F32).astype(o_ref.dtype)


def _inproj(x, g, mod, w, *, mode, n_out, tn, shift_idx, scale_idx, name):
    t, d = x.shape
    tm = ROW_TILE
    rows_per_cond = t // mod.shape[0]
    nj = n_out // tn
    x_spec = pl.BlockSpec((tm, d), lambda i, j: (i, 0))
    g_spec = pl.BlockSpec((1, d), lambda i, j: (0, 0))
    mod_spec = pl.BlockSpec((None, 6, d), lambda i, j: ((i * tm) // rows_per_cond, 0, 0))
    o_spec = pl.BlockSpec((tm, tn), lambda i, j: (i, j))
    o_shape = jax.ShapeDtypeStruct((t, n_out), BF16)
    if mode == "plain":
        in_specs = [x_spec, g_spec, mod_spec, pl.BlockSpec((d, tn), lambda i, j: (0, j))]
        args = (x, g, mod, w)
        out_specs, out_shape = o_spec, o_shape
    elif mode == "glu":
        in_specs = [x_spec, g_spec, mod_spec,
                    pl.BlockSpec((d, tn), lambda i, j: (0, j)),
                    pl.BlockSpec((d, tn), lambda i, j: (0, nj + j))]
        args = (x, g, mod, w, w)
        out_specs, out_shape = o_spec, o_shape
    else:
        in_specs = [x_spec, g_spec, mod_spec,
                    pl.BlockSpec((d, tn), lambda i, j: (0, j)),
                    pl.BlockSpec((d, LANE), lambda i, j: (0, n_out // LANE))]
        args = (x, g, mod, w, w)
        out_specs = (o_spec, pl.BlockSpec((tm, LANE), lambda i, j: (i, 0)))
        out_shape = (o_shape, jax.ShapeDtypeStruct((t, LANE), F32))
    return pl.pallas_call(
        functools.partial(_inproj_kernel, mode=mode, shift_idx=shift_idx, scale_idx=scale_idx),
        grid=(t // tm, nj),
        in_specs=in_specs,
        out_specs=out_specs,
        out_shape=out_shape,
        scratch_shapes=[pltpu.VMEM((tm, d), BF16)],
        compiler_params=_cparams("parallel", "arbitrary"),
        name=name,
    )(*args)


def _outproj_kernel(*refs, n_a, gate_idx):
    a_refs, w_refs = refs[:n_a], refs[n_a:2 * n_a]
    x_ref, mod_ref, o_ref = refs[2 * n_a:]
    acc = jnp.dot(a_refs[0][...], w_refs[0][...], preferred_element_type=F32)
    for a_ref, w_ref in zip(a_refs[1:], w_refs[1:]):
        acc = acc + jnp.dot(a_ref[...], w_ref[...], preferred_element_type=F32)
    o_ref[...] = x_ref[...] + mod_ref[gate_idx:gate_idx + 1, :] * acc


def _outproj(a_list, w, x, mod, *, gate_idx, tn, name):
    t, d = x.shape
    tm = ROW_TILE
    rows_per_cond = t // mod.shape[0]
    n_a = len(a_list)
    k = a_list[0].shape[1]
    in_specs = [pl.BlockSpec((tm, k), lambda i, j: (i, 0)) for _ in a_list]
    in_specs += [pl.BlockSpec((k, tn), functools.partial(lambda i, j, q: (q, j), q=q)) for q in range(n_a)]
    in_specs += [pl.BlockSpec((tm, tn), lambda i, j: (i, j)),
                 pl.BlockSpec((None, 6, tn), lambda i, j: ((i * tm) // rows_per_cond, 0, j))]
    return pl.pallas_call(
        functools.partial(_outproj_kernel, n_a=n_a, gate_idx=gate_idx),
        grid=(t // tm, d // tn),
        in_specs=in_specs,
        out_specs=pl.BlockSpec((tm, tn), lambda i, j: (i, j)),
        out_shape=jax.ShapeDtypeStruct((t, d), F32),
        compiler_params=_cparams("parallel", "parallel"),
        name=name,
    )(*a_list, *([w] * n_a), x, mod)


def _conv_into(src_ref, w_ref, b_ref, pad_ref, dst_ref, seq, width, act):
    zeros = jnp.zeros((PAD, width), F32)
    pad_ref[0:PAD, 0:width] = zeros
    pad_ref[PAD + seq:2 * PAD + seq, 0:width] = zeros
    step = min(seq, CONV_ROWS)
    for r0 in range(0, seq, step):
        pad_ref[PAD + r0:PAD + r0 + step, 0:width] = src_ref[r0:r0 + step, :].astype(F32)
    w = w_ref[...]
    for r0 in range(0, seq, step):
        acc = b_ref[...]
        for k in range(CONV_W):
            lo = PAD - CONV_LEFT + k + r0
            acc = acc + w[k:k + 1, :] * pad_ref[lo:lo + step, 0:width]
        dst_ref[r0:r0 + step, :] = act(acc)


def _lru_kernel(gate_ref, xr_ref, cw_ref, cb_ref, wr_ref, wi_ref, br_ref, bi_ref, lam_ref, h0_ref,
                rec_ref, sfin_ref, pad_s, xc_s, a_s, b_s, h_s, *, seq, width):
    _conv_into(xr_ref, cw_ref, cb_ref, pad_s, xc_s, seq, width, lambda v: v)
    step = min(seq, CONV_ROWS)

    def gates(d):
        sp = jax.nn.softplus(-lam_ref[d:d + 1, :])
        for r0 in range(0, seq, step):
            for k in range(width // LRU_BW):
                ks = slice(k * LRU_BW, (k + 1) * LRU_BW)
                xk = xc_s[r0:r0 + step, ks]
                xb = xk.astype(BF16)
                r = jax.nn.sigmoid(jnp.dot(xb, wr_ref[d, k], preferred_element_type=F32) + br_ref[d:d + 1, ks])
                gi = jax.nn.sigmoid(jnp.dot(xb, wi_ref[d, k], preferred_element_type=F32) + bi_ref[d:d + 1, ks])
                a = jnp.exp(-LRU_C * r * sp[:, ks])
                a_s[r0:r0 + step, ks] = a
                b_s[r0:r0 + step, ks] = jnp.sqrt(1.0 - a * a) * (gi * xk)

    gates(0)

    def fwd(t, h):
        h = a_s[pl.ds(t, 1), :] * h + b_s[pl.ds(t, 1), :]
        h_s[pl.ds(t, 1), :] = h
        return h

    sfin_ref[0:1, :] = lax.fori_loop(0, seq, fwd, h0_ref[0:1, :], unroll=8)

    gates(1)

    def bwd(i, h):
        t = seq - 1 - i
        h = a_s[pl.ds(t, 1), :] * h + b_s[pl.ds(t, 1), :]
        h_s[pl.ds(t, 1), :] = h_s[pl.ds(t, 1), :] + h
        return h

    sfin_ref[1:2, :] = lax.fori_loop(0, seq, bwd, h0_ref[1:2, :], unroll=8)

    for r0 in range(0, seq, step):
        rows = slice(r0, r0 + step)
        rec_ref[rows, :] = (h_s[rows, :] * jax.nn.gelu(gate_ref[rows, :].astype(F32))).astype(BF16)


def _lru(u, nseq, seq, conv_w, conv_b, w_r, w_i, b_r, b_i, lam, h0):
    d_rnn = conv_w.shape[1]
    width = 512
    nb = d_rnn // width
    kb = width // LRU_BW
    vec = lambda rows: pl.BlockSpec((rows, width), lambda b, c: (0, c))
    return pl.pallas_call(
        functools.partial(_lru_kernel, seq=seq, width=width),
        grid=(nseq, nb),
        in_specs=[
            pl.BlockSpec((seq, width), lambda b, c: (b, c)),
            pl.BlockSpec((seq, width), lambda b, c: (b, nb + c)),
            vec(CONV_W), vec(1),
            pl.BlockSpec((2, kb, LRU_BW, LRU_BW), lambda b, c: (0, c, 0, 0)),
            pl.BlockSpec((2, kb, LRU_BW, LRU_BW), lambda b, c: (0, c, 0, 0)),
            vec(2), vec(2), vec(2),
            pl.BlockSpec((None, 2, width), lambda b, c: (b, 0, c)),
        ],
        out_specs=(pl.BlockSpec((seq, width), lambda b, c: (b, c)),
                   pl.BlockSpec((None, 2, width), lambda b, c: (b, 0, c))),
        out_shape=(jax.ShapeDtypeStruct((nseq * seq, d_rnn), BF16),
                   jax.ShapeDtypeStruct((nseq, 2, d_rnn), F32)),
        scratch_shapes=[pltpu.VMEM((seq + 2 * PAD, width), F32)] + [pltpu.VMEM((seq, width), F32)] * 4,
        compiler_params=_cparams("parallel", "parallel"),
        name="lru",
    )(u, u, conv_w, conv_b.reshape(1, d_rnn), w_r, w_i, b_r, b_i, lam, h0)


def _half_rms(x, gain):
    x2 = x * x
    s0 = jnp.sum(x2[:, :DA_HALF], axis=-1, keepdims=True)
    s1 = jnp.sum(x2[:, DA_HALF:], axis=-1, keepdims=True)
    lane = lax.broadcasted_iota(jnp.int32, x.shape, 1)
    ms = jnp.where(lane < DA_HALF, s0, s1) * (1.0 / DA_HALF)
    return x * lax.rsqrt(ms + EPS) * gain


def _rope(x, cos, sin_signed):
    q = DA_HALF // 4
    lane = lax.broadcasted_iota(jnp.int32, x.shape, 1)
    rot = jnp.where((lane % (2 * q)) < q, pltpu.roll(x, LANE - q, 1), pltpu.roll(x, q, 1))
    return x * cos + rot * sin_signed


def _attn_kernel(*refs, latent, seq, past, lam_init):
    if latent:
        (q_ref, k_ref, v_ref, kc_ref, vc_ref, cosq_ref, sinq_ref, cosk_ref, sink_ref,
         qn_ref, kn_ref, dl_ref, sub_ref, o_ref, kt_s, v_s) = refs
    else:
        (q_ref, k_ref, v_ref, qn_ref, kn_ref, dl_ref, sub_ref, o_ref, ko_ref, vo_ref, kt_s, v_s) = refs

    @pl.when(pl.program_id(2) == 0)
    def _():
        step = min(seq, CONV_ROWS)
        for r0 in range(0, seq, step):
            rows = slice(r0, r0 + step)
            kn = _half_rms(k_ref[rows, :].astype(F32), kn_ref[...])
            if latent:
                kn = _rope(kn, cosk_ref[rows, :], sink_ref[rows, :])
            else:
                ko_ref[rows, :] = kn
                vo_ref[rows, :] = v_ref[rows, :].astype(F32)
            kt_s[:, rows] = kn.T.astype(BF16)
            v_s[rows, :] = v_ref[rows, :]
        if latent:
            kt_s[:, seq:seq + past] = kc_ref[...].T.astype(BF16)
            v_s[seq:seq + past, :] = vc_ref[...].astype(BF16)

    qn = _half_rms(q_ref[...].astype(F32), qn_ref[...])
    if latent:
        qn = _rope(qn, cosq_ref[...], sinq_ref[...])
    qb = (qn * (DA_HALF ** -0.5)).astype(BF16)
    s0 = jnp.dot(qb[:, :DA_HALF], kt_s[0:DA_HALF, :], preferred_element_type=F32)
    s1 = jnp.dot(qb[:, DA_HALF:], kt_s[DA_HALF:, :], preferred_element_type=F32)
    p0 = jnp.exp(s0 - jnp.max(s0, axis=-1, keepdims=True))
    p1 = jnp.exp(s1 - jnp.max(s1, axis=-1, keepdims=True))
    l0 = jnp.sum(p0, axis=-1, keepdims=True)
    l1 = jnp.sum(p1, axis=-1, keepdims=True)
    dl = dl_ref[...]
    lam = (jnp.exp(jnp.sum(dl[0:1] * dl[1:2], axis=-1, keepdims=True))
           - jnp.exp(jnp.sum(dl[2:3] * dl[3:4], axis=-1, keepdims=True)) + lam_init)
    w = p0 * (1.0 / l0) - p1 * (lam / l1)
    o = jnp.dot(w.astype(BF16), v_s[...], preferred_element_type=F32)
    o = o * lax.rsqrt(jnp.mean(o * o, axis=-1, keepdims=True) + EPS) * sub_ref[...] * (1.0 - lam_init)
    o_ref[...] = o.astype(BF16)


def _attn(u, nseq, seq, heads, lam_init, q_norm, k_norm, da_lam, subln, ctx=None, rope=None):
    latent = ctx is not None
    hd = DA_VDIM
    tq = min(seq, 256)
    nq = seq // tq
    col0 = (u.shape[1] - 3 * heads * hd) // hd
    past = ctx[0].shape[2] if latent else 0
    q_spec = pl.BlockSpec((tq, hd), lambda b, h, i: (b * nq + i, col0 + h))
    k_spec = pl.BlockSpec((seq, hd), lambda b, h, i: (b, col0 + heads + h))
    v_spec = pl.BlockSpec((seq, hd), lambda b, h, i: (b, col0 + 2 * heads + h))
    small = lambda r, c: pl.BlockSpec((r, c), lambda b, h, i: (0, 0))
    par_specs = [small(1, hd), small(1, hd), small(4, DA_HALF), small(1, hd)]
    pars = (q_norm.reshape(1, hd), k_norm.reshape(1, hd), da_lam, subln.reshape(1, hd))
    o_spec = pl.BlockSpec((tq, hd), lambda b, h, i: (b * nq + i, h))
    o_shape = jax.ShapeDtypeStruct((nseq * seq, heads * hd), BF16)
    if latent:
        kc, vc, layer = ctx
        cos, sin = rope
        c_spec = pl.BlockSpec((None, None, past, hd), lambda b, h, i: (b, layer, 0, h))
        tq_spec = pl.BlockSpec((tq, hd), lambda b, h, i: (i, 0))
        tk_spec = pl.BlockSpec((seq, hd), lambda b, h, i: (0, 0))
        in_specs = [q_spec, k_spec, v_spec, c_spec, c_spec, tq_spec, tq_spec, tk_spec, tk_spec] + par_specs
        args = (u, u, u, kc, vc, cos, sin, cos, sin) + pars
        out_specs, out_shape = o_spec, o_shape
    else:
        in_specs = [q_spec, k_spec, v_spec] + par_specs
        args = (u, u, u) + pars
        kv_spec = pl.BlockSpec((seq, hd), lambda b, h, i: (b, h))
        kv_shape = jax.ShapeDtypeStruct((nseq * seq, heads * hd), F32)
        out_specs, out_shape = (o_spec, kv_spec, kv_spec), (o_shape, kv_shape, kv_shape)
    return pl.pallas_call(
        functools.partial(_attn_kernel, latent=latent, seq=seq, past=past, lam_init=lam_init),
        grid=(nseq, heads, nq),
        in_specs=in_specs,
        out_specs=out_specs,
        out_shape=out_shape,
        scratch_shapes=[pltpu.VMEM((hd, seq + past), BF16), pltpu.VMEM((seq + past, hd), BF16)],
        compiler_params=_cparams("parallel", "parallel", "arbitrary"),
        name="attn",
    )(*args)


def _split3(x):
    hi = x.astype(BF16)
    r1 = x - hi.astype(F32)
    mid = r1.astype(BF16)
    lo = (r1 - mid.astype(F32)).astype(BF16)
    return hi, mid, lo


def _dot_nt(a, b):
    return lax.dot_general(a, b, (((1,), (1,)), ((), ())), preferred_element_type=F32)


def _dot_tn(a, b):
    return lax.dot_general(a, b, (((0,), (0,)), ((), ())), preferred_element_type=F32)


def _ssd_kernel(*refs, seq, has_h0):
    (z_ref, x_ref, b_ref, c_ref, dtc_ref, dtr_ref, dbc_ref, alc_ref, dbr_ref, alr_ref,
     cwx_ref, cwb_ref, cwc_ref, cbx_ref, cbb_ref, cbc_ref, dsk_ref, nw_ref) = refs[:18]
    rest = refs[18:]
    if has_h0:
        h0_ref, rest = rest[0], rest[1:]
    y_ref, st_ref, pad_s, xs_s, bs_s, cs_s, ys_s, state_s = rest

    q = SSD_CHUNK
    p = SSD_HEADDIM
    _conv_into(x_ref, cwx_ref, cbx_ref, pad_s, xs_s, seq, xs_s.shape[1], jax.nn.silu)
    _conv_into(b_ref, cwb_ref, cbb_ref, pad_s, bs_s, seq, SSD_STATE, jax.nn.silu)
    _conv_into(c_ref, cwc_ref, cbc_ref, pad_s, cs_s, seq, SSD_STATE, jax.nn.silu)

    li = lax.broadcasted_iota(jnp.int32, (q, q), 0)
    si = lax.broadcasted_iota(jnp.int32, (q, q), 1)
    lower = (li >= si)
    upper = (li <= si)
    lower_b = lower.astype(BF16)
    upper_b = upper.astype(BF16)
    a_col = -jnp.exp(alc_ref[...])
    a_row = -jnp.exp(alr_ref[...])

    def chunk(c, d):
        rows = pl.ds(pl.multiple_of(c * q, q), q)
        o = d * SSD_HPG
        dt_col = jax.nn.softplus(dtc_ref[rows, :] + dbc_ref[...])
        dt_row = jax.nn.softplus(dtr_ref[c] + dbr_ref[...])
        da_col = dt_col * a_col
        da_row = dt_row * a_row
        tri_l, tri_r = (lower_b, upper_b) if d == 0 else (upper_b, lower_b)
        cs_col = sum(jnp.dot(tri_l, part, preferred_element_type=F32) for part in _split3(da_col))
        cs_row = sum(jnp.dot(part, tri_r, preferred_element_type=F32) for part in _split3(da_row))
        valid = lower if d == 0 else upper
        last = q - 1 if d == 0 else 0
        bb = bs_s[rows, :].astype(BF16)
        cb_ = cs_s[rows, :].astype(BF16)
        cb = _dot_nt(cb_, bb)
        y_off = _dot_nt(cb_, state_s[...].astype(BF16))
        for h in range(SSD_HPG):
            col = cs_col[:, o + h:o + h + 1]
            row = cs_row[o + h:o + h + 1, :]
            tot = cs_col[last:last + 1, o + h:o + h + 1]
            decay = jnp.exp(jnp.where(valid, col - row, -1e30))
            m = (cb * decay * dt_row[o + h:o + h + 1, :]).astype(BF16)
            hs = slice(h * p, (h + 1) * p)
            xh = xs_s[rows, hs]
            y = jnp.dot(m, xh.astype(BF16), preferred_element_type=F32) + y_off[:, hs] * jnp.exp(col)
            if d == 0:
                ys_s[rows, hs] = y
            else:
                ys_s[rows, hs] = ys_s[rows, hs] + y
            xw = (xh * (dt_col[:, o + h:o + h + 1] * jnp.exp(tot - col))).astype(BF16)
            state_s[hs, :] = state_s[hs, :] * jnp.exp(tot) + _dot_tn(xw, bb)

    nchunks = seq // q
    for d in range(2):
        for h in range(SSD_HPG):
            hs = slice(h * p, (h + 1) * p)
            state_s[hs, :] = h0_ref[d, h] if has_h0 else jnp.zeros((p, SSD_STATE), F32)
        if d == 0:
            lax.fori_loop(0, nchunks, lambda c, carry: (chunk(c, 0), carry)[1], 0)
        else:
            lax.fori_loop(0, nchunks, lambda i, carry: (chunk(nchunks - 1 - i, 1), carry)[1], 0)
        for h in range(SSD_HPG):
            st_ref[d, h] = state_s[h * p:(h + 1) * p, :]

    step = min(seq, CONV_ROWS)
    for r0 in range(0, seq, step):
        rows = slice(r0, r0 + step)
        y = ys_s[rows, :] + dsk_ref[...] * xs_s[rows, :]
        y = y * jax.nn.silu(z_ref[rows, :].astype(F32))
        y = y * lax.rsqrt(jnp.mean(y * y, axis=-1, keepdims=True) + EPS) * nw_ref[...]
        y_ref[rows, :] = y.astype(BF16)


def _ssd(u, dt, nseq, seq, conv_w, conv_b, dt_bias, a_log, d_skip, norm_w, h0):
    t = nseq * seq
    g = SSD_GROUPS
    e = SSD_HPG
    heads = g * e
    gw = e * SSD_HEADDIM
    d_inner = heads * SSD_HEADDIM
    nc = seq // SSD_CHUNK
    xb0 = d_inner // gw
    bb0 = 2 * d_inner // SSD_STATE
    per_group = lambda v: v.reshape(2, g, e).transpose(1, 0, 2).reshape(g, 2 * e)
    dtc = dt.reshape(t, 2, g, e).transpose(2, 0, 1, 3).reshape(g, t, 2 * e)
    dtr = dt.reshape(t // SSD_CHUNK, SSD_CHUNK, 2, g, e).transpose(3, 0, 2, 4, 1).reshape(
        g, t // SSD_CHUNK, 2 * e, SSD_CHUNK)
    dbg = per_group(dt_bias)
    alg = per_group(a_log)
    dsk = jnp.repeat(d_skip, SSD_HEADDIM).reshape(1, d_inner)
    cw = lambda width, blk0: pl.BlockSpec((CONV_W, width), lambda b, gi: (0, blk0 + gi))
    cbias = lambda width, blk0: pl.BlockSpec((1, width), lambda b, gi: (0, blk0 + gi))
    st_spec = pl.BlockSpec((None, 2, e, SSD_HEADDIM, SSD_STATE), lambda b, gi: (b, 0, gi, 0, 0))
    in_specs = [
        pl.BlockSpec((seq, gw), lambda b, gi: (b, gi)),
        pl.BlockSpec((seq, gw), lambda b, gi: (b, xb0 + gi)),
        pl.BlockSpec((seq, SSD_STATE), lambda b, gi: (b, bb0 + gi)),
        pl.BlockSpec((seq, SSD_STATE), lambda b, gi: (b, bb0 + g + gi)),
        pl.BlockSpec((None, seq, 2 * e), lambda b, gi: (gi, b, 0)),
        pl.BlockSpec((None, nc, 2 * e, SSD_CHUNK), lambda b, gi: (gi, b, 0, 0)),
        pl.BlockSpec((None, 1, 2 * e), lambda b, gi: (gi, 0, 0)),
        pl.BlockSpec((None, 1, 2 * e), lambda b, gi: (gi, 0, 0)),
        pl.BlockSpec((None, 2 * e, 1), lambda b, gi: (gi, 0, 0)),
        pl.BlockSpec((None, 2 * e, 1), lambda b, gi: (gi, 0, 0)),
        cw(gw, 0), cw(SSD_STATE, d_inner // SSD_STATE), cw(SSD_STATE, d_inner // SSD_STATE + g),
        cbias(gw, 0), cbias(SSD_STATE, d_inner // SSD_STATE), cbias(SSD_STATE, d_inner // SSD_STATE + g),
        cbias(gw, 0), cbias(gw, 0),
    ]
    conv_b2 = conv_b.reshape(1, -1)
    args = [u, u, u, u, dtc, dtr, dbg.reshape(g, 1, 2 * e), alg.reshape(g, 1, 2 * e),
            dbg.reshape(g, 2 * e, 1), alg.reshape(g, 2 * e, 1),
            conv_w, conv_w, conv_w, conv_b2, conv_b2, conv_b2, dsk, norm_w.reshape(1, d_inner)]
    if h0 is not None:
        in_specs.append(st_spec)
        args.append(h0)
    return pl.pallas_call(
        functools.partial(_ssd_kernel, seq=seq, has_h0=h0 is not None),
        grid=(nseq, g),
        in_specs=in_specs,
        out_specs=(pl.BlockSpec((seq, gw), lambda b, gi: (b, gi)), st_spec),
        out_shape=(jax.ShapeDtypeStruct((t, d_inner), BF16),
                   jax.ShapeDtypeStruct((nseq, 2, heads, SSD_HEADDIM, SSD_STATE), F32)),
        scratch_shapes=[
            pltpu.VMEM((seq + 2 * PAD, gw), F32),
            pltpu.VMEM((seq, gw), F32),
            pltpu.VMEM((seq, SSD_STATE), F32),
            pltpu.VMEM((seq, SSD_STATE), F32),
            pltpu.VMEM((seq, gw), F32),
            pltpu.VMEM((gw, SSD_STATE), F32),
        ],
        compiler_params=_cparams("parallel", "parallel"),
        name="ssd",
    )(*args)


def _rope_tables(seq):
    rows = seq // GRID_W
    row = jnp.repeat(jnp.arange(rows), GRID_W).astype(F32)
    col = jnp.tile(jnp.arange(GRID_W), rows).astype(F32)
    quarter = DA_HALF // 4
    inv = ROPE_THETA ** (-jnp.arange(quarter, dtype=F32) / quarter)
    ang_r = row[:, None] * inv
    ang_c = col[:, None] * inv
    ang = jnp.concatenate([ang_r, ang_r, ang_c, ang_c], axis=-1)
    sign = jnp.tile(jnp.concatenate([-jnp.ones((quarter,), F32), jnp.ones((quarter,), F32)]), 2)
    cos = jnp.cos(ang)
    sin = jnp.sin(ang) * sign
    return jnp.tile(cos, (1, 2)), jnp.tile(sin, (1, 2))


def kernel(x_prompt, x_sample, c, cache_attn_k, cache_attn_v, state_lru, state_ssd, c_ctx, w_ada, b_ada, norm_g, lru_conv_w, lru_conv_b, lru_w_r, lru_b_r, lru_w_i, lru_b_i, lru_lambda, even_w_in, even_w_out, da_q_norm, da_k_norm, da_lambda, da_subln, ssd_w_in, ssd_conv_w, ssd_conv_b, ssd_dt_bias, ssd_a_log, ssd_d, ssd_norm_w, ssd_w_out, ffn_w_in, ffn_w_out):
    depth = w_ada.shape[0]
    batch, seq_p, d = x_prompt.shape
    dec_batch, seq_s, _ = x_sample.shape
    past = cache_attn_k.shape[2]
    heads = cache_attn_k.shape[3]
    d_rnn = lru_conv_w.shape[2]
    d_ff = ffn_w_out.shape[1]
    d_inner = ssd_w_out.shape[1]
    n_even = even_w_in.shape[0]

    cond = jnp.zeros((N_COND, d), F32).at[0].set(c_ctx).at[1:1 + dec_batch].set(c)
    mod_all = _ada(cond, w_ada, b_ada).reshape(depth, N_COND, 6, d)
    rope = _rope_tables(seq_s)
    cache_k = cache_attn_k.reshape(dec_batch, n_even, past, heads * DA_VDIM)
    cache_v = cache_attn_v.reshape(dec_batch, n_even, past, heads * DA_VDIM)

    xs = [x_prompt.reshape(batch * seq_p, d), x_sample.reshape(dec_batch * seq_s, d)]
    shapes = [(batch, seq_p), (dec_batch, seq_s)]
    new_k, new_v, new_lru, new_ssd = [], [], [], []

    for i in range(depth):
        j = i // 2
        mods = [mod_all[i, 0:1], mod_all[i, 1:1 + dec_batch]]
        g_mix = norm_g[i, 0].reshape(1, d)
        g_ffn = norm_g[i, 1].reshape(1, d)
        if i % 2 == 0:
            lam_init = 0.8 - 0.6 * math.exp(-0.3 * i)
            w_in = even_w_in[j].astype(BF16)
            w_out = even_w_out[j].astype(BF16)
            w_r = lru_w_r[j].astype(BF16)
            w_i = lru_w_i[j].astype(BF16)
            for s in range(2):
                nseq, seq = shapes[s]
                u = _inproj(xs[s], g_mix, mods[s], w_in, mode="plain", n_out=w_in.shape[1], tn=1024,
                            shift_idx=0, scale_idx=1, name="even_in")
                h0 = jnp.zeros((nseq, 2, d_rnn), F32) if s == 0 else state_lru[:, j]
                rec, s_fin = _lru(u, nseq, seq, lru_conv_w[j], lru_conv_b[j], w_r, w_i,
                                  lru_b_r[j], lru_b_i[j], lru_lambda[j], h0)
                if s == 0:
                    att, kc, vc = _attn(u, nseq, seq, heads, lam_init, da_q_norm[j], da_k_norm[j],
                                        da_lambda[j], da_subln[j])
                    new_k.append(kc.reshape(nseq, seq, heads, DA_VDIM))
                    new_v.append(vc.reshape(nseq, seq, heads, DA_VDIM))
                    new_lru.append(s_fin)
                else:
                    att = _attn(u, nseq, seq, heads, lam_init, da_q_norm[j], da_k_norm[j],
                                da_lambda[j], da_subln[j], ctx=(cache_k, cache_v, j), rope=rope)
                xs[s] = _outproj([rec, att], w_out, xs[s], mods[s], gate_idx=2, tn=512, name="even_out")
        else:
            w_in = ssd_w_in[j].astype(BF16)
            w_out = ssd_w_out[j].astype(BF16)
            n_main = w_in.shape[1] - 2 * (d_inner // SSD_HEADDIM)
            for s in range(2):
                nseq, seq = shapes[s]
                u, dt = _inproj(xs[s], g_mix, mods[s], w_in, mode="dt", n_out=n_main, tn=1024,
                                shift_idx=0, scale_idx=1, name="odd_in")
                h0 = None if s == 0 else state_ssd[:, j]
                y, st = _ssd(u, dt, nseq, seq, ssd_conv_w[j], ssd_conv_b[j], ssd_dt_bias[j],
                             ssd_a_log[j], ssd_d[j], ssd_norm_w[j], h0)
                if s == 0:
                    new_ssd.append(st)
                xs[s] = _outproj([y], w_out, xs[s], mods[s], gate_idx=2, tn=512, name="odd_out")
        w_in = ffn_w_in[i].astype(BF16)
        w_out = ffn_w_out[i].astype(BF16)
        for s in range(2):
            act = _inproj(xs[s], g_ffn, mods[s], w_in, mode="glu", n_out=d_ff, tn=512,
                          shift_idx=3, scale_idx=4, name="ffn_in")
            xs[s] = _outproj([act], w_out, xs[s], mods[s], gate_idx=5, tn=512, name="ffn_out")

    return (xs[0].reshape(batch, seq_p, d), xs[1].reshape(dec_batch, seq_s, d),
            jnp.stack(new_k, axis=1), jnp.stack(new_v, axis=1),
            jnp.stack(new_lru, axis=1), jnp.stack(new_ssd, axis=1))
```

```python
import functools
import math

import jax
import jax.numpy as jnp
from jax import lax
from jax.experimental import pallas as pl
from jax.experimental.pallas import tpu as pltpu

F32 = jnp.float32
BF16 = jnp.bfloat16

EPS = 1e-6
GRID_W = 64
CONV_W = 4
CONV_LEFT = CONV_W // 2
LRU_BW = 128
LRU_C = 8.0
DA_HALF = 64
DA_VDIM = 2 * DA_HALF
ROPE_THETA = 10000.0
SSD_HEADDIM = 64
SSD_GROUPS = 8
SSD_HPG = 8
SSD_STATE = 128
SSD_CHUNK = 128
N_COND = 16

LANE = 128
SUBLANE = 8
VMEM_LIMIT = 56 * 1024 * 1024

ROW_TILE = 1024
MOD_ROWS = 32
CONV_ROWS = 256
CONV_BLK = 128
HALO = CONV_BLK // 2
LOG2E = 1.4426950408889634


def _cparams(*sem):
    return pltpu.CompilerParams(dimension_semantics=sem, vmem_limit_bytes=VMEM_LIMIT)


def _ada_kernel(c_ref, w_ref, b_ref, o_ref):
    s = jax.nn.silu(c_ref[...]).astype(BF16)
    o_ref[...] = jnp.dot(s, w_ref[...].astype(BF16), preferred_element_type=F32) + b_ref[...]


def _ada(cond, w_ada, b_ada):
    depth, d, n = w_ada.shape
    tn = 1024
    return pl.pallas_call(
        _ada_kernel,
        grid=(depth, n // tn),
        in_specs=[
            pl.BlockSpec((N_COND, d), lambda l, j: (0, 0)),
            pl.BlockSpec((None, d, tn), lambda l, j: (l, 0, j)),
            pl.BlockSpec((None, 1, tn), lambda l, j: (l, 0, j)),
        ],
        out_specs=pl.BlockSpec((None, N_COND, tn), lambda l, j: (l, 0, j)),
        out_shape=jax.ShapeDtypeStruct((depth, N_COND, n), F32),
        compiler_params=_cparams("parallel", "parallel"),
        name="ada",
    )(cond, w_ada, b_ada.reshape(depth, 1, n))


def _modulate_into(x_ref, g_ref, mod_ref, h_ref, shift_idx, scale_idx):
    gs = g_ref[...] * (1.0 + mod_ref[scale_idx:scale_idx + 1, :])
    shift = mod_ref[shift_idx:shift_idx + 1, :]

    def body(r, carry):
        rows = pl.ds(pl.multiple_of(r * MOD_ROWS, MOD_ROWS), MOD_ROWS)
        x = x_ref[rows, :]
        ms = jnp.mean(x * x, axis=-1, keepdims=True)
        h_ref[rows, :] = (x * lax.rsqrt(ms + EPS) * gs + shift).astype(BF16)
        return carry

    lax.fori_loop(0, x_ref.shape[0] // MOD_ROWS, body, 0)


def _inproj_kernel(*refs, mode, shift_idx, scale_idx):
    if mode == "plain":
        x_ref, g_ref, mod_ref, w_ref, o_ref, h_ref = refs
    elif mode == "glu":
        x_ref, g_ref, mod_ref, wg_ref, wu_ref, o_ref, h_ref = refs
    else:
        x_ref, g_ref, mod_ref, w_ref, wdt_ref, o_ref, dt_ref, h_ref = refs

    @pl.when(pl.program_id(1) == 0)
    def _():
        _modulate_into(x_ref, g_ref, mod_ref, h_ref, shift_idx, scale_idx)
        if mode == "dt":
            dt_ref[...] = jnp.dot(h_ref[...], wdt_ref[...], preferred_element_type=F32)

    h = h_ref[...]
    if mode == "glu":
        g = jnp.dot(h, wg_ref[...], preferred_element_type=F32)
        u = jnp.dot(h, wu_ref[...], preferred_element_type=F32)
        o_ref[...] = (jax.nn.silu(g) * u).astype(o_ref.dtype)
    else:
        o_ref[...] = jnp.dot(h, w_ref[...], preferred_element_type=F32).astype(o_ref.dtype)


def _inproj(x, g, mod, w, *, mode, n_out, tn, shift_idx, scale_idx, name):
    t, d = x.shape
    tm = ROW_TILE
    rows_per_cond = t // mod.shape[0]
    nj = n_out // tn
    x_spec = pl.BlockSpec((tm, d), lambda i, j: (i, 0))
    g_spec = pl.BlockSpec((1, d), lambda i, j: (0, 0))
    mod_spec = pl.BlockSpec((None, 6, d), lambda i, j: ((i * tm) // rows_per_cond, 0, 0))
    o_spec = pl.BlockSpec((tm, tn), lambda i, j: (i, j))
    o_shape = jax.ShapeDtypeStruct((t, n_out), BF16)
    if mode == "plain":
        in_specs = [x_spec, g_spec, mod_spec, pl.BlockSpec((d, tn), lambda i, j: (0, j))]
        args = (x, g, mod, w)
        out_specs, out_shape = o_spec, o_shape
    elif mode == "glu":
        in_specs = [x_spec, g_spec, mod_spec,
                    pl.BlockSpec((d, tn), lambda i, j: (0, j)),
                    pl.BlockSpec((d, tn), lambda i, j: (0, nj + j))]
        args = (x, g, mod, w, w)
        out_specs, out_shape = o_spec, o_shape
    else:
        in_specs = [x_spec, g_spec, mod_spec,
                    pl.BlockSpec((d, tn), lambda i, j: (0, j)),
                    pl.BlockSpec((d, LANE), lambda i, j: (0, n_out // LANE))]
        args = (x, g, mod, w, w)
        out_specs = (o_spec, pl.BlockSpec((tm, LANE), lambda i, j: (i, 0)))
        out_shape = (o_shape, jax.ShapeDtypeStruct((t, LANE), F32))
    return pl.pallas_call(
        functools.partial(_inproj_kernel, mode=mode, shift_idx=shift_idx, scale_idx=scale_idx),
        grid=(t // tm, nj),
        in_specs=in_specs,
        out_specs=out_specs,
        out_shape=out_shape,
        scratch_shapes=[pltpu.VMEM((tm, d), BF16)],
        compiler_params=_cparams("parallel", "arbitrary"),
        name=name,
    )(*args)


def _outproj_kernel(*refs, n_a, gate_idx):
    a_refs, w_refs = refs[:n_a], refs[n_a:2 * n_a]
    x_ref, mod_ref, o_ref = refs[2 * n_a:]
    acc = jnp.dot(a_refs[0][...], w_refs[0][...], preferred_element_type=F32)
    for a_ref, w_ref in zip(a_refs[1:], w_refs[1:]):
        acc = acc + jnp.dot(a_ref[...], w_ref[...], preferred_element_type=F32)
    o_ref[...] = x_ref[...] + mod_ref[gate_idx:gate_idx + 1, :] * acc


def _outproj(a_list, w, x, mod, *, gate_idx, tn, name):
    t, d = x.shape
    tm = ROW_TILE
    rows_per_cond = t // mod.shape[0]
    n_a = len(a_list)
    k = a_list[0].shape[1]
    in_specs = [pl.BlockSpec((tm, k), lambda i, j: (i, 0)) for _ in a_list]
    in_specs += [pl.BlockSpec((k, tn), functools.partial(lambda i, j, q: (q, j), q=q)) for q in range(n_a)]
    in_specs += [pl.BlockSpec((tm, tn), lambda i, j: (i, j)),
                 pl.BlockSpec((None, 6, tn), lambda i, j: ((i * tm) // rows_per_cond, 0, j))]
    return pl.pallas_call(
        functools.partial(_outproj_kernel, n_a=n_a, gate_idx=gate_idx),
        grid=(t // tm, d // tn),
        in_specs=in_specs,
        out_specs=pl.BlockSpec((tm, tn), lambda i, j: (i, j)),
        out_shape=jax.ShapeDtypeStruct((t, d), F32),
        compiler_params=_cparams("parallel", "parallel"),
        name=name,
    )(*a_list, *([w] * n_a), x, mod)


def _conv_blocks(src_ref, w_ref, b_ref, pad_ref, seq, width, emit):
    zeros = jnp.zeros((HALO, width), BF16)
    pad_ref[0:HALO, 0:width] = zeros
    pad_ref[HALO + seq:2 * HALO + seq, 0:width] = zeros
    pad_ref[HALO:HALO + seq, 0:width] = src_ref[...]
    r_i = lax.broadcasted_iota(jnp.int32, (CONV_W * CONV_BLK, 2 * CONV_BLK), 0)
    c_i = lax.broadcasted_iota(jnp.int32, (CONV_W * CONV_BLK, 2 * CONV_BLK), 1)
    shift = (c_i == HALO + (r_i % CONV_BLK) + (r_i // CONV_BLK) - CONV_LEFT).astype(BF16)
    w = w_ref[...]
    for r0 in range(0, seq, CONV_BLK):
        sh = jnp.dot(shift, pad_ref[r0:r0 + 2 * CONV_BLK, 0:width], preferred_element_type=F32)
        acc = b_ref[...]
        for k in range(CONV_W):
            acc = acc + w[k:k + 1, :] * sh[k * CONV_BLK:(k + 1) * CONV_BLK]
        emit(r0, acc)


def _lru_kernel(gate_ref, xr_ref, cw_ref, cb_ref, wr_ref, wi_ref, br_ref, bi_ref, lam_ref, h0_ref,
                rec_ref, sfin_ref, pad_s, xc_s, a_s, b_s, h_s, *, seq, width):
    def put_xc(r0, v):
        xc_s[r0:r0 + CONV_BLK, :] = v

    _conv_blocks(xr_ref, cw_ref, cb_ref, pad_s, seq, width, put_xc)
    step = min(seq, CONV_ROWS)

    def gates(d):
        sp = jax.nn.softplus(-lam_ref[d:d + 1, :])
        for r0 in range(0, seq, step):
            for k in range(width // LRU_BW):
                ks = slice(k * LRU_BW, (k + 1) * LRU_BW)
                xk = xc_s[r0:r0 + step, ks]
                xb = xk.astype(BF16)
                r = jax.nn.sigmoid(jnp.dot(xb, wr_ref[d, k], preferred_element_type=F32) + br_ref[d:d + 1, ks])
                gi = jax.nn.sigmoid(jnp.dot(xb, wi_ref[d, k], preferred_element_type=F32) + bi_ref[d:d + 1, ks])
                a = jnp.exp(-LRU_C * r * sp[:, ks])
                a_s[r0:r0 + step, ks] = a
                b_s[r0:r0 + step, ks] = jnp.sqrt(1.0 - a * a) * (gi * xk)

    gates(0)

    def fwd(t, h):
        h = a_s[pl.ds(t, 1), :] * h + b_s[pl.ds(t, 1), :]
        h_s[pl.ds(t, 1), :] = h
        return h

    sfin_ref[0:1, :] = lax.fori_loop(0, seq, fwd, h0_ref[0:1, :], unroll=8)

    gates(1)

    def bwd(i, h):
        t = seq - 1 - i
        h = a_s[pl.ds(t, 1), :] * h + b_s[pl.ds(t, 1), :]
        h_s[pl.ds(t, 1), :] = h_s[pl.ds(t, 1), :] + h
        return h

    sfin_ref[1:2, :] = lax.fori_loop(0, seq, bwd, h0_ref[1:2, :], unroll=8)

    for r0 in range(0, seq, step):
        rows = slice(r0, r0 + step)
        rec_ref[rows, :] = (h_s[rows, :] * jax.nn.gelu(gate_ref[rows, :].astype(F32))).astype(BF16)


def _lru(u, nseq, seq, conv_w, conv_b, w_r, w_i, b_r, b_i, lam, h0):
    d_rnn = conv_w.shape[1]
    width = 512
    nb = d_rnn // width
    kb = width // LRU_BW
    vec = lambda rows: pl.BlockSpec((rows, width), lambda b, c: (0, c))
    return pl.pallas_call(
        functools.partial(_lru_kernel, seq=seq, width=width),
        grid=(nseq, nb),
        in_specs=[
            pl.BlockSpec((seq, width), lambda b, c: (b, c)),
            pl.BlockSpec((seq, width), lambda b, c: (b, nb + c)),
            vec(CONV_W), vec(1),
            pl.BlockSpec((2, kb, LRU_BW, LRU_BW), lambda b, c: (0, c, 0, 0)),
            pl.BlockSpec((2, kb, LRU_BW, LRU_BW), lambda b, c: (0, c, 0, 0)),
            vec(2), vec(2), vec(2),
            pl.BlockSpec((None, 2, width), lambda b, c: (b, 0, c)),
        ],
        out_specs=(pl.BlockSpec((seq, width), lambda b, c: (b, c)),
                   pl.BlockSpec((None, 2, width), lambda b, c: (b, 0, c))),
        out_shape=(jax.ShapeDtypeStruct((nseq * seq, d_rnn), BF16),
                   jax.ShapeDtypeStruct((nseq, 2, d_rnn), F32)),
        scratch_shapes=[pltpu.VMEM((seq + 2 * HALO, width), BF16)] + [pltpu.VMEM((seq, width), F32)] * 4,
        compiler_params=_cparams("parallel", "parallel"),
        name="lru",
    )(u, u, conv_w, conv_b.reshape(1, d_rnn), w_r, w_i, b_r, b_i, lam, h0)


def _half_rms(x, gain):
    x2 = x * x
    s0 = jnp.sum(x2[:, :DA_HALF], axis=-1, keepdims=True)
    s1 = jnp.sum(x2[:, DA_HALF:], axis=-1, keepdims=True)
    lane = lax.broadcasted_iota(jnp.int32, x.shape, 1)
    ms = jnp.where(lane < DA_HALF, s0, s1) * (1.0 / DA_HALF)
    return x * lax.rsqrt(ms + EPS) * gain


def _rope(x, cos, sin_signed):
    q = DA_HALF // 4
    lane = lax.broadcasted_iota(jnp.int32, x.shape, 1)
    rot = jnp.where((lane % (2 * q)) < q, pltpu.roll(x, LANE - q, 1), pltpu.roll(x, q, 1))
    return x * cos + rot * sin_signed


def _attn_kernel(*refs, latent, seq, past, lam_init, sub):
    if latent:
        (q_ref, k_ref, v_ref, kc_ref, vc_ref, cosq_ref, sinq_ref, cosk_ref, sink_ref,
         qn_ref, kn_ref, dl_ref, sub_ref, o_ref, kt_s, vx_s) = refs
    else:
        (q_ref, k_ref, v_ref, qn_ref, kn_ref, dl_ref, sub_ref, o_ref, ko_ref, vo_ref, kt_s, vx_s) = refs
    hd = DA_VDIM

    @pl.when(pl.program_id(2) == 0)
    def _():
        vx_s[:, hd:2 * hd] = jnp.ones((seq + past, hd), BF16)
        step = min(seq, CONV_ROWS)
        for r0 in range(0, seq, step):
            rows = slice(r0, r0 + step)
            kn = _half_rms(k_ref[rows, :].astype(F32), kn_ref[...])
            if latent:
                kn = _rope(kn, cosk_ref[rows, :], sink_ref[rows, :])
            else:
                ko_ref[rows, :] = kn
                vo_ref[rows, :] = v_ref[rows, :].astype(F32)
            kt_s[:, rows] = kn.T.astype(BF16)
            vx_s[rows, 0:hd] = v_ref[rows, :]
        if latent:
            kt_s[:, seq:seq + past] = kc_ref[...].T.astype(BF16)
            vx_s[seq:seq + past, 0:hd] = vc_ref[...].astype(BF16)

    dl = dl_ref[...]
    lam = (jnp.exp(jnp.sum(dl[0:1] * dl[1:2], axis=-1, keepdims=True))
           - jnp.exp(jnp.sum(dl[2:3] * dl[3:4], axis=-1, keepdims=True)) + lam_init)
    for r0 in range(0, q_ref.shape[0], sub):
        rows = slice(r0, r0 + sub)
        qn = _half_rms(q_ref[rows, :].astype(F32), qn_ref[...])
        if latent:
            qn = _rope(qn, cosq_ref[rows, :], sinq_ref[rows, :])
        qb = (qn * (DA_HALF ** -0.5 * LOG2E)).astype(BF16)
        outs = []
        for m in range(2):
            ms = slice(m * DA_HALF, (m + 1) * DA_HALF)
            s = jnp.dot(qb[:, ms], kt_s[ms, :], preferred_element_type=F32)
            p = jnp.exp2(s - jnp.max(s, axis=-1, keepdims=True)).astype(BF16)
            outs.append(jnp.dot(p, vx_s[...], preferred_element_type=F32))
        o = outs[0][:, :hd] / outs[0][:, hd:] - lam * (outs[1][:, :hd] / outs[1][:, hd:])
        o = o * lax.rsqrt(jnp.mean(o * o, axis=-1, keepdims=True) + EPS) * sub_ref[...] * (1.0 - lam_init)
        o_ref[rows, :] = o.astype(BF16)


def _attn(u, nseq, seq, heads, lam_init, q_norm, k_norm, da_lam, subln, ctx=None, rope=None):
    latent = ctx is not None
    hd = DA_VDIM
    sub = 256
    tq = min(seq, 4 * sub)
    nq = seq // tq
    col0 = (u.shape[1] - 3 * heads * hd) // hd
    past = ctx[0].shape[2] if latent else 0
    q_spec = pl.BlockSpec((tq, hd), lambda b, h, i: (b * nq + i, col0 + h))
    k_spec = pl.BlockSpec((seq, hd), lambda b, h, i: (b, col0 + heads + h))
    v_spec = pl.BlockSpec((seq, hd), lambda b, h, i: (b, col0 + 2 * heads + h))
    small = lambda r, c: pl.BlockSpec((r, c), lambda b, h, i: (0, 0))
    par_specs = [small(1, hd), small(1, hd), small(4, DA_HALF), small(1, hd)]
    pars = (q_norm.reshape(1, hd), k_norm.reshape(1, hd), da_lam, subln.reshape(1, hd))
    o_spec = pl.BlockSpec((tq, hd), lambda b, h, i: (b * nq + i, h))
    o_shape = jax.ShapeDtypeStruct((nseq * seq, heads * hd), BF16)
    if latent:
        kc, vc, layer = ctx
        cos, sin = rope
        c_spec = pl.BlockSpec((None, None, past, hd), lambda b, h, i: (b, layer, 0, h))
        tq_spec = pl.BlockSpec((tq, hd), lambda b, h, i: (i, 0))
        tk_spec = pl.BlockSpec((seq, hd), lambda b, h, i: (0, 0))
        in_specs = [q_spec, k_spec, v_spec, c_spec, c_spec, tq_spec, tq_spec, tk_spec, tk_spec] + par_specs
        args = (u, u, u, kc, vc, cos, sin, cos, sin) + pars
        out_specs, out_shape = o_spec, o_shape
    else:
        in_specs = [q_spec, k_spec, v_spec] + par_specs
        args = (u, u, u) + pars
        kv_spec = pl.BlockSpec((seq, hd), lambda b, h, i: (b, h))
        kv_shape = jax.ShapeDtypeStruct((nseq * seq, heads * hd), F32)
        out_specs, out_shape = (o_spec, kv_spec, kv_spec), (o_shape, kv_shape, kv_shape)
    return pl.pallas_call(
        functools.partial(_attn_kernel, latent=latent, seq=seq, past=past, lam_init=lam_init, sub=sub),
        grid=(nseq, heads, nq),
        in_specs=in_specs,
        out_specs=out_specs,
        out_shape=out_shape,
        scratch_shapes=[pltpu.VMEM((hd, seq + past), BF16), pltpu.VMEM((seq + past, 2 * hd), BF16)],
        compiler_params=_cparams("parallel", "parallel", "arbitrary"),
        name="attn",
    )(*args)


def _split3(x):
    hi = x.astype(BF16)
    r1 = x - hi.astype(F32)
    mid = r1.astype(BF16)
    lo = (r1 - mid.astype(F32)).astype(BF16)
    return hi, mid, lo


def _dot(a, b):
    return jnp.dot(a, b, preferred_element_type=F32)


def _ssd_kernel(*refs, seq, has_h0):
    (z_ref, x_ref, b_ref, c_ref, dtc_ref, dtr_ref, dbc_ref, alc_ref, dbr_ref, alr_ref,
     cwx_ref, cwb_ref, cwc_ref, cbx_ref, cbb_ref, cbc_ref, dsk_ref, nw_ref) = refs[:18]
    rest = refs[18:]
    if has_h0:
        h0_ref, rest = rest[0], rest[1:]
    (y_ref, st_ref, pad_s, padb_s, padc_s, xs_s, bt_s, cs_s, ys_s, st_s, exp_s, slot_s, half_s,
     csr_s, dtot_s, new_s, ent_s) = rest

    q = SSD_CHUNK
    nslot = 2 * SSD_HPG
    pair_w = 2 * SSD_HEADDIM
    gw = xs_s.shape[1]

    def put_x(r0, v):
        xs_s[r0:r0 + CONV_BLK, :] = jax.nn.silu(v)

    def put_b(r0, v):
        bt_s[:, r0:r0 + CONV_BLK] = jax.nn.silu(v).T

    def put_c(r0, v):
        cs_s[r0:r0 + CONV_BLK, :] = jax.nn.silu(v).astype(BF16)

    _conv_blocks(x_ref, cwx_ref, cbx_ref, pad_s, seq, gw, put_x)
    _conv_blocks(b_ref, cwb_ref, cbb_ref, padb_s, seq, SSD_STATE, put_b)
    _conv_blocks(c_ref, cwc_ref, cbc_ref, padc_s, seq, SSD_STATE, put_c)

    li = lax.broadcasted_iota(jnp.int32, (q, q), 0)
    si = lax.broadcasted_iota(jnp.int32, (q, q), 1)
    lower = (li >= si)
    upper = (li <= si)
    lower_b = lower.astype(BF16)
    upper_b = upper.astype(BF16)
    one_b = jnp.ones((q, q), BF16)
    zero_b = jnp.zeros((q, q), BF16)
    a_rep = -jnp.exp(alc_ref[...])
    a_row = -jnp.exp(alr_ref[...])
    for s in range(nslot):
        slot_s[s] = (si % nslot == s).astype(BF16)
    for i in range(2):
        half_s[i] = (si // SSD_HEADDIM == i).astype(BF16)
    ki = lax.broadcasted_iota(jnp.int32, (q, gw), 0)
    ji = lax.broadcasted_iota(jnp.int32, (q, gw), 1)
    for d in range(2):
        exp_s[d] = (ki == d * SSD_HPG + ji // SSD_HEADDIM).astype(BF16)

    fwd_lane = (si % nslot) < SSD_HPG
    fwd_row = lax.broadcasted_iota(jnp.int32, (nslot, q), 0) < SSD_HPG
    diag = (li == si)

    def local_pass(c, carry):
        rows = pl.ds(pl.multiple_of(c * q, q), q)
        dt_rep = jax.nn.softplus(dtc_ref[rows, :] + dbc_ref[...])
        da_rep = dt_rep * a_rep
        dt_row = jax.nn.softplus(dtr_ref[c] + dbr_ref[...])
        da_row = dt_row * a_row
        csf_rep = sum(_dot(lower_b, part) for part in _split3(da_rep))
        csf_row = sum(_dot(part, upper_b) for part in _split3(da_row))
        tot_rep = csf_rep[q - 1:q, :]
        tot_row = csf_row[:, q - 1:q]
        cs_rep = jnp.where(fwd_lane, csf_rep, tot_rep - csf_rep + da_rep)
        cs_row = jnp.where(fwd_row, csf_row, tot_row - csf_row + da_row)
        csr_s[rows, :] = cs_rep
        et = jnp.broadcast_to(jnp.exp(tot_rep), (SUBLANE, q))
        et_hi = et.astype(BF16)
        et_mid = (et - et_hi.astype(F32)).astype(BF16)
        for d in range(2):
            dtot_s[d, c] = _dot(et_hi, exp_s[d]) + _dot(et_mid, exp_s[d])
        w_row = dt_row * jnp.exp(tot_row - cs_row)
        c_hi = cs_rep.astype(BF16).astype(F32)
        c_mid = (cs_rep - c_hi).astype(BF16).astype(F32)
        c_lo = cs_rep - c_hi - c_mid
        lhs_all = jnp.where(si < nslot, c_hi, jnp.where(si < 2 * nslot, c_mid, jnp.where(
            si < 3 * nslot, c_lo, jnp.where(si < 6 * nslot, 1.0, 0.0)))).astype(BF16)
        r_hi, r_mid, r_lo = _split3(jnp.log(dt_row) - cs_row)
        rhs = jnp.concatenate([one_b[:3 * nslot], r_hi, r_mid, r_lo, zero_b[:2 * nslot]], axis=0)
        cc = cs_s[rows, :]
        bt = bt_s[:, rows]
        cb = _dot(cc, bt.astype(BF16))
        xb = xs_s[rows, :].astype(BF16)
        for pr in range(SSD_HPG // 2):
            ps = slice(pr * pair_w, (pr + 1) * pair_w)
            acc = None
            for hh in range(2):
                sf = 2 * pr + hh
                sb = SSD_HPG + sf
                seg_f = _dot(lhs_all * slot_s[sf], rhs)
                seg_b = _dot(lhs_all * slot_s[sb], rhs)
                dec = jnp.exp(jnp.where(lower, seg_f, seg_b)) + jnp.where(diag, dt_row[sb:sb + 1, :], 0.0)
                m = (cb * dec).astype(BF16)
                btw_f = (bt * w_row[sf:sf + 1, :]).astype(BF16)
                btw_b = (bt * w_row[sb:sb + 1, :]).astype(BF16)
                xh = xb[:, ps] * half_s[hh]
                r = _dot(jnp.concatenate([m, btw_f, btw_b], axis=0), xh)
                acc = r if acc is None else acc + r
            ys_s[rows, ps] = acc[:q]
            new_s[0, c, :, ps] = acc[q:2 * q]
            new_s[1, c, :, ps] = acc[2 * q:]
        return carry

    nchunks = seq // q
    lax.fori_loop(0, nchunks, local_pass, 0)

    for d in range(2):
        for pr in range(SSD_HPG // 2):
            ps = slice(pr * pair_w, (pr + 1) * pair_w)
            if has_h0:
                st_s[:, ps] = h0_ref[d, 2 * pr:2 * pr + 2].reshape(pair_w, SSD_STATE).T
            else:
                st_s[:, ps] = jnp.zeros((SSD_STATE, pair_w), F32)

        def carry_state(i, carry, d=d):
            c = i if d == 0 else nchunks - 1 - i
            st = st_s[...]
            ent_s[d, c] = st.astype(BF16)
            st_s[...] = st * dtot_s[d, c][0:1, :] + new_s[d, c]
            return carry

        lax.fori_loop(0, nchunks, carry_state, 0)
        for pr in range(SSD_HPG // 2):
            ps = slice(pr * pair_w, (pr + 1) * pair_w)
            st_ref[d, 2 * pr:2 * pr + 2] = st_s[:, ps].T.reshape(2, SSD_HEADDIM, SSD_STATE)

    def finish(c, carry):
        rows = pl.ds(pl.multiple_of(c * q, q), q)
        e1 = jnp.exp(csr_s[rows, :])
        e_hi = e1.astype(BF16)
        e_mid = (e1 - e_hi.astype(F32)).astype(BF16)
        cc = cs_s[rows, :]
        y = ys_s[rows, :] + dsk_ref[...] * xs_s[rows, :]
        for d in range(2):
            e1x = _dot(e_hi, exp_s[d]) + _dot(e_mid, exp_s[d])
            y = y + _dot(cc, ent_s[d, c]) * e1x
        y = y * jax.nn.silu(z_ref[rows, :].astype(F32))
        y = y * lax.rsqrt(jnp.mean(y * y, axis=-1, keepdims=True) + EPS) * nw_ref[...]
        y_ref[rows, :] = y.astype(BF16)
        return carry

    lax.fori_loop(0, nchunks, finish, 0)


def _ssd(u, dt, nseq, seq, conv_w, conv_b, dt_bias, a_log, d_skip, norm_w, h0):
    t = nseq * seq
    g = SSD_GROUPS
    e = SSD_HPG
    heads = g * e
    gw = e * SSD_HEADDIM
    d_inner = heads * SSD_HEADDIM
    nc = seq // SSD_CHUNK
    xb0 = d_inner // gw
    bb0 = 2 * d_inner // SSD_STATE
    per_group = lambda v: v.reshape(2, g, e).transpose(1, 0, 2).reshape(g, 2 * e)
    rep = LANE // (2 * e)
    dtc = jnp.tile(dt.reshape(t, 2, g, e).transpose(2, 0, 1, 3).reshape(g, t, 2 * e), (1, 1, rep))
    dtr = dt.reshape(t // SSD_CHUNK, SSD_CHUNK, 2, g, e).transpose(3, 0, 2, 4, 1).reshape(
        g, t // SSD_CHUNK, 2 * e, SSD_CHUNK)
    dbg = per_group(dt_bias)
    alg = per_group(a_log)
    dbg_rep = jnp.tile(dbg, (1, rep)).reshape(g, 1, LANE)
    alg_rep = jnp.tile(alg, (1, rep)).reshape(g, 1, LANE)
    dsk = jnp.repeat(d_skip, SSD_HEADDIM).reshape(1, d_inner)
    cw = lambda width, blk0: pl.BlockSpec((CONV_W, width), lambda b, gi: (0, blk0 + gi))
    cbias = lambda width, blk0: pl.BlockSpec((1, width), lambda b, gi: (0, blk0 + gi))
    st_spec = pl.BlockSpec((None, 2, e, SSD_HEADDIM, SSD_STATE), lambda b, gi: (b, 0, gi, 0, 0))
    in_specs = [
        pl.BlockSpec((seq, gw), lambda b, gi: (b, gi)),
        pl.BlockSpec((seq, gw), lambda b, gi: (b, xb0 + gi)),
        pl.BlockSpec((seq, SSD_STATE), lambda b, gi: (b, bb0 + gi)),
        pl.BlockSpec((seq, SSD_STATE), lambda b, gi: (b, bb0 + g + gi)),
        pl.BlockSpec((None, seq, LANE), lambda b, gi: (gi, b, 0)),
        pl.BlockSpec((None, nc, 2 * e, SSD_CHUNK), lambda b, gi: (gi, b, 0, 0)),
        pl.BlockSpec((None, 1, LANE), lambda b, gi: (gi, 0, 0)),
        pl.BlockSpec((None, 1, LANE), lambda b, gi: (gi, 0, 0)),
        pl.BlockSpec((None, 2 * e, 1), lambda b, gi: (gi, 0, 0)),
        pl.BlockSpec((None, 2 * e, 1), lambda b, gi: (gi, 0, 0)),
        cw(gw, 0), cw(SSD_STATE, d_inner // SSD_STATE), cw(SSD_STATE, d_inner // SSD_STATE + g),
        cbias(gw, 0), cbias(SSD_STATE, d_inner // SSD_STATE), cbias(SSD_STATE, d_inner // SSD_STATE + g),
        cbias(gw, 0), cbias(gw, 0),
    ]
    conv_b2 = conv_b.reshape(1, -1)
    args = [u, u, u, u, dtc, dtr, dbg_rep, alg_rep,
            dbg.reshape(g, 2 * e, 1), alg.reshape(g, 2 * e, 1),
            conv_w, conv_w, conv_w, conv_b2, conv_b2, conv_b2, dsk, norm_w.reshape(1, d_inner)]
    if h0 is not None:
        in_specs.append(st_spec)
        args.append(h0)
    return pl.pallas_call(
        functools.partial(_ssd_kernel, seq=seq, has_h0=h0 is not None),
        grid=(nseq, g),
        in_specs=in_specs,
        out_specs=(pl.BlockSpec((seq, gw), lambda b, gi: (b, gi)), st_spec),
        out_shape=(jax.ShapeDtypeStruct((t, d_inner), BF16),
                   jax.ShapeDtypeStruct((nseq, 2, heads, SSD_HEADDIM, SSD_STATE), F32)),
        scratch_shapes=[
            pltpu.VMEM((seq + 2 * HALO, gw), BF16),
            pltpu.VMEM((seq + 2 * HALO, SSD_STATE), BF16),
            pltpu.VMEM((seq + 2 * HALO, SSD_STATE), BF16),
            pltpu.VMEM((seq, gw), F32),
            pltpu.VMEM((SSD_STATE, seq), F32),
            pltpu.VMEM((seq, SSD_STATE), BF16),
            pltpu.VMEM((seq, gw), F32),
            pltpu.VMEM((SSD_STATE, gw), F32),
            pltpu.VMEM((2, SSD_CHUNK, gw), BF16),
            pltpu.VMEM((2 * e, SSD_CHUNK, LANE), BF16),
            pltpu.VMEM((2, SSD_CHUNK, LANE), BF16),
            pltpu.VMEM((seq, LANE), F32),
            pltpu.VMEM((2, nc, SUBLANE, gw), F32),
            pltpu.VMEM((2, nc, SSD_STATE, gw), F32),
            pltpu.VMEM((2, nc, SSD_STATE, gw), BF16),
        ],
        compiler_params=_cparams("parallel", "parallel"),
        name="ssd",
    )(*args)


def _rope_tables(seq):
    rows = seq // GRID_W
    row = jnp.repeat(jnp.arange(rows), GRID_W).astype(F32)
    col = jnp.tile(jnp.arange(GRID_W), rows).astype(F32)
    quarter = DA_HALF // 4
    inv = ROPE_THETA ** (-jnp.arange(quarter, dtype=F32) / quarter)
    ang_r = row[:, None] * inv
    ang_c = col[:, None] * inv
    ang = jnp.concatenate([ang_r, ang_r, ang_c, ang_c], axis=-1)
    sign = jnp.tile(jnp.concatenate([-jnp.ones((quarter,), F32), jnp.ones((quarter,), F32)]), 2)
    cos = jnp.cos(ang)
    sin = jnp.sin(ang) * sign
    return jnp.tile(cos, (1, 2)), jnp.tile(sin, (1, 2))


def kernel(x_prompt, x_sample, c, cache_attn_k, cache_attn_v, state_lru, state_ssd, c_ctx, w_ada, b_ada, norm_g, lru_conv_w, lru_conv_b, lru_w_r, lru_b_r, lru_w_i, lru_b_i, lru_lambda, even_w_in, even_w_out, da_q_norm, da_k_norm, da_lambda, da_subln, ssd_w_in, ssd_conv_w, ssd_conv_b, ssd_dt_bias, ssd_a_log, ssd_d, ssd_norm_w, ssd_w_out, ffn_w_in, ffn_w_out):
    depth = w_ada.shape[0]
    batch, seq_p, d = x_prompt.shape
    dec_batch, seq_s, _ = x_sample.shape
    past = cache_attn_k.shape[2]
    heads = cache_attn_k.shape[3]
    d_rnn = lru_conv_w.shape[2]
    d_ff = ffn_w_out.shape[1]
    d_inner = ssd_w_out.shape[1]
    n_even = even_w_in.shape[0]

    cond = jnp.zeros((N_COND, d), F32).at[0].set(c_ctx).at[1:1 + dec_batch].set(c)
    mod_all = _ada(cond, w_ada, b_ada).reshape(depth, N_COND, 6, d)
    rope = _rope_tables(seq_s)
    cache_k = cache_attn_k.reshape(dec_batch, n_even, past, heads * DA_VDIM)
    cache_v = cache_attn_v.reshape(dec_batch, n_even, past, heads * DA_VDIM)

    xs = [x_prompt.reshape(batch * seq_p, d), x_sample.reshape(dec_batch * seq_s, d)]
    shapes = [(batch, seq_p), (dec_batch, seq_s)]
    new_k, new_v, new_lru, new_ssd = [], [], [], []

    for i in range(depth):
        j = i // 2
        mods = [mod_all[i, 0:1], mod_all[i, 1:1 + dec_batch]]
        g_mix = norm_g[i, 0].reshape(1, d)
        g_ffn = norm_g[i, 1].reshape(1, d)
        if i % 2 == 0:
            lam_init = 0.8 - 0.6 * math.exp(-0.3 * i)
            w_in = even_w_in[j].astype(BF16)
            w_out = even_w_out[j].astype(BF16)
            w_r = lru_w_r[j].astype(BF16)
            w_i = lru_w_i[j].astype(BF16)
            for s in range(2):
                nseq, seq = shapes[s]
                u = _inproj(xs[s], g_mix, mods[s], w_in, mode="plain", n_out=w_in.shape[1], tn=1024,
                            shift_idx=0, scale_idx=1, name="even_in")
                h0 = jnp.zeros((nseq, 2, d_rnn), F32) if s == 0 else state_lru[:, j]
                rec, s_fin = _lru(u, nseq, seq, lru_conv_w[j], lru_conv_b[j], w_r, w_i,
                                  lru_b_r[j], lru_b_i[j], lru_lambda[j], h0)
                if s == 0:
                    att, kc, vc = _attn(u, nseq, seq, heads, lam_init, da_q_norm[j], da_k_norm[j],
                                        da_lambda[j], da_subln[j])
                    new_k.append(kc.reshape(nseq, seq, heads, DA_VDIM))
                    new_v.append(vc.reshape(nseq, seq, heads, DA_VDIM))
                    new_lru.append(s_fin)
                else:
                    att = _attn(u, nseq, seq, heads, lam_init, da_q_norm[j], da_k_norm[j],
                                da_lambda[j], da_subln[j], ctx=(cache_k, cache_v, j), rope=rope)
                xs[s] = _outproj([rec, att], w_out, xs[s], mods[s], gate_idx=2, tn=512, name="even_out")
        else:
            w_in = ssd_w_in[j].astype(BF16)
            w_out = ssd_w_out[j].astype(BF16)
            n_main = w_in.shape[1] - 2 * (d_inner // SSD_HEADDIM)
            for s in range(2):
                nseq, seq = shapes[s]
                u, dt = _inproj(xs[s], g_mix, mods[s], w_in, mode="dt", n_out=n_main, tn=1024,
                                shift_idx=0, scale_idx=1, name="odd_in")
                h0 = None if s == 0 else state_ssd[:, j]
                y, st = _ssd(u, dt, nseq, seq, ssd_conv_w[j], ssd_conv_b[j], ssd_dt_bias[j],
                             ssd_a_log[j], ssd_d[j], ssd_norm_w[j], h0)
                if s == 0:
                    new_ssd.append(st)
                xs[s] = _outproj([y], w_out, xs[s], mods[s], gate_idx=2, tn=512, name="odd_out")
        w_in = ffn_w_in[i].astype(BF16)
        w_out = ffn_w_out[i].astype(BF16)
        for s in range(2):
            act = _inproj(xs[s], g_ffn, mods[s], w_in, mode="glu", n_out=d_ff, tn=512,
                          shift_idx=3, scale_idx=4, name="ffn_in")
            xs[s] = _outproj([act], w_out, xs[s], mods[s], gate_idx=5, tn=512, name="ffn_out")

    return (xs[0].reshape(batch, seq_p, d), xs[1].reshape(dec_batch, seq_s, d),
            jnp.stack(new_k, axis=1), jnp.stack(new_v, axis=1),
            jnp.stack(new_lru, axis=1), jnp.stack(new_ssd, axis=1))
```

```python
import functools
import math

import jax
import jax.numpy as jnp
from jax import lax
from jax.experimental import pallas as pl
from jax.experimental.pallas import tpu as pltpu

F32 = jnp.float32
BF16 = jnp.bfloat16

EPS = 1e-6
GRID_W = 64
CONV_W = 4
CONV_LEFT = CONV_W // 2
LRU_BW = 128
LRU_C = 8.0
DA_HALF = 64
DA_VDIM = 2 * DA_HALF
ROPE_THETA = 10000.0
SSD_HEADDIM = 64
SSD_GROUPS = 8
SSD_HPG = 8
SSD_STATE = 128
SSD_CHUNK = 128
N_COND = 16

LANE = 128
SUBLANE = 8
VMEM_LIMIT = 56 * 1024 * 1024

ROW_TILE = 1024
MOD_ROWS = 32
CONV_ROWS = 256
CONV_BLK = 128
HALO = CONV_BLK // 2
LOG2E = 1.4426950408889634


def _cparams(*sem):
    return pltpu.CompilerParams(dimension_semantics=sem, vmem_limit_bytes=VMEM_LIMIT)


def _ada_kernel(c_ref, w_ref, b_ref, o_ref):
    s = jax.nn.silu(c_ref[...]).astype(BF16)
    o_ref[...] = jnp.dot(s, w_ref[...].astype(BF16), preferred_element_type=F32) + b_ref[...]


def _ada(cond, w_ada, b_ada):
    depth, d, n = w_ada.shape
    tn = 1024
    return pl.pallas_call(
        _ada_kernel,
        grid=(depth, n // tn),
        in_specs=[
            pl.BlockSpec((N_COND, d), lambda l, j: (0, 0)),
            pl.BlockSpec((None, d, tn), lambda l, j: (l, 0, j)),
            pl.BlockSpec((None, 1, tn), lambda l, j: (l, 0, j)),
        ],
        out_specs=pl.BlockSpec((None, N_COND, tn), lambda l, j: (l, 0, j)),
        out_shape=jax.ShapeDtypeStruct((depth, N_COND, n), F32),
        compiler_params=_cparams("parallel", "parallel"),
        name="ada",
    )(cond, w_ada, b_ada.reshape(depth, 1, n))


def _modulate_into(x_ref, g_ref, mod_ref, h_ref, shift_idx, scale_idx):
    gs = g_ref[...] * (1.0 + mod_ref[scale_idx:scale_idx + 1, :])
    shift = mod_ref[shift_idx:shift_idx + 1, :]

    def body(r, carry):
        rows = pl.ds(pl.multiple_of(r * MOD_ROWS, MOD_ROWS), MOD_ROWS)
        x = x_ref[rows, :]
        ms = jnp.mean(x * x, axis=-1, keepdims=True)
        h_ref[rows, :] = (x * lax.rsqrt(ms + EPS) * gs + shift).astype(BF16)
        return carry

    lax.fori_loop(0, x_ref.shape[0] // MOD_ROWS, body, 0)


def _inproj_kernel(*refs, mode, shift_idx, scale_idx):
    if mode == "plain":
        x_ref, g_ref, mod_ref, w_ref, o_ref, h_ref = refs
    elif mode == "glu":
        x_ref, g_ref, mod_ref, wg_ref, wu_ref, o_ref, h_ref = refs
    else:
        x_ref, g_ref, mod_ref, w_ref, wdt_ref, o_ref, dt_ref, h_ref = refs

    @pl.when(pl.program_id(1) == 0)
    def _():
        _modulate_into(x_ref, g_ref, mod_ref, h_ref, shift_idx, scale_idx)
        if mode == "dt":
            dt_ref[...] = jnp.dot(h_ref[...], wdt_ref[...], preferred_element_type=F32)

    h = h_ref[...]
    if mode == "glu":
        g = jnp.dot(h, wg_ref[...], preferred_element_type=F32)
        u = jnp.dot(h, wu_ref[...], preferred_element_type=F32)
        o_ref[...] = (jax.nn.silu(g) * u).astype(o_ref.dtype)
    else:
        o_ref[...] = jnp.dot(h, w_ref[...], preferred_element_type=F32).astype(o_ref.dtype)


def _inproj(x, g, mod, w, layer, *, mode, n_out, tn, shift_idx, scale_idx, name, w_dt=None):
    t, d = x.shape
    tm = ROW_TILE
    rows_per_cond = t // mod.shape[0]
    nj = n_out // tn
    x_spec = pl.BlockSpec((tm, d), lambda i, j: (i, 0))
    g_spec = pl.BlockSpec((1, d), lambda i, j: (0, 0))
    mod_spec = pl.BlockSpec((None, 6, d), lambda i, j: ((i * tm) // rows_per_cond, 0, 0))
    w_spec = pl.BlockSpec((None, d, tn), lambda i, j: (layer, 0, j))
    o_spec = pl.BlockSpec((tm, tn), lambda i, j: (i, j))
    o_shape = jax.ShapeDtypeStruct((t, n_out), BF16)
    if mode == "plain":
        in_specs = [x_spec, g_spec, mod_spec, w_spec]
        args = (x, g, mod, w)
        out_specs, out_shape = o_spec, o_shape
    elif mode == "glu":
        in_specs = [x_spec, g_spec, mod_spec, w_spec,
                    pl.BlockSpec((None, d, tn), lambda i, j: (layer, 0, nj + j))]
        args = (x, g, mod, w, w)
        out_specs, out_shape = o_spec, o_shape
    else:
        in_specs = [x_spec, g_spec, mod_spec, w_spec,
                    pl.BlockSpec((None, d, LANE), lambda i, j: (layer, 0, 0))]
        args = (x, g, mod, w, w_dt)
        out_specs = (o_spec, pl.BlockSpec((tm, LANE), lambda i, j: (i, 0)))
        out_shape = (o_shape, jax.ShapeDtypeStruct((t, LANE), F32))
    return pl.pallas_call(
        functools.partial(_inproj_kernel, mode=mode, shift_idx=shift_idx, scale_idx=scale_idx),
        grid=(t // tm, nj),
        in_specs=in_specs,
        out_specs=out_specs,
        out_shape=out_shape,
        scratch_shapes=[pltpu.VMEM((tm, d), BF16)],
        compiler_params=_cparams("parallel", "arbitrary"),
        name=name,
    )(*args)


def _outproj_kernel(*refs, n_a, gate_idx):
    a_refs, w_refs = refs[:n_a], refs[n_a:2 * n_a]
    x_ref, mod_ref, o_ref = refs[2 * n_a:]
    acc = jnp.dot(a_refs[0][...], w_refs[0][...], preferred_element_type=F32)
    for a_ref, w_ref in zip(a_refs[1:], w_refs[1:]):
        acc = acc + jnp.dot(a_ref[...], w_ref[...], preferred_element_type=F32)
    o_ref[...] = x_ref[...] + mod_ref[gate_idx:gate_idx + 1, :] * acc


def _outproj(a_list, w, layer, x, mod, *, gate_idx, tn, name):
    t, d = x.shape
    tm = ROW_TILE
    rows_per_cond = t // mod.shape[0]
    n_a = len(a_list)
    k = a_list[0].shape[1]
    in_specs = [pl.BlockSpec((tm, k), lambda i, j: (i, 0)) for _ in a_list]
    in_specs += [pl.BlockSpec((None, k, tn), functools.partial(lambda i, j, q: (layer, q, j), q=q))
                 for q in range(n_a)]
    in_specs += [pl.BlockSpec((tm, tn), lambda i, j: (i, j)),
                 pl.BlockSpec((None, 6, tn), lambda i, j: ((i * tm) // rows_per_cond, 0, j))]
    return pl.pallas_call(
        functools.partial(_outproj_kernel, n_a=n_a, gate_idx=gate_idx),
        grid=(t // tm, d // tn),
        in_specs=in_specs,
        out_specs=pl.BlockSpec((tm, tn), lambda i, j: (i, j)),
        out_shape=jax.ShapeDtypeStruct((t, d), F32),
        compiler_params=_cparams("parallel", "parallel"),
        name=name,
    )(*a_list, *([w] * n_a), x, mod)


def _conv_blocks(src_ref, w_ref, b_ref, pad_ref, seq, width, emit):
    zeros = jnp.zeros((HALO, width), BF16)
    pad_ref[0:HALO, 0:width] = zeros
    pad_ref[HALO + seq:2 * HALO + seq, 0:width] = zeros
    pad_ref[HALO:HALO + seq, 0:width] = src_ref[...]
    taps = [k for k in range(CONV_W) if k != CONV_LEFT]
    r_i = lax.broadcasted_iota(jnp.int32, (len(taps) * CONV_BLK, 2 * CONV_BLK), 0)
    c_i = lax.broadcasted_iota(jnp.int32, (len(taps) * CONV_BLK, 2 * CONV_BLK), 1)
    tap_i = r_i // CONV_BLK
    off = jnp.where(tap_i >= CONV_LEFT, tap_i + 1, tap_i) - CONV_LEFT
    shift = (c_i == HALO + (r_i % CONV_BLK) + off).astype(BF16)
    w = w_ref[...]
    for r0 in range(0, seq, CONV_BLK):
        sh = jnp.dot(shift, pad_ref[r0:r0 + 2 * CONV_BLK, 0:width], preferred_element_type=F32)
        mid = pad_ref[HALO + r0:HALO + r0 + CONV_BLK, 0:width].astype(F32)
        acc = b_ref[...] + w[CONV_LEFT:CONV_LEFT + 1, :] * mid
        for i, k in enumerate(taps):
            acc = acc + w[k:k + 1, :] * sh[i * CONV_BLK:(i + 1) * CONV_BLK]
        emit(r0, acc)


def _lru_kernel(gate_ref, xr_ref, cw_ref, cb_ref, wr_ref, wi_ref, br_ref, bi_ref, lam_ref, h0_ref,
                rec_ref, sfin_ref, pad_s, xc_s, a_s, b_s, hf_s, hb_s, *, seq, width):
    def put_xc(r0, v):
        xc_s[r0:r0 + CONV_BLK, :] = v

    _conv_blocks(xr_ref, cw_ref, cb_ref, pad_s, seq, width, put_xc)
    step = min(seq, CONV_ROWS)
    tiles = step // SUBLANE

    def gates(d):
        sp = jax.nn.softplus(-lam_ref[d:d + 1, :])
        for r0 in range(0, seq, step):
            t0 = r0 // SUBLANE
            for k in range(width // LRU_BW):
                ks = slice(k * LRU_BW, (k + 1) * LRU_BW)
                xk = xc_s[r0:r0 + step, ks]
                xb = xk.astype(BF16)
                r = jax.nn.sigmoid(jnp.dot(xb, wr_ref[d, k], preferred_element_type=F32) + br_ref[d:d + 1, ks])
                gi = jax.nn.sigmoid(jnp.dot(xb, wi_ref[d, k], preferred_element_type=F32) + bi_ref[d:d + 1, ks])
                a = jnp.exp(-LRU_C * r * sp[:, ks])
                b = jnp.sqrt(1.0 - a * a) * (gi * xk)
                a_s[t0:t0 + tiles, :, ks] = a.reshape(tiles, SUBLANE, LRU_BW)
                b_s[t0:t0 + tiles, :, ks] = b.reshape(tiles, SUBLANE, LRU_BW)

    def scan(out_s, h0, reverse):
        ntile = seq // SUBLANE

        def body(i, h):
            j = ntile - 1 - i if reverse else i
            for r in (range(SUBLANE - 1, -1, -1) if reverse else range(SUBLANE)):
                h = a_s[j, r:r + 1, :] * h + b_s[j, r:r + 1, :]
                out_s[j, r:r + 1, :] = h
            return h

        return lax.fori_loop(0, ntile, body, h0)

    gates(0)
    sfin_ref[0:1, :] = scan(hf_s, h0_ref[0:1, :], False)
    gates(1)
    sfin_ref[1:2, :] = scan(hb_s, h0_ref[1:2, :], True)

    for r0 in range(0, seq, step):
        t0 = r0 // SUBLANE
        h = (hf_s[t0:t0 + tiles] + hb_s[t0:t0 + tiles]).reshape(step, width)
        rec_ref[r0:r0 + step, :] = (h * jax.nn.gelu(gate_ref[r0:r0 + step, :].astype(F32))).astype(BF16)


def _lru(u, nseq, seq, conv_w, conv_b, w_r, w_i, b_r, b_i, lam, h0):
    d_rnn = conv_w.shape[1]
    width = 512
    nb = d_rnn // width
    kb = width // LRU_BW
    vec = lambda rows: pl.BlockSpec((rows, width), lambda b, c: (0, c))
    return pl.pallas_call(
        functools.partial(_lru_kernel, seq=seq, width=width),
        grid=(nseq, nb),
        in_specs=[
            pl.BlockSpec((seq, width), lambda b, c: (b, c)),
            pl.BlockSpec((seq, width), lambda b, c: (b, nb + c)),
            vec(CONV_W), vec(1),
            pl.BlockSpec((2, kb, LRU_BW, LRU_BW), lambda b, c: (0, c, 0, 0)),
            pl.BlockSpec((2, kb, LRU_BW, LRU_BW), lambda b, c: (0, c, 0, 0)),
            vec(2), vec(2), vec(2),
            pl.BlockSpec((None, 2, width), lambda b, c: (b, 0, c)),
        ],
        out_specs=(pl.BlockSpec((seq, width), lambda b, c: (b, c)),
                   pl.BlockSpec((None, 2, width), lambda b, c: (b, 0, c))),
        out_shape=(jax.ShapeDtypeStruct((nseq * seq, d_rnn), BF16),
                   jax.ShapeDtypeStruct((nseq, 2, d_rnn), F32)),
        scratch_shapes=[pltpu.VMEM((seq + 2 * HALO, width), BF16), pltpu.VMEM((seq, width), F32)]
        + [pltpu.VMEM((seq // SUBLANE, SUBLANE, width), F32)] * 4,
        compiler_params=_cparams("parallel", "parallel"),
        name="lru",
    )(u, u, conv_w, conv_b.reshape(1, d_rnn), w_r, w_i, b_r, b_i, lam, h0)


def _half_rms(x, gain):
    x2 = x * x
    s0 = jnp.sum(x2[:, :DA_HALF], axis=-1, keepdims=True)
    s1 = jnp.sum(x2[:, DA_HALF:], axis=-1, keepdims=True)
    lane = lax.broadcasted_iota(jnp.int32, x.shape, 1)
    ms = jnp.where(lane < DA_HALF, s0, s1) * (1.0 / DA_HALF)
    return x * lax.rsqrt(ms + EPS) * gain


def _rope(x, cos, sin_signed):
    q = DA_HALF // 4
    lane = lax.broadcasted_iota(jnp.int32, x.shape, 1)
    rot = jnp.where((lane % (2 * q)) < q, pltpu.roll(x, LANE - q, 1), pltpu.roll(x, q, 1))
    return x * cos + rot * sin_signed


def _attn_kernel(*refs, latent, seq, past, lam_init, sub, has_prev):
    if latent:
        (q_ref, k_ref, v_ref, kc_ref, vc_ref, cosq_ref, sinq_ref, cosk_ref, sink_ref,
         qn_ref, kn_ref, dl_ref, sub_ref, o_ref, kt_s, vx_s) = refs
    else:
        if has_prev:
            refs = refs[:7] + refs[9:]
        (q_ref, k_ref, v_ref, qn_ref, kn_ref, dl_ref, sub_ref, o_ref, ko_ref, vo_ref, kt_s, vx_s) = refs
    hd = DA_VDIM

    @pl.when(pl.program_id(2) == 0)
    def _():
        vx_s[:, hd:2 * hd] = jnp.ones((seq + past, hd), BF16)
        step = min(seq, CONV_ROWS)
        for r0 in range(0, seq, step):
            rows = slice(r0, r0 + step)
            kn = _half_rms(k_ref[rows, :].astype(F32), kn_ref[...])
            if latent:
                kn = _rope(kn, cosk_ref[rows, :], sink_ref[rows, :])
            else:
                ko_ref[rows, :] = kn
                vo_ref[rows, :] = v_ref[rows, :].astype(F32)
            kt_s[:, rows] = kn.T.astype(BF16)
            vx_s[rows, 0:hd] = v_ref[rows, :]
        if latent:
            kt_s[:, seq:seq + past] = kc_ref[...].T.astype(BF16)
            vx_s[seq:seq + past, 0:hd] = vc_ref[...].astype(BF16)

    dl = dl_ref[...]
    lam = (jnp.exp(jnp.sum(dl[0:1] * dl[1:2], axis=-1, keepdims=True))
           - jnp.exp(jnp.sum(dl[2:3] * dl[3:4], axis=-1, keepdims=True)) + lam_init)
    for r0 in range(0, q_ref.shape[0], sub):
        rows = slice(r0, r0 + sub)
        qn = _half_rms(q_ref[rows, :].astype(F32), qn_ref[...])
        if latent:
            qn = _rope(qn, cosq_ref[rows, :], sinq_ref[rows, :])
        qb = (qn * (DA_HALF ** -0.5 * LOG2E)).astype(BF16)
        outs = []
        for m in range(2):
            ms = slice(m * DA_HALF, (m + 1) * DA_HALF)
            s = jnp.dot(qb[:, ms], kt_s[ms, :], preferred_element_type=F32)
            p = jnp.exp2(s - jnp.max(s, axis=-1, keepdims=True)).astype(BF16)
            outs.append(jnp.dot(p, vx_s[...], preferred_element_type=F32))
        o = outs[0][:, :hd] / outs[0][:, hd:] - lam * (outs[1][:, :hd] / outs[1][:, hd:])
        o = o * lax.rsqrt(jnp.mean(o * o, axis=-1, keepdims=True) + EPS) * sub_ref[...] * (1.0 - lam_init)
        o_ref[rows, :] = o.astype(BF16)


def _attn(u, nseq, seq, heads, lam_init, q_norm, k_norm, da_lam, subln, ctx=None, rope=None,
          kv_layers=0, kv_layer=0, kv_prev=None):
    latent = ctx is not None
    hd = DA_VDIM
    sub = 256
    tq = min(seq, 4 * sub)
    nq = seq // tq
    col0 = (u.shape[1] - 3 * heads * hd) // hd
    past = ctx[0].shape[2] if latent else 0
    q_spec = pl.BlockSpec((tq, hd), lambda b, h, i: (b * nq + i, col0 + h))
    k_spec = pl.BlockSpec((seq, hd), lambda b, h, i: (b, col0 + heads + h))
    v_spec = pl.BlockSpec((seq, hd), lambda b, h, i: (b, col0 + 2 * heads + h))
    small = lambda r, c: pl.BlockSpec((r, c), lambda b, h, i: (0, 0))
    par_specs = [small(1, hd), small(1, hd), small(4, DA_HALF), small(1, hd)]
    pars = (q_norm.reshape(1, hd), k_norm.reshape(1, hd), da_lam, subln.reshape(1, hd))
    o_spec = pl.BlockSpec((tq, hd), lambda b, h, i: (b * nq + i, h))
    o_shape = jax.ShapeDtypeStruct((nseq * seq, heads * hd), BF16)
    aliases = {}
    if latent:
        kc, vc, layer = ctx
        cos, sin = rope
        c_spec = pl.BlockSpec((None, None, past, hd), lambda b, h, i: (b, layer, 0, h))
        tq_spec = pl.BlockSpec((tq, hd), lambda b, h, i: (i, 0))
        tk_spec = pl.BlockSpec((seq, hd), lambda b, h, i: (0, 0))
        in_specs = [q_spec, k_spec, v_spec, c_spec, c_spec, tq_spec, tq_spec, tk_spec, tk_spec] + par_specs
        args = (u, u, u, kc, vc, cos, sin, cos, sin) + pars
        out_specs, out_shape = o_spec, o_shape
    else:
        in_specs = [q_spec, k_spec, v_spec] + par_specs
        args = (u, u, u) + pars
        if kv_prev is not None:
            aliases = {len(args): 1, len(args) + 1: 2}
            in_specs += [pl.BlockSpec(memory_space=pl.ANY)] * 2
            args += tuple(kv_prev)
        kv_spec = pl.BlockSpec((None, None, seq, hd), lambda b, h, i: (b, kv_layer, 0, h))
        kv_shape = jax.ShapeDtypeStruct((nseq, kv_layers, seq, heads * hd), F32)
        out_specs, out_shape = (o_spec, kv_spec, kv_spec), (o_shape, kv_shape, kv_shape)
    return pl.pallas_call(
        functools.partial(_attn_kernel, latent=latent, seq=seq, past=past, lam_init=lam_init, sub=sub,
                          has_prev=kv_prev is not None),
        grid=(nseq, heads, nq),
        in_specs=in_specs,
        out_specs=out_specs,
        out_shape=out_shape,
        input_output_aliases=aliases,
        scratch_shapes=[pltpu.VMEM((hd, seq + past), BF16), pltpu.VMEM((seq + past, 2 * hd), BF16)],
        compiler_params=_cparams("parallel", "parallel", "arbitrary"),
        name="attn",
    )(*args)


def _split3(x):
    hi = x.astype(BF16)
    r1 = x - hi.astype(F32)
    mid = r1.astype(BF16)
    lo = (r1 - mid.astype(F32)).astype(BF16)
    return hi, mid, lo


def _dot(a, b):
    return jnp.dot(a, b, preferred_element_type=F32)


def _ssd_kernel(*refs, seq, has_h0, has_prev, has_st):
    (z_ref, x_ref, b_ref, c_ref, dt_ref, dbc_ref, alc_ref, alr_ref,
     cwx_ref, cwb_ref, cwc_ref, cbx_ref, cbb_ref, cbc_ref, dsk_ref, nw_ref) = refs[:16]
    rest = refs[16:]
    if has_h0:
        h0_ref, rest = rest[0], rest[1:]
    if has_prev:
        rest = rest[1:]
    y_ref, rest = rest[0], rest[1:]
    if has_st:
        st_ref, rest = rest[0], rest[1:]
    (pad_s, padb_s, padc_s, xs_s, bt_s, cs_s, ys_s, st_s, exp_s, slot_s, half_s,
     csr_s, dtot_s, new_s, ent_s, sel_s) = rest

    q = SSD_CHUNK
    nslot = 2 * SSD_HPG
    pair_w = 2 * SSD_HEADDIM
    gw = xs_s.shape[1]

    def put_x(r0, v):
        xs_s[r0:r0 + CONV_BLK, :] = jax.nn.silu(v)

    def put_b(r0, v):
        bt_s[:, r0:r0 + CONV_BLK] = jax.nn.silu(v).T

    def put_c(r0, v):
        cs_s[r0:r0 + CONV_BLK, :] = jax.nn.silu(v).astype(BF16)

    _conv_blocks(x_ref, cwx_ref, cbx_ref, pad_s, seq, gw, put_x)
    _conv_blocks(b_ref, cwb_ref, cbb_ref, padb_s, seq, SSD_STATE, put_b)
    _conv_blocks(c_ref, cwc_ref, cbc_ref, padc_s, seq, SSD_STATE, put_c)

    li = lax.broadcasted_iota(jnp.int32, (q, q), 0)
    si = lax.broadcasted_iota(jnp.int32, (q, q), 1)
    lower = (li >= si)
    upper = (li <= si)
    lower_b = lower.astype(BF16)
    upper_b = upper.astype(BF16)
    one_b = jnp.ones((q, q), BF16)
    zero_b = jnp.zeros((q, q), BF16)
    a_rep = -jnp.exp(alc_ref[...])
    a_row = -jnp.exp(alr_ref[...])
    for s in range(nslot):
        slot_s[s] = (li % nslot == s).astype(BF16)
    for i in range(2):
        half_s[i] = (si // SSD_HEADDIM == i).astype(BF16)
    sel_s[...] = (li == pl.program_id(1) * nslot + si % nslot).astype(BF16)
    lower2 = jnp.concatenate([lower, lower], axis=1)
    diag2 = jnp.concatenate([li == si, li == si], axis=1)
    ki = lax.broadcasted_iota(jnp.int32, (q, gw), 0)
    ji = lax.broadcasted_iota(jnp.int32, (q, gw), 1)
    for d in range(2):
        exp_s[d] = ((ki % nslot == d * SSD_HPG + ji // SSD_HEADDIM) & (ki < 3 * nslot)).astype(BF16)

    def terms3(v):
        lane = lax.broadcasted_iota(jnp.int32, v.shape, 1)
        hi = v.astype(BF16).astype(F32)
        mid = (v - hi).astype(BF16).astype(F32)
        lo = v - hi - mid
        return jnp.where(lane < nslot, hi, jnp.where(lane < 2 * nslot, mid, jnp.where(
            lane < 3 * nslot, lo, 0.0))).astype(BF16)

    fwd_lane = (si % nslot) < SSD_HPG
    fwd_row = lax.broadcasted_iota(jnp.int32, (nslot, q), 0) < SSD_HPG

    def cums(c):
        rows = pl.ds(pl.multiple_of(c * q, q), q)
        raw_rep = sum(_dot(part, sel_s[...]) for part in _split3(dt_ref[rows, :]))
        dt_rep = jax.nn.softplus(raw_rep + dbc_ref[...])
        da_rep = dt_rep * a_rep
        dt_row = dt_rep.T[0:nslot]
        da_row = dt_row * a_row
        csf_rep = sum(_dot(lower_b, part) for part in _split3(da_rep))
        csf_row = sum(_dot(part, upper_b) for part in _split3(da_row))
        bt = bt_s[:, rows]
        cb = _dot(cs_s[rows, :], bt.astype(BF16))
        return rows, dt_row, da_rep, da_row, csf_rep, csf_row, bt, cb

    def segments(c, stage1):
        rows, dt_row, da_rep, da_row, csf_rep, csf_row, bt, cb = stage1
        tot_rep = csf_rep[q - 1:q, :]
        tot_row = csf_row[:, q - 1:q]
        cs_rep = jnp.where(fwd_lane, csf_rep, tot_rep - csf_rep + da_rep)
        cs_row = jnp.where(fwd_row, csf_row, tot_row - csf_row + da_row)
        csr_s[rows, :] = cs_rep
        et = terms3(jnp.broadcast_to(jnp.exp(tot_rep), (SUBLANE, q)))
        for d in range(2):
            dtot_s[d, c] = _dot(et, exp_s[d])
        w_row = dt_row * jnp.exp(tot_row - cs_row)
        c_hi = cs_rep.astype(BF16).astype(F32)
        c_mid = (cs_rep - c_hi).astype(BF16).astype(F32)
        c_lo = cs_rep - c_hi - c_mid
        lhs_all = jnp.where(si < nslot, c_hi, jnp.where(si < 2 * nslot, c_mid, jnp.where(
            si < 3 * nslot, c_lo, jnp.where(si < 6 * nslot, 1.0, 0.0)))).astype(BF16)
        r_hi, r_mid, r_lo = _split3(jnp.log(dt_row) - cs_row)
        rhs = jnp.concatenate([one_b[:3 * nslot], r_hi, r_mid, r_lo, zero_b[:2 * nslot]], axis=0)
        segs = []
        for pr in range(SSD_HPG // 2):
            f0, f1 = 2 * pr, 2 * pr + 1
            b0, b1 = SSD_HPG + f0, SSD_HPG + f1
            segs.append((_dot(lhs_all, jnp.concatenate([rhs * slot_s[f0], rhs * slot_s[f1]], axis=1)),
                         _dot(lhs_all, jnp.concatenate([rhs * slot_s[b0], rhs * slot_s[b1]], axis=1))))
        return rows, dt_row, w_row, bt, cb, segs

    def products(c, stage2):
        rows, dt_row, w_row, bt, cb, segs = stage2
        cb2 = jnp.concatenate([cb, cb], axis=1)
        bt2 = jnp.concatenate([bt, bt], axis=1)
        xb = xs_s[rows, :].astype(BF16)
        row2 = lambda v, s0, s1: jnp.concatenate([v[s0:s0 + 1, :], v[s1:s1 + 1, :]], axis=1)
        for pr in range(SSD_HPG // 2):
            ps = slice(pr * pair_w, (pr + 1) * pair_w)
            f0, f1 = 2 * pr, 2 * pr + 1
            b0, b1 = SSD_HPG + f0, SSD_HPG + f1
            seg_f, seg_b = segs[pr]
            dec = jnp.exp(jnp.where(lower2, seg_f, seg_b)) + jnp.where(diag2, row2(dt_row, b0, b1), 0.0)
            m = (cb2 * dec).astype(BF16)
            btw_f = (bt2 * row2(w_row, f0, f1)).astype(BF16)
            btw_b = (bt2 * row2(w_row, b0, b1)).astype(BF16)
            xp = xb[:, ps]
            xh = jnp.concatenate([xp * half_s[0], xp * half_s[1]], axis=0)
            r = _dot(jnp.concatenate([m, btw_f, btw_b], axis=0), xh)
            ys_s[rows, ps] = r[:q]
            new_s[0, c, :, ps] = r[q:2 * q]
            new_s[1, c, :, ps] = r[2 * q:]

    def local_pass(i, carry):
        pair = (2 * i, 2 * i + 1)
        stage1 = [cums(c) for c in pair]
        stage2 = [segments(c, s) for c, s in zip(pair, stage1)]
        for c, s in zip(pair, stage2):
            products(c, s)
        return carry

    nchunks = seq // q
    lax.fori_loop(0, nchunks // 2, local_pass, 0)

    for d in range(2):
        for pr in range(SSD_HPG // 2):
            ps = slice(pr * pair_w, (pr + 1) * pair_w)
            if has_h0:
                st_s[:, ps] = h0_ref[d, 2 * pr:2 * pr + 2].reshape(pair_w, SSD_STATE).T
            else:
                st_s[:, ps] = jnp.zeros((SSD_STATE, pair_w), F32)

        def carry_state(i, carry, d=d):
            c = i if d == 0 else nchunks - 1 - i
            st = st_s[...]
            ent_s[d, c] = st.astype(BF16)
            st_s[...] = st * dtot_s[d, c][0:1, :] + new_s[d, c]
            return carry

        lax.fori_loop(0, nchunks, carry_state, 0, unroll=2)
        for pr in range(SSD_HPG // 2 if has_st else 0):
            ps = slice(pr * pair_w, (pr + 1) * pair_w)
            st_ref[d, 2 * pr:2 * pr + 2] = st_s[:, ps].T.reshape(2, SSD_HEADDIM, SSD_STATE)

    def finish(c, carry):
        rows = pl.ds(pl.multiple_of(c * q, q), q)
        e1 = terms3(jnp.exp(csr_s[rows, :]))
        cc = cs_s[rows, :]
        y = ys_s[rows, :] + dsk_ref[...] * xs_s[rows, :]
        for d in range(2):
            y = y + _dot(cc, ent_s[d, c]) * _dot(e1, exp_s[d])
        y = y * jax.nn.silu(z_ref[rows, :].astype(F32))
        y = y * lax.rsqrt(jnp.mean(y * y, axis=-1, keepdims=True) + EPS) * nw_ref[...]
        y_ref[rows, :] = y.astype(BF16)
        return carry

    lax.fori_loop(0, nchunks, finish, 0, unroll=2)


def _ssd(u, dt, nseq, seq, conv_w, conv_b, dt_bias, a_log, d_skip, norm_w, *,
         h0=None, h0_layer=0, st_layers=0, st_layer=0, st_prev=None):
    t = nseq * seq
    g = SSD_GROUPS
    e = SSD_HPG
    heads = g * e
    gw = e * SSD_HEADDIM
    d_inner = heads * SSD_HEADDIM
    nc = seq // SSD_CHUNK
    xb0 = d_inner // gw
    bb0 = 2 * d_inner // SSD_STATE
    per_group = lambda v: v.reshape(2, g, e).transpose(1, 0, 2).reshape(g, 2 * e)
    rep = LANE // (2 * e)
    dbg = per_group(dt_bias)
    alg = per_group(a_log)
    dbg_rep = jnp.tile(dbg, (1, rep)).reshape(g, 1, LANE)
    alg_rep = jnp.tile(alg, (1, rep)).reshape(g, 1, LANE)
    dsk = jnp.repeat(d_skip, SSD_HEADDIM).reshape(1, d_inner)
    cw = lambda width, blk0: pl.BlockSpec((CONV_W, width), lambda b, gi: (0, blk0 + gi))
    cbias = lambda width, blk0: pl.BlockSpec((1, width), lambda b, gi: (0, blk0 + gi))
    st_block = (None, None, 2, e, SSD_HEADDIM, SSD_STATE)
    in_specs = [
        pl.BlockSpec((seq, gw), lambda b, gi: (b, gi)),
        pl.BlockSpec((seq, gw), lambda b, gi: (b, xb0 + gi)),
        pl.BlockSpec((seq, SSD_STATE), lambda b, gi: (b, bb0 + gi)),
        pl.BlockSpec((seq, SSD_STATE), lambda b, gi: (b, bb0 + g + gi)),
        pl.BlockSpec((seq, LANE), lambda b, gi: (b, 0)),
        pl.BlockSpec((None, 1, LANE), lambda b, gi: (gi, 0, 0)),
        pl.BlockSpec((None, 1, LANE), lambda b, gi: (gi, 0, 0)),
        pl.BlockSpec((None, 2 * e, 1), lambda b, gi: (gi, 0, 0)),
        cw(gw, 0), cw(SSD_STATE, d_inner // SSD_STATE), cw(SSD_STATE, d_inner // SSD_STATE + g),
        cbias(gw, 0), cbias(SSD_STATE, d_inner // SSD_STATE), cbias(SSD_STATE, d_inner // SSD_STATE + g),
        cbias(gw, 0), cbias(gw, 0),
    ]
    conv_b2 = conv_b.reshape(1, -1)
    args = [u, u, u, u, dt, dbg_rep, alg_rep, alg.reshape(g, 2 * e, 1),
            conv_w, conv_w, conv_w, conv_b2, conv_b2, conv_b2, dsk, norm_w.reshape(1, d_inner)]
    if h0 is not None:
        in_specs.append(pl.BlockSpec(st_block, lambda b, gi: (b, h0_layer, 0, gi, 0, 0)))
        args.append(h0)
    out_specs = [pl.BlockSpec((seq, gw), lambda b, gi: (b, gi))]
    out_shape = [jax.ShapeDtypeStruct((t, d_inner), BF16)]
    aliases = {}
    if st_layers:
        if st_prev is not None:
            aliases = {len(args): 1}
            in_specs.append(pl.BlockSpec(memory_space=pl.ANY))
            args.append(st_prev)
        out_specs.append(pl.BlockSpec(st_block, lambda b, gi: (b, st_layer, 0, gi, 0, 0)))
        out_shape.append(jax.ShapeDtypeStruct((nseq, st_layers, 2, heads, SSD_HEADDIM, SSD_STATE), F32))
    return pl.pallas_call(
        functools.partial(_ssd_kernel, seq=seq, has_h0=h0 is not None, has_prev=st_prev is not None,
                          has_st=bool(st_layers)),
        grid=(nseq, g),
        in_specs=in_specs,
        out_specs=out_specs,
        out_shape=out_shape,
        input_output_aliases=aliases,
        scratch_shapes=[
            pltpu.VMEM((seq + 2 * HALO, gw), BF16),
            pltpu.VMEM((seq + 2 * HALO, SSD_STATE), BF16),
            pltpu.VMEM((seq + 2 * HALO, SSD_STATE), BF16),
            pltpu.VMEM((seq, gw), F32),
            pltpu.VMEM((SSD_STATE, seq), F32),
            pltpu.VMEM((seq, SSD_STATE), BF16),
            pltpu.VMEM((seq, gw), F32),
            pltpu.VMEM((SSD_STATE, gw), F32),
            pltpu.VMEM((2, SSD_CHUNK, gw), BF16),
            pltpu.VMEM((2 * e, SSD_CHUNK, LANE), BF16),
            pltpu.VMEM((2, SSD_CHUNK, LANE), BF16),
            pltpu.VMEM((seq, LANE), F32),
            pltpu.VMEM((2, nc, SUBLANE, gw), F32),
            pltpu.VMEM((2, nc, SSD_STATE, gw), F32),
            pltpu.VMEM((2, nc, SSD_STATE, gw), BF16),
            pltpu.VMEM((SSD_CHUNK, LANE), BF16),
        ],
        compiler_params=_cparams("parallel", "parallel"),
        name="ssd",
    )(*args)


def _rope_tables(seq):
    rows = seq // GRID_W
    row = jnp.repeat(jnp.arange(rows), GRID_W).astype(F32)
    col = jnp.tile(jnp.arange(GRID_W), rows).astype(F32)
    quarter = DA_HALF // 4
    inv = ROPE_THETA ** (-jnp.arange(quarter, dtype=F32) / quarter)
    ang_r = row[:, None] * inv
    ang_c = col[:, None] * inv
    ang = jnp.concatenate([ang_r, ang_r, ang_c, ang_c], axis=-1)
    sign = jnp.tile(jnp.concatenate([-jnp.ones((quarter,), F32), jnp.ones((quarter,), F32)]), 2)
    cos = jnp.cos(ang)
    sin = jnp.sin(ang) * sign
    return jnp.tile(cos, (1, 2)), jnp.tile(sin, (1, 2))


def kernel(x_prompt, x_sample, c, cache_attn_k, cache_attn_v, state_lru, state_ssd, c_ctx, w_ada, b_ada, norm_g, lru_conv_w, lru_conv_b, lru_w_r, lru_b_r, lru_w_i, lru_b_i, lru_lambda, even_w_in, even_w_out, da_q_norm, da_k_norm, da_lambda, da_subln, ssd_w_in, ssd_conv_w, ssd_conv_b, ssd_dt_bias, ssd_a_log, ssd_d, ssd_norm_w, ssd_w_out, ffn_w_in, ffn_w_out):
    depth = w_ada.shape[0]
    batch, seq_p, d = x_prompt.shape
    dec_batch, seq_s, _ = x_sample.shape
    past = cache_attn_k.shape[2]
    heads = cache_attn_k.shape[3]
    d_rnn = lru_conv_w.shape[2]
    d_ff = ffn_w_out.shape[1]
    d_inner = ssd_w_out.shape[1]
    n_even = even_w_in.shape[0]

    cond = jnp.zeros((N_COND, d), F32).at[0].set(c_ctx).at[1:1 + dec_batch].set(c)
    mod_all = _ada(cond, w_ada, b_ada).reshape(depth, N_COND, 6, d)
    rope = _rope_tables(seq_s)
    cache_k = cache_attn_k.reshape(dec_batch, n_even, past, heads * DA_VDIM)
    cache_v = cache_attn_v.reshape(dec_batch, n_even, past, heads * DA_VDIM)

    even_in_b, even_out_b = even_w_in.astype(BF16), even_w_out.astype(BF16)
    n_odd = ssd_w_in.shape[0]
    ssd_heads = d_inner // SSD_HEADDIM
    n_main = ssd_w_in.shape[2] - 2 * ssd_heads
    ssd_in_b, ssd_out_b = ssd_w_in.astype(BF16), ssd_w_out.astype(BF16)
    hpg = ssd_heads // SSD_GROUPS
    dt_cols = n_main + jnp.arange(2 * ssd_heads).reshape(2, SSD_GROUPS, hpg).transpose(1, 0, 2).reshape(-1)
    ssd_dt_b = ssd_w_in[:, :, dt_cols].astype(BF16)
    ffn_in_b, ffn_out_b = ffn_w_in.astype(BF16), ffn_w_out.astype(BF16)

    xs = [x_prompt.reshape(batch * seq_p, d), x_sample.reshape(dec_batch * seq_s, d)]
    shapes = [(batch, seq_p), (dec_batch, seq_s)]
    new_kv, new_lru, new_ssd = None, [], None

    for i in range(depth):
        j = i // 2
        mods = [mod_all[i, 0:1], mod_all[i, 1:1 + dec_batch]]
        g_mix = norm_g[i, 0].reshape(1, d)
        g_ffn = norm_g[i, 1].reshape(1, d)
        if i % 2 == 0:
            lam_init = 0.8 - 0.6 * math.exp(-0.3 * i)
            w_r = lru_w_r[j].astype(BF16)
            w_i = lru_w_i[j].astype(BF16)
            for s in range(2):
                nseq, seq = shapes[s]
                u = _inproj(xs[s], g_mix, mods[s], even_in_b, j, mode="plain", n_out=even_in_b.shape[2],
                            tn=1024, shift_idx=0, scale_idx=1, name="even_in")
                h0 = jnp.zeros((nseq, 2, d_rnn), F32) if s == 0 else state_lru[:, j]
                rec, s_fin = _lru(u, nseq, seq, lru_conv_w[j], lru_conv_b[j], w_r, w_i,
                                  lru_b_r[j], lru_b_i[j], lru_lambda[j], h0)
                if s == 0:
                    att, kc, vc = _attn(u, nseq, seq, heads, lam_init, da_q_norm[j], da_k_norm[j],
                                        da_lambda[j], da_subln[j], kv_layers=n_even, kv_layer=j,
                                        kv_prev=new_kv)
                    new_kv = (kc, vc)
                    new_lru.append(s_fin)
                else:
                    att = _attn(u, nseq, seq, heads, lam_init, da_q_norm[j], da_k_norm[j],
                                da_lambda[j], da_subln[j], ctx=(cache_k, cache_v, j), rope=rope)
                xs[s] = _outproj([rec, att], even_out_b, j, xs[s], mods[s], gate_idx=2, tn=512,
                                 name="even_out")
        else:
            for s in range(2):
                nseq, seq = shapes[s]
                u, dt = _inproj(xs[s], g_mix, mods[s], ssd_in_b, j, mode="dt", n_out=n_main, tn=1024,
                                shift_idx=0, scale_idx=1, name="odd_in", w_dt=ssd_dt_b)
                ssd_args = (u, dt, nseq, seq, ssd_conv_w[j], ssd_conv_b[j], ssd_dt_bias[j],
                            ssd_a_log[j], ssd_d[j], ssd_norm_w[j])
                if s == 0:
                    y, new_ssd = _ssd(*ssd_args, st_layers=n_odd, st_layer=j, st_prev=new_ssd)
                else:
                    y, = _ssd(*ssd_args, h0=state_ssd, h0_layer=j)
                xs[s] = _outproj([y], ssd_out_b, j, xs[s], mods[s], gate_idx=2, tn=512, name="odd_out")
        for s in range(2):
            act = _inproj(xs[s], g_ffn, mods[s], ffn_in_b, i, mode="glu", n_out=d_ff, tn=512,
                          shift_idx=3, scale_idx=4, name="ffn_in")
            xs[s] = _outproj([act], ffn_out_b, i, xs[s], mods[s], gate_idx=5, tn=512, name="ffn_out")

    return (xs[0].reshape(batch, seq_p, d), xs[1].reshape(dec_batch, seq_s, d),
            new_kv[0].reshape(batch, n_even, seq_p, heads, DA_VDIM),
            new_kv[1].reshape(batch, n_even, seq_p, heads, DA_VDIM),
            jnp.stack(new_lru, axis=1), new_ssd)
```

```python
import functools
import math

import jax
import jax.numpy as jnp
from jax import lax
from jax.experimental import pallas as pl
from jax.experimental.pallas import tpu as pltpu

F32 = jnp.float32
BF16 = jnp.bfloat16

EPS = 1e-6
GRID_W = 64
CONV_W = 4
CONV_LEFT = CONV_W // 2
LRU_BW = 128
LRU_C = 8.0
DA_HALF = 64
DA_VDIM = 2 * DA_HALF
ROPE_THETA = 10000.0
SSD_HEADDIM = 64
SSD_GROUPS = 8
SSD_HPG = 8
SSD_STATE = 128
SSD_CHUNK = 128
N_COND = 16

LANE = 128
SUBLANE = 8
VMEM_LIMIT = 56 * 1024 * 1024

ROW_TILE = 1024
MOD_ROWS = 128
CONV_ROWS = 256
CONV_BLK = 128
HALO = CONV_BLK // 2
LOG2E = 1.4426950408889634


def _cparams(*sem):
    return pltpu.CompilerParams(dimension_semantics=sem, vmem_limit_bytes=VMEM_LIMIT)


def _ada_kernel(c_ref, w_ref, b_ref, o_ref):
    s = jax.nn.silu(c_ref[...]).astype(BF16)
    o_ref[...] = jnp.dot(s, w_ref[...].astype(BF16), preferred_element_type=F32) + b_ref[...]


def _ada(cond, w_ada, b_ada):
    depth, d, n = w_ada.shape
    tn = 1024
    return pl.pallas_call(
        _ada_kernel,
        grid=(depth, n // tn),
        in_specs=[
            pl.BlockSpec((N_COND, d), lambda l, j: (0, 0)),
            pl.BlockSpec((None, d, tn), lambda l, j: (l, 0, j)),
            pl.BlockSpec((None, 1, tn), lambda l, j: (l, 0, j)),
        ],
        out_specs=pl.BlockSpec((None, N_COND, tn), lambda l, j: (l, 0, j)),
        out_shape=jax.ShapeDtypeStruct((depth, N_COND, n), F32),
        compiler_params=_cparams("parallel", "parallel"),
        name="ada",
    )(cond, w_ada, b_ada.reshape(depth, 1, n))


def _modulate_into(x_ref, g_ref, mod_ref, h_ref, shift_idx, scale_idx):
    gs = g_ref[...] * (1.0 + mod_ref[scale_idx:scale_idx + 1, :])
    shift = mod_ref[shift_idx:shift_idx + 1, :]

    def body(r, carry):
        rows = pl.ds(pl.multiple_of(r * MOD_ROWS, MOD_ROWS), MOD_ROWS)
        x = x_ref[rows, :]
        ms = jnp.mean(x * x, axis=-1, keepdims=True)
        h_ref[rows, :] = (x * lax.rsqrt(ms + EPS) * gs + shift).astype(BF16)
        return carry

    lax.fori_loop(0, x_ref.shape[0] // MOD_ROWS, body, 0)


def _inproj_kernel(*refs, mode, shift_idx, scale_idx):
    if mode == "plain":
        x_ref, g_ref, mod_ref, w_ref, o_ref, h_ref = refs
    elif mode == "glu":
        x_ref, g_ref, mod_ref, wg_ref, wu_ref, o_ref, h_ref = refs
    else:
        x_ref, g_ref, mod_ref, w_ref, wdt_ref, o_ref, dt_ref, h_ref = refs

    @pl.when(pl.program_id(1) == 0)
    def _():
        _modulate_into(x_ref, g_ref, mod_ref, h_ref, shift_idx, scale_idx)
        if mode == "dt":
            dt_ref[...] = jnp.dot(h_ref[...], wdt_ref[...], preferred_element_type=F32)

    h = h_ref[...]
    if mode == "glu":
        g = jnp.dot(h, wg_ref[...], preferred_element_type=F32)
        u = jnp.dot(h, wu_ref[...], preferred_element_type=F32)
        o_ref[...] = (jax.nn.silu(g) * u).astype(o_ref.dtype)
    else:
        o_ref[...] = jnp.dot(h, w_ref[...], preferred_element_type=F32).astype(o_ref.dtype)


def _inproj(x, g, mod, w, layer, *, mode, n_out, tn, shift_idx, scale_idx, name, w_dt=None):
    t, d = x.shape
    tm = ROW_TILE
    rows_per_cond = t // mod.shape[0]
    nj = n_out // tn
    x_spec = pl.BlockSpec((tm, d), lambda i, j: (i, 0))
    g_spec = pl.BlockSpec((1, d), lambda i, j: (0, 0))
    mod_spec = pl.BlockSpec((None, 6, d), lambda i, j: ((i * tm) // rows_per_cond, 0, 0))
    w_spec = pl.BlockSpec((None, d, tn), lambda i, j: (layer, 0, j))
    o_spec = pl.BlockSpec((tm, tn), lambda i, j: (i, j))
    o_shape = jax.ShapeDtypeStruct((t, n_out), BF16)
    if mode == "plain":
        in_specs = [x_spec, g_spec, mod_spec, w_spec]
        args = (x, g, mod, w)
        out_specs, out_shape = o_spec, o_shape
    elif mode == "glu":
        in_specs = [x_spec, g_spec, mod_spec, w_spec,
                    pl.BlockSpec((None, d, tn), lambda i, j: (layer, 0, nj + j))]
        args = (x, g, mod, w, w)
        out_specs, out_shape = o_spec, o_shape
    else:
        in_specs = [x_spec, g_spec, mod_spec, w_spec,
                    pl.BlockSpec((None, d, LANE), lambda i, j: (layer, 0, 0))]
        args = (x, g, mod, w, w_dt)
        out_specs = (o_spec, pl.BlockSpec((tm, LANE), lambda i, j: (i, 0)))
        out_shape = (o_shape, jax.ShapeDtypeStruct((t, LANE), F32))
    return pl.pallas_call(
        functools.partial(_inproj_kernel, mode=mode, shift_idx=shift_idx, scale_idx=scale_idx),
        grid=(t // tm, nj),
        in_specs=in_specs,
        out_specs=out_specs,
        out_shape=out_shape,
        scratch_shapes=[pltpu.VMEM((tm, d), BF16)],
        compiler_params=_cparams("parallel", "arbitrary"),
        name=name,
    )(*args)


def _outproj_kernel(*refs, n_a, gate_idx):
    a_refs, w_refs = refs[:n_a], refs[n_a:2 * n_a]
    x_ref, mod_ref, o_ref = refs[2 * n_a:]
    acc = jnp.dot(a_refs[0][...], w_refs[0][...], preferred_element_type=F32)
    for a_ref, w_ref in zip(a_refs[1:], w_refs[1:]):
        acc = acc + jnp.dot(a_ref[...], w_ref[...], preferred_element_type=F32)
    o_ref[...] = x_ref[...] + mod_ref[gate_idx:gate_idx + 1, :] * acc


def _outproj(a_list, w, layer, x, mod, *, gate_idx, tn, name):
    t, d = x.shape
    tm = ROW_TILE
    rows_per_cond = t // mod.shape[0]
    n_a = len(a_list)
    k = a_list[0].shape[1]
    in_specs = [pl.BlockSpec((tm, k), lambda i, j: (i, 0)) for _ in a_list]
    in_specs += [pl.BlockSpec((None, k, tn), functools.partial(lambda i, j, q: (layer, q, j), q=q))
                 for q in range(n_a)]
    in_specs += [pl.BlockSpec((tm, tn), lambda i, j: (i, j)),
                 pl.BlockSpec((None, 6, tn), lambda i, j: ((i * tm) // rows_per_cond, 0, j))]
    return pl.pallas_call(
        functools.partial(_outproj_kernel, n_a=n_a, gate_idx=gate_idx),
        grid=(t // tm, d // tn),
        in_specs=in_specs,
        out_specs=pl.BlockSpec((tm, tn), lambda i, j: (i, j)),
        out_shape=jax.ShapeDtypeStruct((t, d), F32),
        compiler_params=_cparams("parallel", "parallel"),
        name=name,
    )(*a_list, *([w] * n_a), x, mod)


def _conv_blocks(src_ref, w_ref, b_ref, pad_ref, seq, width, emit):
    zeros = jnp.zeros((HALO, width), BF16)
    pad_ref[0:HALO, 0:width] = zeros
    pad_ref[HALO + seq:2 * HALO + seq, 0:width] = zeros
    pad_ref[HALO:HALO + seq, 0:width] = src_ref[...]
    taps = [k for k in range(CONV_W) if k != CONV_LEFT]
    r_i = lax.broadcasted_iota(jnp.int32, (len(taps) * CONV_BLK, 2 * CONV_BLK), 0)
    c_i = lax.broadcasted_iota(jnp.int32, (len(taps) * CONV_BLK, 2 * CONV_BLK), 1)
    tap_i = r_i // CONV_BLK
    off = jnp.where(tap_i >= CONV_LEFT, tap_i + 1, tap_i) - CONV_LEFT
    shift = (c_i == HALO + (r_i % CONV_BLK) + off).astype(BF16)
    w = w_ref[...]
    for r0 in range(0, seq, CONV_BLK):
        sh = jnp.dot(shift, pad_ref[r0:r0 + 2 * CONV_BLK, 0:width], preferred_element_type=F32)
        mid = pad_ref[HALO + r0:HALO + r0 + CONV_BLK, 0:width].astype(F32)
        acc = b_ref[...] + w[CONV_LEFT:CONV_LEFT + 1, :] * mid
        for i, k in enumerate(taps):
            acc = acc + w[k:k + 1, :] * sh[i * CONV_BLK:(i + 1) * CONV_BLK]
        emit(r0, acc)


def _lru_kernel(gate_ref, xr_ref, cw_ref, cb_ref, wr_ref, wi_ref, br_ref, bi_ref, lam_ref, h0_ref,
                rec_ref, sfin_ref, pad_s, xc_s, a_s, b_s, hf_s, hb_s, *, seq, width):
    def put_xc(r0, v):
        xc_s[r0:r0 + CONV_BLK, :] = v

    _conv_blocks(xr_ref, cw_ref, cb_ref, pad_s, seq, width, put_xc)
    step = min(seq, CONV_ROWS)
    tiles = step // SUBLANE

    def gates(d):
        sp = jax.nn.softplus(-lam_ref[d:d + 1, :])
        for r0 in range(0, seq, step):
            t0 = r0 // SUBLANE
            for k in range(width // LRU_BW):
                ks = slice(k * LRU_BW, (k + 1) * LRU_BW)
                xk = xc_s[r0:r0 + step, ks]
                xb = xk.astype(BF16)
                r = jax.nn.sigmoid(jnp.dot(xb, wr_ref[d, k], preferred_element_type=F32) + br_ref[d:d + 1, ks])
                gi = jax.nn.sigmoid(jnp.dot(xb, wi_ref[d, k], preferred_element_type=F32) + bi_ref[d:d + 1, ks])
                a = jnp.exp(-LRU_C * r * sp[:, ks])
                b = jnp.sqrt(1.0 - a * a) * (gi * xk)
                a_s[t0:t0 + tiles, :, ks] = a.reshape(tiles, SUBLANE, LRU_BW)
                b_s[t0:t0 + tiles, :, ks] = b.reshape(tiles, SUBLANE, LRU_BW)

    def scan(out_s, h0, reverse):
        ntile = seq // SUBLANE

        def body(i, h):
            j = ntile - 1 - i if reverse else i
            for r in (range(SUBLANE - 1, -1, -1) if reverse else range(SUBLANE)):
                h = a_s[j, r:r + 1, :] * h + b_s[j, r:r + 1, :]
                out_s[j, r:r + 1, :] = h
            return h

        return lax.fori_loop(0, ntile, body, h0)

    gates(0)
    sfin_ref[0:1, :] = scan(hf_s, h0_ref[0:1, :], False)
    gates(1)
    sfin_ref[1:2, :] = scan(hb_s, h0_ref[1:2, :], True)

    for r0 in range(0, seq, step):
        t0 = r0 // SUBLANE
        h = (hf_s[t0:t0 + tiles] + hb_s[t0:t0 + tiles]).reshape(step, width)
        rec_ref[r0:r0 + step, :] = (h * jax.nn.gelu(gate_ref[r0:r0 + step, :].astype(F32))).astype(BF16)


def _lru(u, nseq, seq, conv_w, conv_b, w_r, w_i, b_r, b_i, lam, h0):
    d_rnn = conv_w.shape[1]
    width = 512
    nb = d_rnn // width
    kb = width // LRU_BW
    vec = lambda rows: pl.BlockSpec((rows, width), lambda b, c: (0, c))
    return pl.pallas_call(
        functools.partial(_lru_kernel, seq=seq, width=width),
        grid=(nseq, nb),
        in_specs=[
            pl.BlockSpec((seq, width), lambda b, c: (b, c)),
            pl.BlockSpec((seq, width), lambda b, c: (b, nb + c)),
            vec(CONV_W), vec(1),
            pl.BlockSpec((2, kb, LRU_BW, LRU_BW), lambda b, c: (0, c, 0, 0)),
            pl.BlockSpec((2, kb, LRU_BW, LRU_BW), lambda b, c: (0, c, 0, 0)),
            vec(2), vec(2), vec(2),
            pl.BlockSpec((None, 2, width), lambda b, c: (b, 0, c)),
        ],
        out_specs=(pl.BlockSpec((seq, width), lambda b, c: (b, c)),
                   pl.BlockSpec((None, 2, width), lambda b, c: (b, 0, c))),
        out_shape=(jax.ShapeDtypeStruct((nseq * seq, d_rnn), BF16),
                   jax.ShapeDtypeStruct((nseq, 2, d_rnn), F32)),
        scratch_shapes=[pltpu.VMEM((seq + 2 * HALO, width), BF16), pltpu.VMEM((seq, width), F32)]
        + [pltpu.VMEM((seq // SUBLANE, SUBLANE, width), F32)] * 4,
        compiler_params=_cparams("parallel", "parallel"),
        name="lru",
    )(u, u, conv_w, conv_b.reshape(1, d_rnn), w_r, w_i, b_r, b_i, lam, h0)


def _half_rms(x, gain):
    x2 = x * x
    s0 = jnp.sum(x2[:, :DA_HALF], axis=-1, keepdims=True)
    s1 = jnp.sum(x2[:, DA_HALF:], axis=-1, keepdims=True)
    lane = lax.broadcasted_iota(jnp.int32, x.shape, 1)
    ms = jnp.where(lane < DA_HALF, s0, s1) * (1.0 / DA_HALF)
    return x * lax.rsqrt(ms + EPS) * gain


def _rope(x, cos, sin_signed):
    q = DA_HALF // 4
    lane = lax.broadcasted_iota(jnp.int32, x.shape, 1)
    rot = jnp.where((lane % (2 * q)) < q, pltpu.roll(x, LANE - q, 1), pltpu.roll(x, q, 1))
    return x * cos + rot * sin_signed


def _attn_kernel(*refs, latent, seq, past, lam_init, sub, has_prev):
    if latent:
        (q_ref, k_ref, v_ref, kc_ref, vc_ref, cosq_ref, sinq_ref, cosk_ref, sink_ref,
         qn_ref, kn_ref, dl_ref, sub_ref, o_ref, kt_s, vx_s) = refs
    else:
        if has_prev:
            refs = refs[:7] + refs[9:]
        (q_ref, k_ref, v_ref, qn_ref, kn_ref, dl_ref, sub_ref, o_ref, ko_ref, vo_ref, kt_s, vx_s) = refs
    hd = DA_VDIM

    @pl.when(pl.program_id(2) == 0)
    def _():
        vx_s[:, hd:2 * hd] = jnp.ones((seq + past, hd), BF16)
        step = min(seq, CONV_ROWS)
        for r0 in range(0, seq, step):
            rows = slice(r0, r0 + step)
            kn = _half_rms(k_ref[rows, :].astype(F32), kn_ref[...])
            if latent:
                kn = _rope(kn, cosk_ref[rows, :], sink_ref[rows, :])
            else:
                ko_ref[rows, :] = kn
                vo_ref[rows, :] = v_ref[rows, :].astype(F32)
            kt_s[:, rows] = kn.T.astype(BF16)
            vx_s[rows, 0:hd] = v_ref[rows, :]
        if latent:
            kt_s[:, seq:seq + past] = kc_ref[...].T.astype(BF16)
            vx_s[seq:seq + past, 0:hd] = vc_ref[...].astype(BF16)

    dl = dl_ref[...]
    lam = (jnp.exp(jnp.sum(dl[0:1] * dl[1:2], axis=-1, keepdims=True))
           - jnp.exp(jnp.sum(dl[2:3] * dl[3:4], axis=-1, keepdims=True)) + lam_init)
    blocks = range(0, q_ref.shape[0], sub)
    scores = []
    for r0 in blocks:
        rows = slice(r0, r0 + sub)
        qn = _half_rms(q_ref[rows, :].astype(F32), qn_ref[...])
        if latent:
            qn = _rope(qn, cosq_ref[rows, :], sinq_ref[rows, :])
        qb = (qn * (DA_HALF ** -0.5 * LOG2E)).astype(BF16)
        scores.append([jnp.dot(qb[:, m * DA_HALF:(m + 1) * DA_HALF], kt_s[m * DA_HALF:(m + 1) * DA_HALF, :],
                               preferred_element_type=F32) for m in range(2)])
    for r0, (s0, s1) in zip(blocks, scores):
        rows = slice(r0, r0 + sub)
        outs = []
        for s in (s0, s1):
            p = jnp.exp2(s - jnp.max(s, axis=-1, keepdims=True)).astype(BF16)
            outs.append(jnp.dot(p, vx_s[...], preferred_element_type=F32))
        o = outs[0][:, :hd] / outs[0][:, hd:] - lam * (outs[1][:, :hd] / outs[1][:, hd:])
        o = o * lax.rsqrt(jnp.mean(o * o, axis=-1, keepdims=True) + EPS) * sub_ref[...] * (1.0 - lam_init)
        o_ref[rows, :] = o.astype(BF16)


def _attn(u, nseq, seq, heads, lam_init, q_norm, k_norm, da_lam, subln, ctx=None, rope=None,
          kv_layers=0, kv_layer=0, kv_prev=None):
    latent = ctx is not None
    hd = DA_VDIM
    sub = 256
    tq = min(seq, 4 * sub)
    nq = seq // tq
    col0 = (u.shape[1] - 3 * heads * hd) // hd
    past = ctx[0].shape[2] if latent else 0
    q_spec = pl.BlockSpec((tq, hd), lambda b, h, i: (b * nq + i, col0 + h))
    k_spec = pl.BlockSpec((seq, hd), lambda b, h, i: (b, col0 + heads + h))
    v_spec = pl.BlockSpec((seq, hd), lambda b, h, i: (b, col0 + 2 * heads + h))
    small = lambda r, c: pl.BlockSpec((r, c), lambda b, h, i: (0, 0))
    par_specs = [small(1, hd), small(1, hd), small(4, DA_HALF), small(1, hd)]
    pars = (q_norm.reshape(1, hd), k_norm.reshape(1, hd), da_lam, subln.reshape(1, hd))
    o_spec = pl.BlockSpec((tq, hd), lambda b, h, i: (b * nq + i, h))
    o_shape = jax.ShapeDtypeStruct((nseq * seq, heads * hd), BF16)
    aliases = {}
    if latent:
        kc, vc, layer = ctx
        cos, sin = rope
        c_spec = pl.BlockSpec((None, None, past, hd), lambda b, h, i: (b, layer, 0, h))
        tq_spec = pl.BlockSpec((tq, hd), lambda b, h, i: (i, 0))
        tk_spec = pl.BlockSpec((seq, hd), lambda b, h, i: (0, 0))
        in_specs = [q_spec, k_spec, v_spec, c_spec, c_spec, tq_spec, tq_spec, tk_spec, tk_spec] + par_specs
        args = (u, u, u, kc, vc, cos, sin, cos, sin) + pars
        out_specs, out_shape = o_spec, o_shape
    else:
        in_specs = [q_spec, k_spec, v_spec] + par_specs
        args = (u, u, u) + pars
        if kv_prev is not None:
            aliases = {len(args): 1, len(args) + 1: 2}
            in_specs += [pl.BlockSpec(memory_space=pl.ANY)] * 2
            args += tuple(kv_prev)
        kv_spec = pl.BlockSpec((None, None, seq, hd), lambda b, h, i: (b, kv_layer, 0, h))
        kv_shape = jax.ShapeDtypeStruct((nseq, kv_layers, seq, heads * hd), F32)
        out_specs, out_shape = (o_spec, kv_spec, kv_spec), (o_shape, kv_shape, kv_shape)
    return pl.pallas_call(
        functools.partial(_attn_kernel, latent=latent, seq=seq, past=past, lam_init=lam_init, sub=sub,
                          has_prev=kv_prev is not None),
        grid=(nseq, heads, nq),
        in_specs=in_specs,
        out_specs=out_specs,
        out_shape=out_shape,
        input_output_aliases=aliases,
        scratch_shapes=[pltpu.VMEM((hd, seq + past), BF16), pltpu.VMEM((seq + past, 2 * hd), BF16)],
        compiler_params=_cparams("parallel", "parallel", "arbitrary"),
        name="attn",
    )(*args)


def _split3(x):
    hi = x.astype(BF16)
    r1 = x - hi.astype(F32)
    mid = r1.astype(BF16)
    lo = (r1 - mid.astype(F32)).astype(BF16)
    return hi, mid, lo


def _dot(a, b):
    return jnp.dot(a, b, preferred_element_type=F32)


def _ssd_kernel(*refs, seq, has_h0, has_prev, has_st):
    (z_ref, x_ref, b_ref, c_ref, dt_ref, dbc_ref, alc_ref, alr_ref,
     cwx_ref, cwb_ref, cwc_ref, cbx_ref, cbb_ref, cbc_ref, dsk_ref, nw_ref) = refs[:16]
    rest = refs[16:]
    if has_h0:
        h0_ref, rest = rest[0], rest[1:]
    if has_prev:
        rest = rest[1:]
    y_ref, rest = rest[0], rest[1:]
    if has_st:
        st_ref, rest = rest[0], rest[1:]
    (pad_s, padb_s, padc_s, xs_s, bt_s, cs_s, ys_s, st_s, exp_s, slot_s, half_s,
     csr_s, dtot_s, new_s, ent_s, sel_s) = rest

    q = SSD_CHUNK
    nslot = 2 * SSD_HPG
    pair_w = 2 * SSD_HEADDIM
    gw = xs_s.shape[1]

    def put_x(r0, v):
        xs_s[r0:r0 + CONV_BLK, :] = jax.nn.silu(v)

    def put_b(r0, v):
        bt_s[:, r0:r0 + CONV_BLK] = jax.nn.silu(v).T

    def put_c(r0, v):
        cs_s[r0:r0 + CONV_BLK, :] = jax.nn.silu(v).astype(BF16)

    _conv_blocks(x_ref, cwx_ref, cbx_ref, pad_s, seq, gw, put_x)
    _conv_blocks(b_ref, cwb_ref, cbb_ref, padb_s, seq, SSD_STATE, put_b)
    _conv_blocks(c_ref, cwc_ref, cbc_ref, padc_s, seq, SSD_STATE, put_c)

    li = lax.broadcasted_iota(jnp.int32, (q, q), 0)
    si = lax.broadcasted_iota(jnp.int32, (q, q), 1)
    lower = (li >= si)
    upper = (li <= si)
    lower_b = lower.astype(BF16)
    upper_b = upper.astype(BF16)
    one_b = jnp.ones((q, q), BF16)
    zero_b = jnp.zeros((q, q), BF16)
    a_rep = -jnp.exp(alc_ref[...])
    a_row = -jnp.exp(alr_ref[...])
    for s in range(nslot):
        slot_s[s] = (li % nslot == s).astype(BF16)
    for i in range(2):
        half_s[i] = (si // SSD_HEADDIM == i).astype(BF16)
    sel_s[...] = (li == pl.program_id(1) * nslot + si % nslot).astype(BF16)
    lower2 = jnp.concatenate([lower, lower], axis=1)
    diag2 = jnp.concatenate([li == si, li == si], axis=1)
    ki = lax.broadcasted_iota(jnp.int32, (q, gw), 0)
    ji = lax.broadcasted_iota(jnp.int32, (q, gw), 1)
    for d in range(2):
        exp_s[d] = ((ki % nslot == d * SSD_HPG + ji // SSD_HEADDIM) & (ki < 3 * nslot)).astype(BF16)

    def terms3(v):
        lane = lax.broadcasted_iota(jnp.int32, v.shape, 1)
        hi = v.astype(BF16).astype(F32)
        mid = (v - hi).astype(BF16).astype(F32)
        lo = v - hi - mid
        return jnp.where(lane < nslot, hi, jnp.where(lane < 2 * nslot, mid, jnp.where(
            lane < 3 * nslot, lo, 0.0))).astype(BF16)

    fwd_lane = (si % nslot) < SSD_HPG
    fwd_row = lax.broadcasted_iota(jnp.int32, (nslot, q), 0) < SSD_HPG

    def cums(c):
        rows = pl.ds(pl.multiple_of(c * q, q), q)
        raw_rep = sum(_dot(part, sel_s[...]) for part in _split3(dt_ref[rows, :]))
        dt_rep = jax.nn.softplus(raw_rep + dbc_ref[...])
        da_rep = dt_rep * a_rep
        dt_row = dt_rep.T[0:nslot]
        da_row = dt_row * a_row
        csf_rep = sum(_dot(lower_b, part) for part in _split3(da_rep))
        csf_row = sum(_dot(part, upper_b) for part in _split3(da_row))
        bt = bt_s[:, rows]
        cb = _dot(cs_s[rows, :], bt.astype(BF16))
        return rows, dt_row, da_rep, da_row, csf_rep, csf_row, bt, cb

    def segments(c, stage1):
        rows, dt_row, da_rep, da_row, csf_rep, csf_row, bt, cb = stage1
        tot_rep = csf_rep[q - 1:q, :]
        tot_row = csf_row[:, q - 1:q]
        cs_rep = jnp.where(fwd_lane, csf_rep, tot_rep - csf_rep + da_rep)
        cs_row = jnp.where(fwd_row, csf_row, tot_row - csf_row + da_row)
        csr_s[rows, :] = cs_rep
        et = terms3(jnp.broadcast_to(jnp.exp(tot_rep), (SUBLANE, q)))
        for d in range(2):
            dtot_s[d, c] = _dot(et, exp_s[d])
        w_row = dt_row * jnp.exp(tot_row - cs_row)
        c_hi = cs_rep.astype(BF16).astype(F32)
        c_mid = (cs_rep - c_hi).astype(BF16).astype(F32)
        c_lo = cs_rep - c_hi - c_mid
        lhs_all = jnp.where(si < nslot, c_hi, jnp.where(si < 2 * nslot, c_mid, jnp.where(
            si < 3 * nslot, c_lo, jnp.where(si < 6 * nslot, 1.0, 0.0)))).astype(BF16)
        r_hi, r_mid, r_lo = _split3(jnp.log(dt_row) - cs_row)
        rhs = jnp.concatenate([one_b[:3 * nslot], r_hi, r_mid, r_lo, zero_b[:2 * nslot]], axis=0)
        segs = []
        for pr in range(SSD_HPG // 2):
            f0, f1 = 2 * pr, 2 * pr + 1
            b0, b1 = SSD_HPG + f0, SSD_HPG + f1
            segs.append((_dot(lhs_all, jnp.concatenate([rhs * slot_s[f0], rhs * slot_s[f1]], axis=1)),
                         _dot(lhs_all, jnp.concatenate([rhs * slot_s[b0], rhs * slot_s[b1]], axis=1))))
        return rows, dt_row, w_row, bt, cb, segs

    def products(c, stage2):
        rows, dt_row, w_row, bt, cb, segs = stage2
        cb2 = jnp.concatenate([cb, cb], axis=1)
        bt2 = jnp.concatenate([bt, bt], axis=1)
        xb = xs_s[rows, :].astype(BF16)
        row2 = lambda v, s0, s1: jnp.concatenate([v[s0:s0 + 1, :], v[s1:s1 + 1, :]], axis=1)
        for pr in range(SSD_HPG // 2):
            ps = slice(pr * pair_w, (pr + 1) * pair_w)
            f0, f1 = 2 * pr, 2 * pr + 1
            b0, b1 = SSD_HPG + f0, SSD_HPG + f1
            seg_f, seg_b = segs[pr]
            dec = jnp.exp(jnp.where(lower2, seg_f, seg_b)) + jnp.where(diag2, row2(dt_row, b0, b1), 0.0)
            m = (cb2 * dec).astype(BF16)
            btw_f = (bt2 * row2(w_row, f0, f1)).astype(BF16)
            btw_b = (bt2 * row2(w_row, b0, b1)).astype(BF16)
            xp = xb[:, ps]
            xh = jnp.concatenate([xp * half_s[0], xp * half_s[1]], axis=0)
            r = _dot(jnp.concatenate([m, btw_f, btw_b], axis=0), xh)
            ys_s[rows, ps] = r[:q]
            new_s[0, c, :, ps] = r[q:2 * q]
            new_s[1, c, :, ps] = r[2 * q:]

    def local_pass(i, carry):
        pair = (2 * i, 2 * i + 1)
        stage1 = [cums(c) for c in pair]
        stage2 = [segments(c, s) for c, s in zip(pair, stage1)]
        for c, s in zip(pair, stage2):
            products(c, s)
        return carry

    nchunks = seq // q
    lax.fori_loop(0, nchunks // 2, local_pass, 0)

    for d in range(2):
        for pr in range(SSD_HPG // 2):
            ps = slice(pr * pair_w, (pr + 1) * pair_w)
            if has_h0:
                st_s[:, ps] = h0_ref[d, 2 * pr:2 * pr + 2].reshape(pair_w, SSD_STATE).T
            else:
                st_s[:, ps] = jnp.zeros((SSD_STATE, pair_w), F32)

        def carry_state(i, carry, d=d):
            c = i if d == 0 else nchunks - 1 - i
            st = st_s[...]
            ent_s[d, c] = st.astype(BF16)
            st_s[...] = st * dtot_s[d, c][0:1, :] + new_s[d, c]
            return carry

        lax.fori_loop(0, nchunks, carry_state, 0, unroll=2)
        for pr in range(SSD_HPG // 2 if has_st else 0):
            ps = slice(pr * pair_w, (pr + 1) * pair_w)
            st_ref[d, 2 * pr:2 * pr + 2] = st_s[:, ps].T.reshape(2, SSD_HEADDIM, SSD_STATE)

    def finish(c, carry):
        rows = pl.ds(pl.multiple_of(c * q, q), q)
        e1 = terms3(jnp.exp(csr_s[rows, :]))
        cc = cs_s[rows, :]
        y = ys_s[rows, :] + dsk_ref[...] * xs_s[rows, :]
        for d in range(2):
            y = y + _dot(cc, ent_s[d, c]) * _dot(e1, exp_s[d])
        y = y * jax.nn.silu(z_ref[rows, :].astype(F32))
        y = y * lax.rsqrt(jnp.mean(y * y, axis=-1, keepdims=True) + EPS) * nw_ref[...]
        y_ref[rows, :] = y.astype(BF16)
        return carry

    lax.fori_loop(0, nchunks, finish, 0, unroll=2)


def _ssd(u, dt, nseq, seq, conv_w, conv_b, dt_bias, a_log, d_skip, norm_w, *,
         h0=None, h0_layer=0, st_layers=0, st_layer=0, st_prev=None):
    t = nseq * seq
    g = SSD_GROUPS
    e = SSD_HPG
    heads = g * e
    gw = e * SSD_HEADDIM
    d_inner = heads * SSD_HEADDIM
    nc = seq // SSD_CHUNK
    xb0 = d_inner // gw
    bb0 = 2 * d_inner // SSD_STATE
    per_group = lambda v: v.reshape(2, g, e).transpose(1, 0, 2).reshape(g, 2 * e)
    rep = LANE // (2 * e)
    dbg = per_group(dt_bias)
    alg = per_group(a_log)
    dbg_rep = jnp.tile(dbg, (1, rep)).reshape(g, 1, LANE)
    alg_rep = jnp.tile(alg, (1, rep)).reshape(g, 1, LANE)
    dsk = jnp.repeat(d_skip, SSD_HEADDIM).reshape(1, d_inner)
    cw = lambda width, blk0: pl.BlockSpec((CONV_W, width), lambda b, gi: (0, blk0 + gi))
    cbias = lambda width, blk0: pl.BlockSpec((1, width), lambda b, gi: (0, blk0 + gi))
    st_block = (None, None, 2, e, SSD_HEADDIM, SSD_STATE)
    in_specs = [
        pl.BlockSpec((seq, gw), lambda b, gi: (b, gi)),
        pl.BlockSpec((seq, gw), lambda b, gi: (b, xb0 + gi)),
        pl.BlockSpec((seq, SSD_STATE), lambda b, gi: (b, bb0 + gi)),
        pl.BlockSpec((seq, SSD_STATE), lambda b, gi: (b, bb0 + g + gi)),
        pl.BlockSpec((seq, LANE), lambda b, gi: (b, 0)),
        pl.BlockSpec((None, 1, LANE), lambda b, gi: (gi, 0, 0)),
        pl.BlockSpec((None, 1, LANE), lambda b, gi: (gi, 0, 0)),
        pl.BlockSpec((None, 2 * e, 1), lambda b, gi: (gi, 0, 0)),
        cw(gw, 0), cw(SSD_STATE, d_inner // SSD_STATE), cw(SSD_STATE, d_inner // SSD_STATE + g),
        cbias(gw, 0), cbias(SSD_STATE, d_inner // SSD_STATE), cbias(SSD_STATE, d_inner // SSD_STATE + g),
        cbias(gw, 0), cbias(gw, 0),
    ]
    conv_b2 = conv_b.reshape(1, -1)
    args = [u, u, u, u, dt, dbg_rep, alg_rep, alg.reshape(g, 2 * e, 1),
            conv_w, conv_w, conv_w, conv_b2, conv_b2, conv_b2, dsk, norm_w.reshape(1, d_inner)]
    if h0 is not None:
        in_specs.append(pl.BlockSpec(st_block, lambda b, gi: (b, h0_layer, 0, gi, 0, 0)))
        args.append(h0)
    out_specs = [pl.BlockSpec((seq, gw), lambda b, gi: (b, gi))]
    out_shape = [jax.ShapeDtypeStruct((t, d_inner), BF16)]
    aliases = {}
    if st_layers:
        if st_prev is not None:
            aliases = {len(args): 1}
            in_specs.append(pl.BlockSpec(memory_space=pl.ANY))
            args.append(st_prev)
        out_specs.append(pl.BlockSpec(st_block, lambda b, gi: (b, st_layer, 0, gi, 0, 0)))
        out_shape.append(jax.ShapeDtypeStruct((nseq, st_layers, 2, heads, SSD_HEADDIM, SSD_STATE), F32))
    return pl.pallas_call(
        functools.partial(_ssd_kernel, seq=seq, has_h0=h0 is not None, has_prev=st_prev is not None,
                          has_st=bool(st_layers)),
        grid=(nseq, g),
        in_specs=in_specs,
        out_specs=out_specs,
        out_shape=out_shape,
        input_output_aliases=aliases,
        scratch_shapes=[
            pltpu.VMEM((seq + 2 * HALO, gw), BF16),
            pltpu.VMEM((seq + 2 * HALO, SSD_STATE), BF16),
            pltpu.VMEM((seq + 2 * HALO, SSD_STATE), BF16),
            pltpu.VMEM((seq, gw), F32),
            pltpu.VMEM((SSD_STATE, seq), F32),
            pltpu.VMEM((seq, SSD_STATE), BF16),
            pltpu.VMEM((seq, gw), F32),
            pltpu.VMEM((SSD_STATE, gw), F32),
            pltpu.VMEM((2, SSD_CHUNK, gw), BF16),
            pltpu.VMEM((2 * e, SSD_CHUNK, LANE), BF16),
            pltpu.VMEM((2, SSD_CHUNK, LANE), BF16),
            pltpu.VMEM((seq, LANE), F32),
            pltpu.VMEM((2, nc, SUBLANE, gw), F32),
            pltpu.VMEM((2, nc, SSD_STATE, gw), F32),
            pltpu.VMEM((2, nc, SSD_STATE, gw), BF16),
            pltpu.VMEM((SSD_CHUNK, LANE), BF16),
        ],
        compiler_params=_cparams("parallel", "parallel"),
        name="ssd",
    )(*args)


def _rope_tables(seq):
    rows = seq // GRID_W
    row = jnp.repeat(jnp.arange(rows), GRID_W).astype(F32)
    col = jnp.tile(jnp.arange(GRID_W), rows).astype(F32)
    quarter = DA_HALF // 4
    inv = ROPE_THETA ** (-jnp.arange(quarter, dtype=F32) / quarter)
    ang_r = row[:, None] * inv
    ang_c = col[:, None] * inv
    ang = jnp.concatenate([ang_r, ang_r, ang_c, ang_c], axis=-1)
    sign = jnp.tile(jnp.concatenate([-jnp.ones((quarter,), F32), jnp.ones((quarter,), F32)]), 2)
    cos = jnp.cos(ang)
    sin = jnp.sin(ang) * sign
    return jnp.tile(cos, (1, 2)), jnp.tile(sin, (1, 2))


def kernel(x_prompt, x_sample, c, cache_attn_k, cache_attn_v, state_lru, state_ssd, c_ctx, w_ada, b_ada, norm_g, lru_conv_w, lru_conv_b, lru_w_r, lru_b_r, lru_w_i, lru_b_i, lru_lambda, even_w_in, even_w_out, da_q_norm, da_k_norm, da_lambda, da_subln, ssd_w_in, ssd_conv_w, ssd_conv_b, ssd_dt_bias, ssd_a_log, ssd_d, ssd_norm_w, ssd_w_out, ffn_w_in, ffn_w_out):
    depth = w_ada.shape[0]
    batch, seq_p, d = x_prompt.shape
    dec_batch, seq_s, _ = x_sample.shape
    past = cache_attn_k.shape[2]
    heads = cache_attn_k.shape[3]
    d_rnn = lru_conv_w.shape[2]
    d_ff = ffn_w_out.shape[1]
    d_inner = ssd_w_out.shape[1]
    n_even = even_w_in.shape[0]

    cond = jnp.zeros((N_COND, d), F32).at[0].set(c_ctx).at[1:1 + dec_batch].set(c)
    mod_all = _ada(cond, w_ada, b_ada).reshape(depth, N_COND, 6, d)
    rope = _rope_tables(seq_s)
    cache_k = cache_attn_k.reshape(dec_batch, n_even, past, heads * DA_VDIM)
    cache_v = cache_attn_v.reshape(dec_batch, n_even, past, heads * DA_VDIM)

    even_in_b, even_out_b = even_w_in.astype(BF16), even_w_out.astype(BF16)
    n_odd = ssd_w_in.shape[0]
    ssd_heads = d_inner // SSD_HEADDIM
    n_main = ssd_w_in.shape[2] - 2 * ssd_heads
    ssd_in_b, ssd_out_b = ssd_w_in.astype(BF16), ssd_w_out.astype(BF16)
    hpg = ssd_heads // SSD_GROUPS
    ssd_dt_b = ssd_w_in[:, :, n_main:].astype(BF16).reshape(n_odd, d, 2, SSD_GROUPS, hpg).transpose(
        0, 1, 3, 2, 4).reshape(n_odd, d, 2 * ssd_heads)
    ffn_in_b, ffn_out_b = ffn_w_in.astype(BF16), ffn_w_out.astype(BF16)

    xs = [x_prompt.reshape(batch * seq_p, d), x_sample.reshape(dec_batch * seq_s, d)]
    shapes = [(batch, seq_p), (dec_batch, seq_s)]
    new_kv, new_lru, new_ssd = None, [], None

    for i in range(depth):
        j = i // 2
        mods = [mod_all[i, 0:1], mod_all[i, 1:1 + dec_batch]]
        g_mix = norm_g[i, 0].reshape(1, d)
        g_ffn = norm_g[i, 1].reshape(1, d)
        if i % 2 == 0:
            lam_init = 0.8 - 0.6 * math.exp(-0.3 * i)
            w_r = lru_w_r[j].astype(BF16)
            w_i = lru_w_i[j].astype(BF16)
            for s in range(2):
                nseq, seq = shapes[s]
                u = _inproj(xs[s], g_mix, mods[s], even_in_b, j, mode="plain", n_out=even_in_b.shape[2],
                            tn=1024, shift_idx=0, scale_idx=1, name="even_in")
                h0 = jnp.zeros((nseq, 2, d_rnn), F32) if s == 0 else state_lru[:, j]
                rec, s_fin = _lru(u, nseq, seq, lru_conv_w[j], lru_conv_b[j], w_r, w_i,
                                  lru_b_r[j], lru_b_i[j], lru_lambda[j], h0)
                if s == 0:
                    att, kc, vc = _attn(u, nseq, seq, heads, lam_init, da_q_norm[j], da_k_norm[j],
                                        da_lambda[j], da_subln[j], kv_layers=n_even, kv_layer=j,
                                        kv_prev=new_kv)
                    new_kv = (kc, vc)
                    new_lru.append(s_fin)
                else:
                    att = _attn(u, nseq, seq, heads, lam_init, da_q_norm[j], da_k_norm[j],
                                da_lambda[j], da_subln[j], ctx=(cache_k, cache_v, j), rope=rope)
                xs[s] = _outproj([rec, att], even_out_b, j, xs[s], mods[s], gate_idx=2, tn=1024,
                                 name="even_out")
        else:
            for s in range(2):
                nseq, seq = shapes[s]
                u, dt = _inproj(xs[s], g_mix, mods[s], ssd_in_b, j, mode="dt", n_out=n_main, tn=1024,
                                shift_idx=0, scale_idx=1, name="odd_in", w_dt=ssd_dt_b)
                ssd_args = (u, dt, nseq, seq, ssd_conv_w[j], ssd_conv_b[j], ssd_dt_bias[j],
                            ssd_a_log[j], ssd_d[j], ssd_norm_w[j])
                if s == 0:
                    y, new_ssd = _ssd(*ssd_args, st_layers=n_odd, st_layer=j, st_prev=new_ssd)
                else:
                    y, = _ssd(*ssd_args, h0=state_ssd, h0_layer=j)
                xs[s] = _outproj([y], ssd_out_b, j, xs[s], mods[s], gate_idx=2, tn=1024, name="odd_out")
        for s in range(2):
            act = _inproj(xs[s], g_ffn, mods[s], ffn_in_b, i, mode="glu", n_out=d_ff, tn=512,
                          shift_idx=3, scale_idx=4, name="ffn_in")
            xs[s] = _outproj([act], ffn_out_b, i, xs[s], mods[s], gate_idx=5, tn=512, name="ffn_out")

    return (xs[0].reshape(batch, seq_p, d), xs[1].reshape(dec_batch, seq_s, d),
            new_kv[0].reshape(batch, n_even, seq_p, heads, DA_VDIM),
            new_kv[1].reshape(batch, n_even, seq_p, heads, DA_VDIM),
            jnp.stack(new_lru, axis=1), new_ssd)
```

```python
import functools
import math

import jax
import jax.numpy as jnp
from jax import lax
from jax.experimental import pallas as pl
from jax.experimental.pallas import tpu as pltpu

F32 = jnp.float32
BF16 = jnp.bfloat16

EPS = 1e-6
GRID_W = 64
CONV_W = 4
CONV_LEFT = CONV_W // 2
LRU_BW = 128
LRU_C = 8.0
DA_HALF = 64
DA_VDIM = 2 * DA_HALF
ROPE_THETA = 10000.0
SSD_HEADDIM = 64
SSD_GROUPS = 8
SSD_HPG = 8
SSD_STATE = 128
SSD_CHUNK = 128
N_COND = 16

LANE = 128
SUBLANE = 8
VMEM_LIMIT = 56 * 1024 * 1024

ROW_TILE = 1024
MOD_ROWS = 128
CONV_ROWS = 256
CONV_BLK = 128
HALO = CONV_BLK // 2
LOG2E = 1.4426950408889634
SQRT_FLOOR = 1e-37


def _sigmoid(x):
    return 0.5 * jnp.tanh(0.5 * x) + 0.5


def _silu(x):
    return x * _sigmoid(x)


def _cparams(*sem):
    return pltpu.CompilerParams(dimension_semantics=sem, vmem_limit_bytes=VMEM_LIMIT)


def _ada_kernel(c_ref, w_ref, b_ref, o_ref):
    s = _silu(c_ref[...]).astype(BF16)
    o_ref[...] = jnp.dot(s, w_ref[...].astype(BF16), preferred_element_type=F32) + b_ref[...]


def _ada(cond, w_ada, b_ada):
    depth, d, n = w_ada.shape
    tn = 1024
    return pl.pallas_call(
        _ada_kernel,
        grid=(depth, n // tn),
        in_specs=[
            pl.BlockSpec((N_COND, d), lambda l, j: (0, 0)),
            pl.BlockSpec((None, d, tn), lambda l, j: (l, 0, j)),
            pl.BlockSpec((None, 1, tn), lambda l, j: (l, 0, j)),
        ],
        out_specs=pl.BlockSpec((None, N_COND, tn), lambda l, j: (l, 0, j)),
        out_shape=jax.ShapeDtypeStruct((depth, N_COND, n), F32),
        compiler_params=_cparams("parallel", "parallel"),
        name="ada",
    )(cond, w_ada, b_ada.reshape(depth, 1, n))


def _modulate_into(x_ref, g_ref, mod_ref, h_ref, shift_idx, scale_idx):
    gs = g_ref[...] * (1.0 + mod_ref[scale_idx:scale_idx + 1, :])
    shift = mod_ref[shift_idx:shift_idx + 1, :]

    def body(r, carry):
        rows = pl.ds(pl.multiple_of(r * MOD_ROWS, MOD_ROWS), MOD_ROWS)
        x = x_ref[rows, :]
        ms = jnp.mean(x * x, axis=-1, keepdims=True)
        h_ref[rows, :] = (x * lax.rsqrt(ms + EPS) * gs + shift).astype(BF16)
        return carry

    lax.fori_loop(0, x_ref.shape[0] // MOD_ROWS, body, 0)


def _inproj_kernel(*refs, mode, shift_idx, scale_idx):
    if mode == "plain":
        x_ref, g_ref, mod_ref, w_ref, o_ref, h_ref = refs
    elif mode == "glu":
        x_ref, g_ref, mod_ref, wg_ref, wu_ref, o_ref, h_ref = refs
    else:
        x_ref, g_ref, mod_ref, w_ref, wdt_ref, o_ref, dt_ref, h_ref = refs

    @pl.when(pl.program_id(1) == 0)
    def _():
        _modulate_into(x_ref, g_ref, mod_ref, h_ref, shift_idx, scale_idx)
        if mode == "dt":
            dt_ref[...] = jnp.dot(h_ref[...], wdt_ref[...], preferred_element_type=F32)

    h = h_ref[...]
    if mode == "glu":
        g = jnp.dot(h, wg_ref[...], preferred_element_type=F32)
        u = jnp.dot(h, wu_ref[...], preferred_element_type=F32)
        o_ref[...] = (_silu(g) * u).astype(o_ref.dtype)
    else:
        o_ref[...] = jnp.dot(h, w_ref[...], preferred_element_type=F32).astype(o_ref.dtype)


def _inproj(x, g, mod, w, layer, *, mode, n_out, tn, shift_idx, scale_idx, name, w_dt=None):
    t, d = x.shape
    tm = ROW_TILE
    rows_per_cond = t // mod.shape[0]
    nj = n_out // tn
    x_spec = pl.BlockSpec((tm, d), lambda i, j: (i, 0))
    g_spec = pl.BlockSpec((1, d), lambda i, j: (0, 0))
    mod_spec = pl.BlockSpec((None, 6, d), lambda i, j: ((i * tm) // rows_per_cond, 0, 0))
    w_spec = pl.BlockSpec((None, d, tn), lambda i, j: (layer, 0, j))
    o_spec = pl.BlockSpec((tm, tn), lambda i, j: (i, j))
    o_shape = jax.ShapeDtypeStruct((t, n_out), BF16)
    if mode == "plain":
        in_specs = [x_spec, g_spec, mod_spec, w_spec]
        args = (x, g, mod, w)
        out_specs, out_shape = o_spec, o_shape
    elif mode == "glu":
        in_specs = [x_spec, g_spec, mod_spec, w_spec,
                    pl.BlockSpec((None, d, tn), lambda i, j: (layer, 0, nj + j))]
        args = (x, g, mod, w, w)
        out_specs, out_shape = o_spec, o_shape
    else:
        in_specs = [x_spec, g_spec, mod_spec, w_spec,
                    pl.BlockSpec((None, d, LANE), lambda i, j: (layer, 0, 0))]
        args = (x, g, mod, w, w_dt)
        out_specs = (o_spec, pl.BlockSpec((tm, LANE), lambda i, j: (i, 0)))
        out_shape = (o_shape, jax.ShapeDtypeStruct((t, LANE), F32))
    return pl.pallas_call(
        functools.partial(_inproj_kernel, mode=mode, shift_idx=shift_idx, scale_idx=scale_idx),
        grid=(t // tm, nj),
        in_specs=in_specs,
        out_specs=out_specs,
        out_shape=out_shape,
        scratch_shapes=[pltpu.VMEM((tm, d), BF16)],
        compiler_params=_cparams("parallel", "arbitrary"),
        name=name,
    )(*args)


def _outproj_kernel(*refs, n_a, gate_idx):
    a_refs, w_refs = refs[:n_a], refs[n_a:2 * n_a]
    x_ref, mod_ref, o_ref = refs[2 * n_a:]
    acc = jnp.dot(a_refs[0][...], w_refs[0][...], preferred_element_type=F32)
    for a_ref, w_ref in zip(a_refs[1:], w_refs[1:]):
        acc = acc + jnp.dot(a_ref[...], w_ref[...], preferred_element_type=F32)
    o_ref[...] = x_ref[...] + mod_ref[gate_idx:gate_idx + 1, :] * acc


def _outproj(a_list, w, layer, x, mod, *, gate_idx, tn, name):
    t, d = x.shape
    tm = ROW_TILE
    rows_per_cond = t // mod.shape[0]
    n_a = len(a_list)
    k = a_list[0].shape[1]
    in_specs = [pl.BlockSpec((tm, k), lambda i, j: (i, 0)) for _ in a_list]
    in_specs += [pl.BlockSpec((None, k, tn), functools.partial(lambda i, j, q: (layer, q, j), q=q))
                 for q in range(n_a)]
    in_specs += [pl.BlockSpec((tm, tn), lambda i, j: (i, j)),
                 pl.BlockSpec((None, 6, tn), lambda i, j: ((i * tm) // rows_per_cond, 0, j))]
    return pl.pallas_call(
        functools.partial(_outproj_kernel, n_a=n_a, gate_idx=gate_idx),
        grid=(t // tm, d // tn),
        in_specs=in_specs,
        out_specs=pl.BlockSpec((tm, tn), lambda i, j: (i, j)),
        out_shape=jax.ShapeDtypeStruct((t, d), F32),
        compiler_params=_cparams("parallel", "parallel"),
        name=name,
    )(*a_list, *([w] * n_a), x, mod)


def _conv_blocks(src_ref, w_ref, b_ref, pad_ref, seq, width, emit):
    assert CONV_LEFT == 2 and CONV_W == 4
    zeros = jnp.zeros((HALO, width), BF16)
    pad_ref[0:HALO, 0:width] = zeros
    pad_ref[HALO + seq:2 * HALO + seq, 0:width] = zeros
    pad_ref[HALO:HALO + seq, 0:width] = src_ref[...]
    taps = [k for k in range(CONV_W) if k != CONV_LEFT]
    r_i = lax.broadcasted_iota(jnp.int32, (len(taps) * CONV_BLK, CONV_BLK), 0)
    c_i = lax.broadcasted_iota(jnp.int32, (len(taps) * CONV_BLK, CONV_BLK), 1)
    tap_i = r_i // CONV_BLK
    off = jnp.where(tap_i >= CONV_LEFT, tap_i + 1, tap_i) - CONV_LEFT
    shift = (c_i == (r_i % CONV_BLK) + off).astype(BF16)
    sub = lax.broadcasted_iota(jnp.int32, (SUBLANE, width), 0)
    edge = 2 * SUBLANE
    w = w_ref[...]
    for r0 in range(0, seq, CONV_BLK):
        lo = HALO + r0
        cur = pad_ref[lo:lo + CONV_BLK, 0:width]
        sh = jnp.dot(shift, cur, preferred_element_type=F32)
        acc = b_ref[...] + w[CONV_LEFT:CONV_LEFT + 1, :] * cur.astype(F32)
        for i, k in enumerate(taps):
            acc = acc + w[k:k + 1, :] * sh[i * CONV_BLK:(i + 1) * CONV_BLK]
        before = pad_ref[lo - edge:lo, 0:width].astype(F32)
        after = pad_ref[lo + CONV_BLK:lo + CONV_BLK + edge, 0:width].astype(F32)
        xm2, xm1, xp0 = before[edge - 2:edge - 1], before[edge - 1:edge], after[0:1]
        top = jnp.where(sub == 0, w[0:1, :] * xm2 + w[1:2, :] * xm1, jnp.where(sub == 1, w[0:1, :] * xm1, 0.0))
        bot = jnp.where(sub == SUBLANE - 1, w[3:4, :] * xp0, 0.0)
        acc = jnp.concatenate([acc[:SUBLANE] + top, acc[SUBLANE:CONV_BLK - SUBLANE],
                               acc[CONV_BLK - SUBLANE:] + bot], axis=0)
        emit(r0, acc)


def _lru_kernel(gate_ref, xr_ref, cw_ref, cb_ref, wr_ref, wi_ref, br_ref, bi_ref, lam_ref, h0_ref,
                rec_ref, sfin_ref, pad_s, xc_s, a_s, b_s, hf_s, hb_s, *, seq, width):
    def put_xc(r0, v):
        xc_s[r0:r0 + CONV_BLK, :] = v

    _conv_blocks(xr_ref, cw_ref, cb_ref, pad_s, seq, width, put_xc)
    step = min(seq, CONV_ROWS)
    tiles = step // SUBLANE

    def gates(d):
        sp = jax.nn.softplus(-lam_ref[d:d + 1, :])
        for r0 in range(0, seq, step):
            t0 = r0 // SUBLANE
            for k in range(width // LRU_BW):
                ks = slice(k * LRU_BW, (k + 1) * LRU_BW)
                xk = xc_s[r0:r0 + step, ks]
                xb = xk.astype(BF16)
                r = _sigmoid(jnp.dot(xb, wr_ref[d, k], preferred_element_type=F32) + br_ref[d:d + 1, ks])
                gi = _sigmoid(jnp.dot(xb, wi_ref[d, k], preferred_element_type=F32) + bi_ref[d:d + 1, ks])
                a = jnp.exp(-LRU_C * r * sp[:, ks])
                y = 1.0 - a * a
                b = y * lax.rsqrt(jnp.maximum(y, SQRT_FLOOR)) * (gi * xk)
                a_s[t0:t0 + tiles, :, ks] = a.reshape(tiles, SUBLANE, LRU_BW)
                b_s[t0:t0 + tiles, :, ks] = b.reshape(tiles, SUBLANE, LRU_BW)

    def scan(out_s, h0, reverse):
        ntile = seq // SUBLANE

        def body(i, h):
            j = ntile - 1 - i if reverse else i
            for r in (range(SUBLANE - 1, -1, -1) if reverse else range(SUBLANE)):
                h = a_s[j, r:r + 1, :] * h + b_s[j, r:r + 1, :]
                out_s[j, r:r + 1, :] = h
            return h

        return lax.fori_loop(0, ntile, body, h0)

    gates(0)
    sfin_ref[0:1, :] = scan(hf_s, h0_ref[0:1, :], False)
    gates(1)
    sfin_ref[1:2, :] = scan(hb_s, h0_ref[1:2, :], True)

    for r0 in range(0, seq, step):
        t0 = r0 // SUBLANE
        h = (hf_s[t0:t0 + tiles] + hb_s[t0:t0 + tiles]).reshape(step, width)
        rec_ref[r0:r0 + step, :] = (h * jax.nn.gelu(gate_ref[r0:r0 + step, :].astype(F32))).astype(BF16)


def _lru(u, nseq, seq, conv_w, conv_b, w_r, w_i, b_r, b_i, lam, h0):
    d_rnn = conv_w.shape[1]
    width = 512
    nb = d_rnn // width
    kb = width // LRU_BW
    vec = lambda rows: pl.BlockSpec((rows, width), lambda b, c: (0, c))
    return pl.pallas_call(
        functools.partial(_lru_kernel, seq=seq, width=width),
        grid=(nseq, nb),
        in_specs=[
            pl.BlockSpec((seq, width), lambda b, c: (b, c)),
            pl.BlockSpec((seq, width), lambda b, c: (b, nb + c)),
            vec(CONV_W), vec(1),
            pl.BlockSpec((2, kb, LRU_BW, LRU_BW), lambda b, c: (0, c, 0, 0)),
            pl.BlockSpec((2, kb, LRU_BW, LRU_BW), lambda b, c: (0, c, 0, 0)),
            vec(2), vec(2), vec(2),
            pl.BlockSpec((None, 2, width), lambda b, c: (b, 0, c)),
        ],
        out_specs=(pl.BlockSpec((seq, width), lambda b, c: (b, c)),
                   pl.BlockSpec((None, 2, width), lambda b, c: (b, 0, c))),
        out_shape=(jax.ShapeDtypeStruct((nseq * seq, d_rnn), BF16),
                   jax.ShapeDtypeStruct((nseq, 2, d_rnn), F32)),
        scratch_shapes=[pltpu.VMEM((seq + 2 * HALO, width), BF16), pltpu.VMEM((seq, width), F32)]
        + [pltpu.VMEM((seq // SUBLANE, SUBLANE, width), F32)] * 4,
        compiler_params=_cparams("parallel", "parallel"),
        name="lru",
    )(u, u, conv_w, conv_b.reshape(1, d_rnn), w_r, w_i, b_r, b_i, lam, h0)


def _half_rms(x, gain):
    x2 = x * x
    s0 = jnp.sum(x2[:, :DA_HALF], axis=-1, keepdims=True)
    s1 = jnp.sum(x2[:, DA_HALF:], axis=-1, keepdims=True)
    lane = lax.broadcasted_iota(jnp.int32, x.shape, 1)
    ms = jnp.where(lane < DA_HALF, s0, s1) * (1.0 / DA_HALF)
    return x * lax.rsqrt(ms + EPS) * gain


def _rope(x, cos, sin_signed):
    q = DA_HALF // 4
    lane = lax.broadcasted_iota(jnp.int32, x.shape, 1)
    rot = jnp.where((lane % (2 * q)) < q, pltpu.roll(x, LANE - q, 1), pltpu.roll(x, q, 1))
    return x * cos + rot * sin_signed


def _attn_kernel(*refs, latent, seq, past, lam_init, sub, has_prev):
    if latent:
        (q_ref, k_ref, v_ref, kc_ref, vc_ref, cosq_ref, sinq_ref, cosk_ref, sink_ref,
         qn_ref, kn_ref, dl_ref, sub_ref, o_ref, kt_s, vx_s) = refs
    else:
        if has_prev:
            refs = refs[:7] + refs[9:]
        (q_ref, k_ref, v_ref, qn_ref, kn_ref, dl_ref, sub_ref, o_ref, ko_ref, vo_ref, kt_s, vx_s) = refs
    hd = DA_VDIM

    @pl.when(pl.program_id(2) == 0)
    def _():
        vx_s[:, hd:2 * hd] = jnp.ones((seq + past, hd), BF16)
        step = min(seq, CONV_ROWS)
        for r0 in range(0, seq, step):
            rows = slice(r0, r0 + step)
            kn = _half_rms(k_ref[rows, :].astype(F32), kn_ref[...])
            if latent:
                kn = _rope(kn, cosk_ref[rows, :], sink_ref[rows, :])
            else:
                ko_ref[rows, :] = kn
                vo_ref[rows, :] = v_ref[rows, :].astype(F32)
            kt_s[:, rows] = kn.T.astype(BF16)
            vx_s[rows, 0:hd] = v_ref[rows, :]
        if latent:
            kt_s[:, seq:seq + past] = kc_ref[...].T.astype(BF16)
            vx_s[seq:seq + past, 0:hd] = vc_ref[...].astype(BF16)

    dl = dl_ref[...]
    lam = (jnp.exp(jnp.sum(dl[0:1] * dl[1:2], axis=-1, keepdims=True))
           - jnp.exp(jnp.sum(dl[2:3] * dl[3:4], axis=-1, keepdims=True)) + lam_init)
    blocks = range(0, q_ref.shape[0], sub)
    scores = []
    for r0 in blocks:
        rows = slice(r0, r0 + sub)
        qn = _half_rms(q_ref[rows, :].astype(F32), qn_ref[...])
        if latent:
            qn = _rope(qn, cosq_ref[rows, :], sinq_ref[rows, :])
        qb = (qn * (DA_HALF ** -0.5 * LOG2E)).astype(BF16)
        scores.append([jnp.dot(qb[:, m * DA_HALF:(m + 1) * DA_HALF], kt_s[m * DA_HALF:(m + 1) * DA_HALF, :],
                               preferred_element_type=F32) for m in range(2)])
    for r0, (s0, s1) in zip(blocks, scores):
        rows = slice(r0, r0 + sub)
        outs = []
        for s in (s0, s1):
            p = jnp.exp2(s - jnp.max(s, axis=-1, keepdims=True)).astype(BF16)
            outs.append(jnp.dot(p, vx_s[...], preferred_element_type=F32))
        o = outs[0][:, :hd] / outs[0][:, hd:] - lam * (outs[1][:, :hd] / outs[1][:, hd:])
        o = o * lax.rsqrt(jnp.mean(o * o, axis=-1, keepdims=True) + EPS) * sub_ref[...] * (1.0 - lam_init)
        o_ref[rows, :] = o.astype(BF16)


def _attn(u, nseq, seq, heads, lam_init, q_norm, k_norm, da_lam, subln, ctx=None, rope=None,
          kv_layers=0, kv_layer=0, kv_prev=None):
    latent = ctx is not None
    hd = DA_VDIM
    sub = 256
    tq = min(seq, 8 * sub)
    nq = seq // tq
    col0 = (u.shape[1] - 3 * heads * hd) // hd
    past = ctx[0].shape[2] if latent else 0
    q_spec = pl.BlockSpec((tq, hd), lambda b, h, i: (b * nq + i, col0 + h))
    k_spec = pl.BlockSpec((seq, hd), lambda b, h, i: (b, col0 + heads + h))
    v_spec = pl.BlockSpec((seq, hd), lambda b, h, i: (b, col0 + 2 * heads + h))
    small = lambda r, c: pl.BlockSpec((r, c), lambda b, h, i: (0, 0))
    par_specs = [small(1, hd), small(1, hd), small(4, DA_HALF), small(1, hd)]
    pars = (q_norm.reshape(1, hd), k_norm.reshape(1, hd), da_lam, subln.reshape(1, hd))
    o_spec = pl.BlockSpec((tq, hd), lambda b, h, i: (b * nq + i, h))
    o_shape = jax.ShapeDtypeStruct((nseq * seq, heads * hd), BF16)
    aliases = {}
    if latent:
        kc, vc, layer = ctx
        cos, sin = rope
        c_spec = pl.BlockSpec((None, None, past, hd), lambda b, h, i: (b, layer, 0, h))
        tq_spec = pl.BlockSpec((tq, hd), lambda b, h, i: (i, 0))
        tk_spec = pl.BlockSpec((seq, hd), lambda b, h, i: (0, 0))
        in_specs = [q_spec, k_spec, v_spec, c_spec, c_spec, tq_spec, tq_spec, tk_spec, tk_spec] + par_specs
        args = (u, u, u, kc, vc, cos, sin, cos, sin) + pars
        out_specs, out_shape = o_spec, o_shape
    else:
        in_specs = [q_spec, k_spec, v_spec] + par_specs
        args = (u, u, u) + pars
        if kv_prev is not None:
            aliases = {len(args): 1, len(args) + 1: 2}
            in_specs += [pl.BlockSpec(memory_space=pl.ANY)] * 2
            args += tuple(kv_prev)
        kv_spec = pl.BlockSpec((None, None, seq, hd), lambda b, h, i: (b, kv_layer, 0, h))
        kv_shape = jax.ShapeDtypeStruct((nseq, kv_layers, seq, heads * hd), F32)
        out_specs, out_shape = (o_spec, kv_spec, kv_spec), (o_shape, kv_shape, kv_shape)
    return pl.pallas_call(
        functools.partial(_attn_kernel, latent=latent, seq=seq, past=past, lam_init=lam_init, sub=sub,
                          has_prev=kv_prev is not None),
        grid=(nseq, heads, nq),
        in_specs=in_specs,
        out_specs=out_specs,
        out_shape=out_shape,
        input_output_aliases=aliases,
        scratch_shapes=[pltpu.VMEM((hd, seq + past), BF16), pltpu.VMEM((seq + past, 2 * hd), BF16)],
        compiler_params=_cparams("parallel", "parallel", "arbitrary"),
        name="attn",
    )(*args)


def _split3(x):
    hi = x.astype(BF16)
    r1 = x - hi.astype(F32)
    mid = r1.astype(BF16)
    lo = (r1 - mid.astype(F32)).astype(BF16)
    return hi, mid, lo


def _dot(a, b):
    return jnp.dot(a, b, preferred_element_type=F32)


def _ssd_kernel(*refs, seq, has_h0, has_prev, has_st):
    (z_ref, x_ref, b_ref, c_ref, dt_ref, dbc_ref, alc_ref, alr_ref,
     cwx_ref, cwb_ref, cwc_ref, cbx_ref, cbb_ref, cbc_ref, dsk_ref, nw_ref) = refs[:16]
    rest = refs[16:]
    if has_h0:
        h0_ref, rest = rest[0], rest[1:]
    if has_prev:
        rest = rest[1:]
    y_ref, rest = rest[0], rest[1:]
    if has_st:
        st_ref, rest = rest[0], rest[1:]
    (pad_s, padb_s, padc_s, xs_s, bt_s, cs_s, ys_s, st_s, exp_s, slot_s, half_s,
     csr_s, dtot_s, new_s, ent_s, sel_s) = rest

    q = SSD_CHUNK
    nslot = 2 * SSD_HPG
    pair_w = 2 * SSD_HEADDIM
    gw = xs_s.shape[1]

    def put_x(r0, v):
        xs_s[r0:r0 + CONV_BLK, :] = _silu(v)

    def put_b(r0, v):
        bt_s[:, r0:r0 + CONV_BLK] = _silu(v).T

    def put_c(r0, v):
        cs_s[r0:r0 + CONV_BLK, :] = _silu(v).astype(BF16)

    _conv_blocks(x_ref, cwx_ref, cbx_ref, pad_s, seq, gw, put_x)
    _conv_blocks(b_ref, cwb_ref, cbb_ref, padb_s, seq, SSD_STATE, put_b)
    _conv_blocks(c_ref, cwc_ref, cbc_ref, padc_s, seq, SSD_STATE, put_c)

    li = lax.broadcasted_iota(jnp.int32, (q, q), 0)
    si = lax.broadcasted_iota(jnp.int32, (q, q), 1)
    lower = (li >= si)
    upper = (li <= si)
    lower_b = lower.astype(BF16)
    upper_b = upper.astype(BF16)
    one_b = jnp.ones((q, q), BF16)
    zero_b = jnp.zeros((q, q), BF16)
    a_rep = -jnp.exp(alc_ref[...])
    a_row = -jnp.exp(alr_ref[...])
    for s in range(nslot):
        slot_s[s] = (li % nslot == s).astype(BF16)
    for i in range(2):
        half_s[i] = (si // SSD_HEADDIM == i).astype(BF16)
    sel_s[...] = (li == pl.program_id(1) * nslot + si % nslot).astype(BF16)
    lower2 = jnp.concatenate([lower, lower], axis=1)
    diag2 = jnp.concatenate([li == si, li == si], axis=1)
    ki = lax.broadcasted_iota(jnp.int32, (q, gw), 0)
    ji = lax.broadcasted_iota(jnp.int32, (q, gw), 1)
    for d in range(2):
        exp_s[d] = ((ki % nslot == d * SSD_HPG + ji // SSD_HEADDIM) & (ki < 3 * nslot)).astype(BF16)

    def terms3(v):
        lane = lax.broadcasted_iota(jnp.int32, v.shape, 1)
        hi = v.astype(BF16).astype(F32)
        mid = (v - hi).astype(BF16).astype(F32)
        lo = v - hi - mid
        return jnp.where(lane < nslot, hi, jnp.where(lane < 2 * nslot, mid, jnp.where(
            lane < 3 * nslot, lo, 0.0))).astype(BF16)

    fwd_lane = (si % nslot) < SSD_HPG
    fwd_row = lax.broadcasted_iota(jnp.int32, (nslot, q), 0) < SSD_HPG

    def cums(c):
        rows = pl.ds(pl.multiple_of(c * q, q), q)
        raw_rep = sum(_dot(part, sel_s[...]) for part in _split3(dt_ref[rows, :]))
        dt_rep = jax.nn.softplus(raw_rep + dbc_ref[...])
        da_rep = dt_rep * a_rep
        dt_row = dt_rep.T[0:nslot]
        da_row = dt_row * a_row
        csf_rep = sum(_dot(lower_b, part) for part in _split3(da_rep))
        csf_row = sum(_dot(part, upper_b) for part in _split3(da_row))
        bt = bt_s[:, rows]
        cb = _dot(cs_s[rows, :], bt.astype(BF16))
        return rows, dt_row, da_rep, da_row, csf_rep, csf_row, bt, cb

    def segments(c, stage1):
        rows, dt_row, da_rep, da_row, csf_rep, csf_row, bt, cb = stage1
        tot_rep = csf_rep[q - 1:q, :]
        tot_row = csf_row[:, q - 1:q]
        cs_rep = jnp.where(fwd_lane, csf_rep, tot_rep - csf_rep + da_rep)
        cs_row = jnp.where(fwd_row, csf_row, tot_row - csf_row + da_row)
        csr_s[rows, :] = cs_rep
        et = terms3(jnp.broadcast_to(jnp.exp(tot_rep), (SUBLANE, q)))
        for d in range(2):
            dtot_s[d, c] = _dot(et, exp_s[d])
        w_row = dt_row * jnp.exp(tot_row - cs_row)
        c_hi = cs_rep.astype(BF16).astype(F32)
        c_mid = (cs_rep - c_hi).astype(BF16).astype(F32)
        c_lo = cs_rep - c_hi - c_mid
        lhs_all = jnp.where(si < nslot, c_hi, jnp.where(si < 2 * nslot, c_mid, jnp.where(
            si < 3 * nslot, c_lo, jnp.where(si < 6 * nslot, 1.0, 0.0)))).astype(BF16)
        r_hi, r_mid, r_lo = _split3(jnp.log(dt_row) - cs_row)
        rhs = jnp.concatenate([one_b[:3 * nslot], r_hi, r_mid, r_lo, zero_b[:2 * nslot]], axis=0)
        segs = []
        for pr in range(SSD_HPG // 2):
            f0, f1 = 2 * pr, 2 * pr + 1
            b0, b1 = SSD_HPG + f0, SSD_HPG + f1
            segs.append((_dot(lhs_all, jnp.concatenate([rhs * slot_s[f0], rhs * slot_s[f1]], axis=1)),
                         _dot(lhs_all, jnp.concatenate([rhs * slot_s[b0], rhs * slot_s[b1]], axis=1))))
        return rows, dt_row, w_row, bt, cb, segs

    def products(c, stage2):
        rows, dt_row, w_row, bt, cb, segs = stage2
        cb2 = jnp.concatenate([cb, cb], axis=1)
        bt2 = jnp.concatenate([bt, bt], axis=1)
        xb = xs_s[rows, :].astype(BF16)
        row2 = lambda v, s0, s1: jnp.concatenate([v[s0:s0 + 1, :], v[s1:s1 + 1, :]], axis=1)
        for pr in range(SSD_HPG // 2):
            ps = slice(pr * pair_w, (pr + 1) * pair_w)
            f0, f1 = 2 * pr, 2 * pr + 1
            b0, b1 = SSD_HPG + f0, SSD_HPG + f1
            seg_f, seg_b = segs[pr]
            dec = jnp.exp(jnp.where(lower2, seg_f, seg_b)) + jnp.where(diag2, row2(dt_row, b0, b1), 0.0)
            m = (cb2 * dec).astype(BF16)
            btw_f = (bt2 * row2(w_row, f0, f1)).astype(BF16)
            btw_b = (bt2 * row2(w_row, b0, b1)).astype(BF16)
            xp = xb[:, ps]
            xh = jnp.concatenate([xp * half_s[0], xp * half_s[1]], axis=0)
            r = _dot(jnp.concatenate([m, btw_f, btw_b], axis=0), xh)
            ys_s[rows, ps] = r[:q]
            new_s[0, c, :, ps] = r[q:2 * q]
            new_s[1, c, :, ps] = r[2 * q:]

    nchunks = seq // q
    group = 4 if nchunks % 4 == 0 else 2

    def local_pass(i, carry):
        chunks = [group * i + n for n in range(group)]
        stage1 = [cums(c) for c in chunks]
        stage2 = [segments(c, s) for c, s in zip(chunks, stage1)]
        for c, s in zip(chunks, stage2):
            products(c, s)
        return carry

    lax.fori_loop(0, nchunks // group, local_pass, 0)

    for d in range(2):
        for pr in range(SSD_HPG // 2):
            ps = slice(pr * pair_w, (pr + 1) * pair_w)
            if has_h0:
                st_s[:, ps] = h0_ref[d, 2 * pr:2 * pr + 2].reshape(pair_w, SSD_STATE).T
            else:
                st_s[:, ps] = jnp.zeros((SSD_STATE, pair_w), F32)

        def carry_state(i, carry, d=d):
            c = i if d == 0 else nchunks - 1 - i
            st = st_s[...]
            ent_s[d, c] = st.astype(BF16)
            st_s[...] = st * dtot_s[d, c][0:1, :] + new_s[d, c]
            return carry

        lax.fori_loop(0, nchunks, carry_state, 0, unroll=2)
        for pr in range(SSD_HPG // 2 if has_st else 0):
            ps = slice(pr * pair_w, (pr + 1) * pair_w)
            st_ref[d, 2 * pr:2 * pr + 2] = st_s[:, ps].T.reshape(2, SSD_HEADDIM, SSD_STATE)

    def finish(c, carry):
        rows = pl.ds(pl.multiple_of(c * q, q), q)
        e1 = terms3(jnp.exp(csr_s[rows, :]))
        cc = cs_s[rows, :]
        y = ys_s[rows, :] + dsk_ref[...] * xs_s[rows, :]
        for d in range(2):
            y = y + _dot(cc, ent_s[d, c]) * _dot(e1, exp_s[d])
        y = y * _silu(z_ref[rows, :].astype(F32))
        y = y * lax.rsqrt(jnp.mean(y * y, axis=-1, keepdims=True) + EPS) * nw_ref[...]
        y_ref[rows, :] = y.astype(BF16)
        return carry

    lax.fori_loop(0, nchunks, finish, 0, unroll=2)


def _ssd(u, dt, nseq, seq, conv_w, conv_b, dt_bias, a_log, d_skip, norm_w, *,
         h0=None, h0_layer=0, st_layers=0, st_layer=0, st_prev=None):
    t = nseq * seq
    g = SSD_GROUPS
    e = SSD_HPG
    heads = g * e
    gw = e * SSD_HEADDIM
    d_inner = heads * SSD_HEADDIM
    nc = seq // SSD_CHUNK
    xb0 = d_inner // gw
    bb0 = 2 * d_inner // SSD_STATE
    per_group = lambda v: v.reshape(2, g, e).transpose(1, 0, 2).reshape(g, 2 * e)
    rep = LANE // (2 * e)
    dbg = per_group(dt_bias)
    alg = per_group(a_log)
    dbg_rep = jnp.tile(dbg, (1, rep)).reshape(g, 1, LANE)
    alg_rep = jnp.tile(alg, (1, rep)).reshape(g, 1, LANE)
    dsk = jnp.repeat(d_skip, SSD_HEADDIM).reshape(1, d_inner)
    cw = lambda width, blk0: pl.BlockSpec((CONV_W, width), lambda b, gi: (0, blk0 + gi))
    cbias = lambda width, blk0: pl.BlockSpec((1, width), lambda b, gi: (0, blk0 + gi))
    st_block = (None, None, 2, e, SSD_HEADDIM, SSD_STATE)
    in_specs = [
        pl.BlockSpec((seq, gw), lambda b, gi: (b, gi)),
        pl.BlockSpec((seq, gw), lambda b, gi: (b, xb0 + gi)),
        pl.BlockSpec((seq, SSD_STATE), lambda b, gi: (b, bb0 + gi)),
        pl.BlockSpec((seq, SSD_STATE), lambda b, gi: (b, bb0 + g + gi)),
        pl.BlockSpec((seq, LANE), lambda b, gi: (b, 0)),
        pl.BlockSpec((None, 1, LANE), lambda b, gi: (gi, 0, 0)),
        pl.BlockSpec((None, 1, LANE), lambda b, gi: (gi, 0, 0)),
        pl.BlockSpec((None, 2 * e, 1), lambda b, gi: (gi, 0, 0)),
        cw(gw, 0), cw(SSD_STATE, d_inner // SSD_STATE), cw(SSD_STATE, d_inner // SSD_STATE + g),
        cbias(gw, 0), cbias(SSD_STATE, d_inner // SSD_STATE), cbias(SSD_STATE, d_inner // SSD_STATE + g),
        cbias(gw, 0), cbias(gw, 0),
    ]
    conv_b2 = conv_b.reshape(1, -1)
    args = [u, u, u, u, dt, dbg_rep, alg_rep, alg.reshape(g, 2 * e, 1),
            conv_w, conv_w, conv_w, conv_b2, conv_b2, conv_b2, dsk, norm_w.reshape(1, d_inner)]
    if h0 is not None:
        in_specs.append(pl.BlockSpec(st_block, lambda b, gi: (b, h0_layer, 0, gi, 0, 0)))
        args.append(h0)
    out_specs = [pl.BlockSpec((seq, gw), lambda b, gi: (b, gi))]
    out_shape = [jax.ShapeDtypeStruct((t, d_inner), BF16)]
    aliases = {}
    if st_layers:
        if st_prev is not None:
            aliases = {len(args): 1}
            in_specs.append(pl.BlockSpec(memory_space=pl.ANY))
            args.append(st_prev)
        out_specs.append(pl.BlockSpec(st_block, lambda b, gi: (b, st_layer, 0, gi, 0, 0)))
        out_shape.append(jax.ShapeDtypeStruct((nseq, st_layers, 2, heads, SSD_HEADDIM, SSD_STATE), F32))
    return pl.pallas_call(
        functools.partial(_ssd_kernel, seq=seq, has_h0=h0 is not None, has_prev=st_prev is not None,
                          has_st=bool(st_layers)),
        grid=(nseq, g),
        in_specs=in_specs,
        out_specs=out_specs,
        out_shape=out_shape,
        input_output_aliases=aliases,
        scratch_shapes=[
            pltpu.VMEM((seq + 2 * HALO, gw), BF16),
            pltpu.VMEM((seq + 2 * HALO, SSD_STATE), BF16),
            pltpu.VMEM((seq + 2 * HALO, SSD_STATE), BF16),
            pltpu.VMEM((seq, gw), F32),
            pltpu.VMEM((SSD_STATE, seq), F32),
            pltpu.VMEM((seq, SSD_STATE), BF16),
            pltpu.VMEM((seq, gw), F32),
            pltpu.VMEM((SSD_STATE, gw), F32),
            pltpu.VMEM((2, SSD_CHUNK, gw), BF16),
            pltpu.VMEM((2 * e, SSD_CHUNK, LANE), BF16),
            pltpu.VMEM((2, SSD_CHUNK, LANE), BF16),
            pltpu.VMEM((seq, LANE), F32),
            pltpu.VMEM((2, nc, SUBLANE, gw), F32),
            pltpu.VMEM((2, nc, SSD_STATE, gw), F32),
            pltpu.VMEM((2, nc, SSD_STATE, gw), BF16),
            pltpu.VMEM((SSD_CHUNK, LANE), BF16),
        ],
        compiler_params=_cparams("parallel", "parallel"),
        name="ssd",
    )(*args)


def _rope_tables(seq):
    rows = seq // GRID_W
    row = jnp.repeat(jnp.arange(rows), GRID_W).astype(F32)
    col = jnp.tile(jnp.arange(GRID_W), rows).astype(F32)
    quarter = DA_HALF // 4
    inv = ROPE_THETA ** (-jnp.arange(quarter, dtype=F32) / quarter)
    ang_r = row[:, None] * inv
    ang_c = col[:, None] * inv
    ang = jnp.concatenate([ang_r, ang_r, ang_c, ang_c], axis=-1)
    sign = jnp.tile(jnp.concatenate([-jnp.ones((quarter,), F32), jnp.ones((quarter,), F32)]), 2)
    cos = jnp.cos(ang)
    sin = jnp.sin(ang) * sign
    return jnp.tile(cos, (1, 2)), jnp.tile(sin, (1, 2))


def kernel(x_prompt, x_sample, c, cache_attn_k, cache_attn_v, state_lru, state_ssd, c_ctx, w_ada, b_ada, norm_g, lru_conv_w, lru_conv_b, lru_w_r, lru_b_r, lru_w_i, lru_b_i, lru_lambda, even_w_in, even_w_out, da_q_norm, da_k_norm, da_lambda, da_subln, ssd_w_in, ssd_conv_w, ssd_conv_b, ssd_dt_bias, ssd_a_log, ssd_d, ssd_norm_w, ssd_w_out, ffn_w_in, ffn_w_out):
    depth = w_ada.shape[0]
    batch, seq_p, d = x_prompt.shape
    dec_batch, seq_s, _ = x_sample.shape
    past = cache_attn_k.shape[2]
    heads = cache_attn_k.shape[3]
    d_rnn = lru_conv_w.shape[2]
    d_ff = ffn_w_out.shape[1]
    d_inner = ssd_w_out.shape[1]
    n_even = even_w_in.shape[0]

    cond = jnp.zeros((N_COND, d), F32).at[0].set(c_ctx).at[1:1 + dec_batch].set(c)
    mod_all = _ada(cond, w_ada, b_ada).reshape(depth, N_COND, 6, d)
    rope = _rope_tables(seq_s)
    cache_k = cache_attn_k.reshape(dec_batch, n_even, past, heads * DA_VDIM)
    cache_v = cache_attn_v.reshape(dec_batch, n_even, past, heads * DA_VDIM)

    even_in_b, even_out_b = even_w_in.astype(BF16), even_w_out.astype(BF16)
    n_odd = ssd_w_in.shape[0]
    ssd_heads = d_inner // SSD_HEADDIM
    n_main = ssd_w_in.shape[2] - 2 * ssd_heads
    ssd_in_b, ssd_out_b = ssd_w_in.astype(BF16), ssd_w_out.astype(BF16)
    hpg = ssd_heads // SSD_GROUPS
    ssd_dt_b = ssd_w_in[:, :, n_main:].astype(BF16).reshape(n_odd, d, 2, SSD_GROUPS, hpg).transpose(
        0, 1, 3, 2, 4).reshape(n_odd, d, 2 * ssd_heads)
    ffn_in_b, ffn_out_b = ffn_w_in.astype(BF16), ffn_w_out.astype(BF16)

    xs = [x_prompt.reshape(batch * seq_p, d), x_sample.reshape(dec_batch * seq_s, d)]
    shapes = [(batch, seq_p), (dec_batch, seq_s)]
    new_kv, new_lru, new_ssd = None, [], None

    for i in range(depth):
        j = i // 2
        mods = [mod_all[i, 0:1], mod_all[i, 1:1 + dec_batch]]
        g_mix = norm_g[i, 0].reshape(1, d)
        g_ffn = norm_g[i, 1].reshape(1, d)
        if i % 2 == 0:
            lam_init = 0.8 - 0.6 * math.exp(-0.3 * i)
            w_r = lru_w_r[j].astype(BF16)
            w_i = lru_w_i[j].astype(BF16)
            for s in range(2):
                nseq, seq = shapes[s]
                u = _inproj(xs[s], g_mix, mods[s], even_in_b, j, mode="plain", n_out=even_in_b.shape[2],
                            tn=1024, shift_idx=0, scale_idx=1, name="even_in")
                h0 = jnp.zeros((nseq, 2, d_rnn), F32) if s == 0 else state_lru[:, j]
                rec, s_fin = _lru(u, nseq, seq, lru_conv_w[j], lru_conv_b[j], w_r, w_i,
                                  lru_b_r[j], lru_b_i[j], lru_lambda[j], h0)
                if s == 0:
                    att, kc, vc = _attn(u, nseq, seq, heads, lam_init, da_q_norm[j], da_k_norm[j],
                                        da_lambda[j], da_subln[j], kv_layers=n_even, kv_layer=j,
                                        kv_prev=new_kv)
                    new_kv = (kc, vc)
                    new_lru.append(s_fin)
                else:
                    att = _attn(u, nseq, seq, heads, lam_init, da_q_norm[j], da_k_norm[j],
                                da_lambda[j], da_subln[j], ctx=(cache_k, cache_v, j), rope=rope)
                xs[s] = _outproj([rec, att], even_out_b, j, xs[s], mods[s], gate_idx=2, tn=1024,
                                 name="even_out")
        else:
            for s in range(2):
                nseq, seq = shapes[s]
                u, dt = _inproj(xs[s], g_mix, mods[s], ssd_in_b, j, mode="dt", n_out=n_main, tn=1024,
                                shift_idx=0, scale_idx=1, name="odd_in", w_dt=ssd_dt_b)
                ssd_args = (u, dt, nseq, seq, ssd_conv_w[j], ssd_conv_b[j], ssd_dt_bias[j],
                            ssd_a_log[j], ssd_d[j], ssd_norm_w[j])
                if s == 0:
                    y, new_ssd = _ssd(*ssd_args, st_layers=n_odd, st_layer=j, st_prev=new_ssd)
                else:
                    y, = _ssd(*ssd_args, h0=state_ssd, h0_layer=j)
                xs[s] = _outproj([y], ssd_out_b, j, xs[s], mods[s], gate_idx=2, tn=1024, name="odd_out")
        for s in range(2):
            act = _inproj(xs[s], g_ffn, mods[s], ffn_in_b, i, mode="glu", n_out=d_ff, tn=512,
                          shift_idx=3, scale_idx=4, name="ffn_in")
            xs[s] = _outproj([act], ffn_out_b, i, xs[s], mods[s], gate_idx=5, tn=512, name="ffn_out")

    return (xs[0].reshape(batch, seq_p, d), xs[1].reshape(dec_batch, seq_s, d),
            new_kv[0].reshape(batch, n_even, seq_p, heads, DA_VDIM),
            new_kv[1].reshape(batch, n_even, seq_p, heads, DA_VDIM),
            jnp.stack(new_lru, axis=1), new_ssd)
```

```python
import functools
import math

import jax
import jax.numpy as jnp
from jax import lax
from jax.experimental import pallas as pl
from jax.experimental.pallas import tpu as pltpu

F32 = jnp.float32
BF16 = jnp.bfloat16

EPS = 1e-6
GRID_W = 64
CONV_W = 4
CONV_LEFT = CONV_W // 2
LRU_BW = 128
LRU_C = 8.0
DA_HALF = 64
DA_VDIM = 2 * DA_HALF
ROPE_THETA = 10000.0
SSD_HEADDIM = 64
SSD_GROUPS = 8
SSD_HPG = 8
SSD_STATE = 128
SSD_CHUNK = 128
N_COND = 16

LANE = 128
SUBLANE = 8
VMEM_LIMIT = 56 * 1024 * 1024

ROW_TILE = 1024
MOD_ROWS = 128
CONV_ROWS = 256
CONV_BLK = 128
HALO = CONV_BLK // 2
LOG2E = 1.4426950408889634
SQRT_FLOOR = 1e-37


def _sigmoid(x):
    return 0.5 * jnp.tanh(0.5 * x) + 0.5


def _silu(x):
    return x * _sigmoid(x)


def _cparams(*sem):
    return pltpu.CompilerParams(dimension_semantics=sem, vmem_limit_bytes=VMEM_LIMIT)


def _ada_kernel(c_ref, w_ref, b_ref, o_ref):
    s = _silu(c_ref[...]).astype(BF16)
    o_ref[...] = jnp.dot(s, w_ref[...].astype(BF16), preferred_element_type=F32) + b_ref[...]


def _ada(cond, w_ada, b_ada):
    depth, d, n = w_ada.shape
    tn = 1024
    return pl.pallas_call(
        _ada_kernel,
        grid=(depth, n // tn),
        in_specs=[
            pl.BlockSpec((N_COND, d), lambda l, j: (0, 0)),
            pl.BlockSpec((None, d, tn), lambda l, j: (l, 0, j)),
            pl.BlockSpec((None, 1, tn), lambda l, j: (l, 0, j)),
        ],
        out_specs=pl.BlockSpec((None, N_COND, tn), lambda l, j: (l, 0, j)),
        out_shape=jax.ShapeDtypeStruct((depth, N_COND, n), F32),
        compiler_params=_cparams("parallel", "parallel"),
        name="ada",
    )(cond, w_ada, b_ada.reshape(depth, 1, n))


def _modulate_into(x_ref, g_ref, mod_ref, h_ref, shift_idx, scale_idx):
    gs = g_ref[...] * (1.0 + mod_ref[scale_idx:scale_idx + 1, :])
    shift = mod_ref[shift_idx:shift_idx + 1, :]

    def body(r, carry):
        rows = pl.ds(pl.multiple_of(r * MOD_ROWS, MOD_ROWS), MOD_ROWS)
        x = x_ref[rows, :]
        ms = jnp.mean(x * x, axis=-1, keepdims=True)
        h_ref[rows, :] = (x * lax.rsqrt(ms + EPS) * gs + shift).astype(BF16)
        return carry

    lax.fori_loop(0, x_ref.shape[0] // MOD_ROWS, body, 0)


def _inproj_kernel(*refs, mode, shift_idx, scale_idx):
    if mode == "plain":
        x_ref, g_ref, mod_ref, w_ref, o_ref, h_ref = refs
    elif mode == "glu":
        x_ref, g_ref, mod_ref, wg_ref, wu_ref, o_ref, h_ref = refs
    else:
        x_ref, g_ref, mod_ref, w_ref, wdt_ref, o_ref, dt_ref, h_ref = refs

    @pl.when(pl.program_id(1) == 0)
    def _():
        _modulate_into(x_ref, g_ref, mod_ref, h_ref, shift_idx, scale_idx)
        if mode == "dt":
            dt_ref[...] = jnp.dot(h_ref[...], wdt_ref[...], preferred_element_type=F32)

    h = h_ref[...]
    if mode == "glu":
        g = jnp.dot(h, wg_ref[...], preferred_element_type=F32)
        u = jnp.dot(h, wu_ref[...], preferred_element_type=F32)
        o_ref[...] = (_silu(g) * u).astype(o_ref.dtype)
    else:
        o_ref[...] = jnp.dot(h, w_ref[...], preferred_element_type=F32).astype(o_ref.dtype)


def _inproj(x, g, mod, w, layer, *, mode, n_out, tn, shift_idx, scale_idx, name, w_dt=None):
    t, d = x.shape
    tm = ROW_TILE
    rows_per_cond = t // mod.shape[0]
    nj = n_out // tn
    x_spec = pl.BlockSpec((tm, d), lambda i, j: (i, 0))
    g_spec = pl.BlockSpec((1, d), lambda i, j: (0, 0))
    mod_spec = pl.BlockSpec((None, 6, d), lambda i, j: ((i * tm) // rows_per_cond, 0, 0))
    w_spec = pl.BlockSpec((None, d, tn), lambda i, j: (layer, 0, j))
    o_spec = pl.BlockSpec((tm, tn), lambda i, j: (i, j))
    o_shape = jax.ShapeDtypeStruct((t, n_out), BF16)
    if mode == "plain":
        in_specs = [x_spec, g_spec, mod_spec, w_spec]
        args = (x, g, mod, w)
        out_specs, out_shape = o_spec, o_shape
    elif mode == "glu":
        in_specs = [x_spec, g_spec, mod_spec, w_spec,
                    pl.BlockSpec((None, d, tn), lambda i, j: (layer, 0, nj + j))]
        args = (x, g, mod, w, w)
        out_specs, out_shape = o_spec, o_shape
    else:
        in_specs = [x_spec, g_spec, mod_spec, w_spec,
                    pl.BlockSpec((None, d, LANE), lambda i, j: (layer, 0, 0))]
        args = (x, g, mod, w, w_dt)
        out_specs = (o_spec, pl.BlockSpec((tm, LANE), lambda i, j: (i, 0)))
        out_shape = (o_shape, jax.ShapeDtypeStruct((t, LANE), F32))
    return pl.pallas_call(
        functools.partial(_inproj_kernel, mode=mode, shift_idx=shift_idx, scale_idx=scale_idx),
        grid=(t // tm, nj),
        in_specs=in_specs,
        out_specs=out_specs,
        out_shape=out_shape,
        scratch_shapes=[pltpu.VMEM((tm, d), BF16)],
        compiler_params=_cparams("parallel", "arbitrary"),
        name=name,
    )(*args)


def _outproj_kernel(*refs, n_a, gate_idx):
    a_refs, w_refs = refs[:n_a], refs[n_a:2 * n_a]
    x_ref, mod_ref, o_ref = refs[2 * n_a:]
    acc = jnp.dot(a_refs[0][...], w_refs[0][...], preferred_element_type=F32)
    for a_ref, w_ref in zip(a_refs[1:], w_refs[1:]):
        acc = acc + jnp.dot(a_ref[...], w_ref[...], preferred_element_type=F32)
    o_ref[...] = x_ref[...] + mod_ref[gate_idx:gate_idx + 1, :] * acc


def _outproj(a_list, w, layer, x, mod, *, gate_idx, tn, name):
    t, d = x.shape
    tm = ROW_TILE
    rows_per_cond = t // mod.shape[0]
    n_a = len(a_list)
    k = a_list[0].shape[1]
    in_specs = [pl.BlockSpec((tm, k), lambda i, j: (i, 0)) for _ in a_list]
    in_specs += [pl.BlockSpec((None, k, tn), functools.partial(lambda i, j, q: (layer, q, j), q=q))
                 for q in range(n_a)]
    in_specs += [pl.BlockSpec((tm, tn), lambda i, j: (i, j)),
                 pl.BlockSpec((None, 6, tn), lambda i, j: ((i * tm) // rows_per_cond, 0, j))]
    return pl.pallas_call(
        functools.partial(_outproj_kernel, n_a=n_a, gate_idx=gate_idx),
        grid=(t // tm, d // tn),
        in_specs=in_specs,
        out_specs=pl.BlockSpec((tm, tn), lambda i, j: (i, j)),
        out_shape=jax.ShapeDtypeStruct((t, d), F32),
        compiler_params=_cparams("parallel", "parallel"),
        name=name,
    )(*a_list, *([w] * n_a), x, mod)


def _conv_blocks(src_ref, w_ref, b_ref, pad_ref, seq, width, emit):
    assert CONV_LEFT == 2 and CONV_W == 4
    zeros = jnp.zeros((HALO, width), BF16)
    pad_ref[0:HALO, 0:width] = zeros
    pad_ref[HALO + seq:2 * HALO + seq, 0:width] = zeros
    pad_ref[HALO:HALO + seq, 0:width] = src_ref[...]
    taps = [k for k in range(CONV_W) if k != CONV_LEFT]
    r_i = lax.broadcasted_iota(jnp.int32, (len(taps) * CONV_BLK, CONV_BLK), 0)
    c_i = lax.broadcasted_iota(jnp.int32, (len(taps) * CONV_BLK, CONV_BLK), 1)
    tap_i = r_i // CONV_BLK
    off = jnp.where(tap_i >= CONV_LEFT, tap_i + 1, tap_i) - CONV_LEFT
    shift = (c_i == (r_i % CONV_BLK) + off).astype(BF16)
    sub = lax.broadcasted_iota(jnp.int32, (SUBLANE, width), 0)
    edge = 2 * SUBLANE
    w = w_ref[...]
    for r0 in range(0, seq, CONV_BLK):
        lo = HALO + r0
        cur = pad_ref[lo:lo + CONV_BLK, 0:width]
        sh = jnp.dot(shift, cur, preferred_element_type=F32)
        acc = b_ref[...] + w[CONV_LEFT:CONV_LEFT + 1, :] * cur.astype(F32)
        for i, k in enumerate(taps):
            acc = acc + w[k:k + 1, :] * sh[i * CONV_BLK:(i + 1) * CONV_BLK]
        before = pad_ref[lo - edge:lo, 0:width].astype(F32)
        after = pad_ref[lo + CONV_BLK:lo + CONV_BLK + edge, 0:width].astype(F32)
        xm2, xm1, xp0 = before[edge - 2:edge - 1], before[edge - 1:edge], after[0:1]
        top = jnp.where(sub == 0, w[0:1, :] * xm2 + w[1:2, :] * xm1, jnp.where(sub == 1, w[0:1, :] * xm1, 0.0))
        bot = jnp.where(sub == SUBLANE - 1, w[3:4, :] * xp0, 0.0)
        acc = jnp.concatenate([acc[:SUBLANE] + top, acc[SUBLANE:CONV_BLK - SUBLANE],
                               acc[CONV_BLK - SUBLANE:] + bot], axis=0)
        emit(r0, acc)


def _lru_kernel(gate_ref, xr_ref, cw_ref, cb_ref, wr_ref, wi_ref, br_ref, bi_ref, lam_ref, h0_ref,
                rec_ref, sfin_ref, pad_s, xc_s, a_s, b_s, hf_s, hb_s, *, seq, width):
    def put_xc(r0, v):
        xc_s[r0:r0 + CONV_BLK, :] = v

    _conv_blocks(xr_ref, cw_ref, cb_ref, pad_s, seq, width, put_xc)
    step = min(seq, CONV_ROWS)
    tiles = step // SUBLANE

    def gates(d):
        sp = jax.nn.softplus(-lam_ref[d:d + 1, :])
        for r0 in range(0, seq, step):
            t0 = r0 // SUBLANE
            for k in range(width // LRU_BW):
                ks = slice(k * LRU_BW, (k + 1) * LRU_BW)
                xk = xc_s[r0:r0 + step, ks]
                xb = xk.astype(BF16)
                r = _sigmoid(jnp.dot(xb, wr_ref[d, k], preferred_element_type=F32) + br_ref[d:d + 1, ks])
                gi = _sigmoid(jnp.dot(xb, wi_ref[d, k], preferred_element_type=F32) + bi_ref[d:d + 1, ks])
                a = jnp.exp(-LRU_C * r * sp[:, ks])
                y = 1.0 - a * a
                b = y * lax.rsqrt(jnp.maximum(y, SQRT_FLOOR)) * (gi * xk)
                a_s[t0:t0 + tiles, :, ks] = a.reshape(tiles, SUBLANE, LRU_BW)
                b_s[t0:t0 + tiles, :, ks] = b.reshape(tiles, SUBLANE, LRU_BW)

    def scan(out_s, h0, reverse):
        ntile = seq // SUBLANE

        def body(i, h):
            j = ntile - 1 - i if reverse else i
            for r in (range(SUBLANE - 1, -1, -1) if reverse else range(SUBLANE)):
                h = a_s[j, r:r + 1, :] * h + b_s[j, r:r + 1, :]
                out_s[j, r:r + 1, :] = h
            return h

        return lax.fori_loop(0, ntile, body, h0)

    gates(0)
    sfin_ref[0:1, :] = scan(hf_s, h0_ref[0:1, :], False)
    gates(1)
    sfin_ref[1:2, :] = scan(hb_s, h0_ref[1:2, :], True)

    for r0 in range(0, seq, step):
        t0 = r0 // SUBLANE
        h = (hf_s[t0:t0 + tiles] + hb_s[t0:t0 + tiles]).reshape(step, width)
        rec_ref[r0:r0 + step, :] = (h * jax.nn.gelu(gate_ref[r0:r0 + step, :].astype(F32))).astype(BF16)


def _lru(u, nseq, seq, conv_w, conv_b, w_r, w_i, b_r, b_i, lam, h0):
    d_rnn = conv_w.shape[1]
    width = 512
    nb = d_rnn // width
    kb = width // LRU_BW
    vec = lambda rows: pl.BlockSpec((rows, width), lambda b, c: (0, c))
    return pl.pallas_call(
        functools.partial(_lru_kernel, seq=seq, width=width),
        grid=(nseq, nb),
        in_specs=[
            pl.BlockSpec((seq, width), lambda b, c: (b, c)),
            pl.BlockSpec((seq, width), lambda b, c: (b, nb + c)),
            vec(CONV_W), vec(1),
            pl.BlockSpec((2, kb, LRU_BW, LRU_BW), lambda b, c: (0, c, 0, 0)),
            pl.BlockSpec((2, kb, LRU_BW, LRU_BW), lambda b, c: (0, c, 0, 0)),
            vec(2), vec(2), vec(2),
            pl.BlockSpec((None, 2, width), lambda b, c: (b, 0, c)),
        ],
        out_specs=(pl.BlockSpec((seq, width), lambda b, c: (b, c)),
                   pl.BlockSpec((None, 2, width), lambda b, c: (b, 0, c))),
        out_shape=(jax.ShapeDtypeStruct((nseq * seq, d_rnn), BF16),
                   jax.ShapeDtypeStruct((nseq, 2, d_rnn), F32)),
        scratch_shapes=[pltpu.VMEM((seq + 2 * HALO, width), BF16), pltpu.VMEM((seq, width), F32)]
        + [pltpu.VMEM((seq // SUBLANE, SUBLANE, width), F32)] * 4,
        compiler_params=_cparams("parallel", "parallel"),
        name="lru",
    )(u, u, conv_w, conv_b.reshape(1, d_rnn), w_r, w_i, b_r, b_i, lam, h0)


def _half_rms(x, gain):
    x2 = x * x
    s0 = jnp.sum(x2[:, :DA_HALF], axis=-1, keepdims=True)
    s1 = jnp.sum(x2[:, DA_HALF:], axis=-1, keepdims=True)
    lane = lax.broadcasted_iota(jnp.int32, x.shape, 1)
    ms = jnp.where(lane < DA_HALF, s0, s1) * (1.0 / DA_HALF)
    return x * lax.rsqrt(ms + EPS) * gain


def _rope(x, cos, sin_signed):
    q = DA_HALF // 4
    lane = lax.broadcasted_iota(jnp.int32, x.shape, 1)
    rot = jnp.where((lane % (2 * q)) < q, pltpu.roll(x, LANE - q, 1), pltpu.roll(x, q, 1))
    return x * cos + rot * sin_signed


def _attn_kernel(*refs, latent, seq, past, lam_init, sub, has_prev):
    if latent:
        (q_ref, k_ref, v_ref, kc_ref, vc_ref, cosq_ref, sinq_ref, cosk_ref, sink_ref,
         qn_ref, kn_ref, dl_ref, sub_ref, o_ref, kt_s, vx_s) = refs
    else:
        if has_prev:
            refs = refs[:7] + refs[9:]
        (q_ref, k_ref, v_ref, qn_ref, kn_ref, dl_ref, sub_ref, o_ref, ko_ref, vo_ref, kt_s, vx_s) = refs
    hd = DA_VDIM

    @pl.when(pl.program_id(2) == 0)
    def _():
        vx_s[:, hd:2 * hd] = jnp.ones((seq + past, hd), BF16)
        step = min(seq, CONV_ROWS)
        for r0 in range(0, seq, step):
            rows = slice(r0, r0 + step)
            kn = _half_rms(k_ref[rows, :].astype(F32), kn_ref[...])
            if latent:
                kn = _rope(kn, cosk_ref[rows, :], sink_ref[rows, :])
            else:
                ko_ref[rows, :] = kn
                vo_ref[rows, :] = v_ref[rows, :].astype(F32)
            kt_s[:, rows] = kn.T.astype(BF16)
            vx_s[rows, 0:hd] = v_ref[rows, :]
        if latent:
            kt_s[:, seq:seq + past] = kc_ref[...].T.astype(BF16)
            vx_s[seq:seq + past, 0:hd] = vc_ref[...].astype(BF16)

    dl = dl_ref[...]
    lam = (jnp.exp(jnp.sum(dl[0:1] * dl[1:2], axis=-1, keepdims=True))
           - jnp.exp(jnp.sum(dl[2:3] * dl[3:4], axis=-1, keepdims=True)) + lam_init)
    blocks = range(0, q_ref.shape[0], sub)
    scores = []
    for r0 in blocks:
        rows = slice(r0, r0 + sub)
        qn = _half_rms(q_ref[rows, :].astype(F32), qn_ref[...])
        if latent:
            qn = _rope(qn, cosq_ref[rows, :], sinq_ref[rows, :])
        qb = (qn * (DA_HALF ** -0.5 * LOG2E)).astype(BF16)
        scores.append([jnp.dot(qb[:, m * DA_HALF:(m + 1) * DA_HALF], kt_s[m * DA_HALF:(m + 1) * DA_HALF, :],
                               preferred_element_type=F32) for m in range(2)])
    for r0, (s0, s1) in zip(blocks, scores):
        rows = slice(r0, r0 + sub)
        outs = []
        for s in (s0, s1):
            p = jnp.exp2(s - jnp.max(s, axis=-1, keepdims=True)).astype(BF16)
            outs.append(jnp.dot(p, vx_s[...], preferred_element_type=F32))
        o = outs[0][:, :hd] / outs[0][:, hd:] - lam * (outs[1][:, :hd] / outs[1][:, hd:])
        o = o * lax.rsqrt(jnp.mean(o * o, axis=-1, keepdims=True) + EPS) * sub_ref[...] * (1.0 - lam_init)
        o_ref[rows, :] = o.astype(BF16)


def _attn(u, nseq, seq, heads, lam_init, q_norm, k_norm, da_lam, subln, ctx=None, rope=None,
          kv_layers=0, kv_layer=0, kv_prev=None):
    latent = ctx is not None
    hd = DA_VDIM
    sub = 256
    tq = min(seq, 8 * sub)
    nq = seq // tq
    col0 = (u.shape[1] - 3 * heads * hd) // hd
    past = ctx[0].shape[2] if latent else 0
    q_spec = pl.BlockSpec((tq, hd), lambda b, h, i: (b * nq + i, col0 + h))
    k_spec = pl.BlockSpec((seq, hd), lambda b, h, i: (b, col0 + heads + h))
    v_spec = pl.BlockSpec((seq, hd), lambda b, h, i: (b, col0 + 2 * heads + h))
    small = lambda r, c: pl.BlockSpec((r, c), lambda b, h, i: (0, 0))
    par_specs = [small(1, hd), small(1, hd), small(4, DA_HALF), small(1, hd)]
    pars = (q_norm.reshape(1, hd), k_norm.reshape(1, hd), da_lam, subln.reshape(1, hd))
    o_spec = pl.BlockSpec((tq, hd), lambda b, h, i: (b * nq + i, h))
    o_shape = jax.ShapeDtypeStruct((nseq * seq, heads * hd), BF16)
    aliases = {}
    if latent:
        kc, vc, layer = ctx
        cos, sin = rope
        c_spec = pl.BlockSpec((None, None, past, hd), lambda b, h, i: (b, layer, 0, h))
        tq_spec = pl.BlockSpec((tq, hd), lambda b, h, i: (i, 0))
        tk_spec = pl.BlockSpec((seq, hd), lambda b, h, i: (0, 0))
        in_specs = [q_spec, k_spec, v_spec, c_spec, c_spec, tq_spec, tq_spec, tk_spec, tk_spec] + par_specs
        args = (u, u, u, kc, vc, cos, sin, cos, sin) + pars
        out_specs, out_shape = o_spec, o_shape
    else:
        in_specs = [q_spec, k_spec, v_spec] + par_specs
        args = (u, u, u) + pars
        if kv_prev is not None:
            aliases = {len(args): 1, len(args) + 1: 2}
            in_specs += [pl.BlockSpec(memory_space=pl.ANY)] * 2
            args += tuple(kv_prev)
        kv_spec = pl.BlockSpec((None, None, seq, hd), lambda b, h, i: (b, kv_layer, 0, h))
        kv_shape = jax.ShapeDtypeStruct((nseq, kv_layers, seq, heads * hd), F32)
        out_specs, out_shape = (o_spec, kv_spec, kv_spec), (o_shape, kv_shape, kv_shape)
    return pl.pallas_call(
        functools.partial(_attn_kernel, latent=latent, seq=seq, past=past, lam_init=lam_init, sub=sub,
                          has_prev=kv_prev is not None),
        grid=(nseq, heads, nq),
        in_specs=in_specs,
        out_specs=out_specs,
        out_shape=out_shape,
        input_output_aliases=aliases,
        scratch_shapes=[pltpu.VMEM((hd, seq + past), BF16), pltpu.VMEM((seq + past, 2 * hd), BF16)],
        compiler_params=_cparams("parallel", "parallel", "arbitrary"),
        name="attn",
    )(*args)


def _split3(x):
    hi = x.astype(BF16)
    r1 = x - hi.astype(F32)
    mid = r1.astype(BF16)
    lo = (r1 - mid.astype(F32)).astype(BF16)
    return hi, mid, lo


def _dot(a, b):
    return jnp.dot(a, b, preferred_element_type=F32)


def _ssd_kernel(*refs, seq, has_h0, has_prev, has_st):
    (z_ref, x_ref, b_ref, c_ref, dt_ref, dbc_ref, alc_ref, alr_ref,
     cwx_ref, cwb_ref, cwc_ref, cbx_ref, cbb_ref, cbc_ref, dsk_ref, nw_ref) = refs[:16]
    rest = refs[16:]
    if has_h0:
        h0_ref, rest = rest[0], rest[1:]
    if has_prev:
        rest = rest[1:]
    y_ref, rest = rest[0], rest[1:]
    if has_st:
        st_ref, rest = rest[0], rest[1:]
    (pad_s, padb_s, padc_s, xs_s, bt_s, cs_s, ys_s, st_s, exp_s, half_s,
     csr_s, dtot_s, new_s, ent_s, sel_s) = rest

    q = SSD_CHUNK
    nslot = 2 * SSD_HPG
    pair_w = 2 * SSD_HEADDIM
    gw = xs_s.shape[1]

    def put_x(r0, v):
        xs_s[r0:r0 + CONV_BLK, :] = _silu(v)

    def put_b(r0, v):
        bt_s[:, r0:r0 + CONV_BLK] = _silu(v).T

    def put_c(r0, v):
        cs_s[r0:r0 + CONV_BLK, :] = _silu(v).astype(BF16)

    _conv_blocks(x_ref, cwx_ref, cbx_ref, pad_s, seq, gw, put_x)
    _conv_blocks(b_ref, cwb_ref, cbb_ref, padb_s, seq, SSD_STATE, put_b)
    _conv_blocks(c_ref, cwc_ref, cbc_ref, padc_s, seq, SSD_STATE, put_c)

    li = lax.broadcasted_iota(jnp.int32, (q, q), 0)
    si = lax.broadcasted_iota(jnp.int32, (q, q), 1)
    lower = (li >= si)
    upper = (li <= si)
    lower_b = lower.astype(BF16)
    upper_b = upper.astype(BF16)
    a_rep = -jnp.exp(alc_ref[...])
    a_row = -jnp.exp(alr_ref[...])
    for i in range(2):
        half_s[i] = (si // SSD_HEADDIM == i).astype(BF16)
    sel_s[...] = (li == pl.program_id(1) * nslot + si % nslot).astype(BF16)
    lower2 = jnp.concatenate([lower, lower], axis=1)
    diag2 = jnp.concatenate([li == si, li == si], axis=1)
    ki = lax.broadcasted_iota(jnp.int32, (q, gw), 0)
    ji = lax.broadcasted_iota(jnp.int32, (q, gw), 1)
    for d in range(2):
        exp_s[d] = ((ki % nslot == d * SSD_HPG + ji // SSD_HEADDIM) & (ki < 3 * nslot)).astype(BF16)

    def terms3(v):
        lane = lax.broadcasted_iota(jnp.int32, v.shape, 1)
        hi = v.astype(BF16).astype(F32)
        mid = (v - hi).astype(BF16).astype(F32)
        lo = v - hi - mid
        return jnp.where(lane < nslot, hi, jnp.where(lane < 2 * nslot, mid, jnp.where(
            lane < 3 * nslot, lo, 0.0))).astype(BF16)

    fwd_lane = (si % nslot) < SSD_HPG
    fwd_row = lax.broadcasted_iota(jnp.int32, (nslot, q), 0) < SSD_HPG

    def cums(c):
        rows = pl.ds(pl.multiple_of(c * q, q), q)
        raw_rep = sum(_dot(part, sel_s[...]) for part in _split3(dt_ref[rows, :]))
        dt_rep = jax.nn.softplus(raw_rep + dbc_ref[...])
        da_rep = dt_rep * a_rep
        dt_row = dt_rep.T[0:nslot]
        da_row = dt_row * a_row
        csf_rep = sum(_dot(lower_b, part) for part in _split3(da_rep))
        csf_row = sum(_dot(part, upper_b) for part in _split3(da_row))
        bt = bt_s[:, rows]
        cb = _dot(cs_s[rows, :], bt.astype(BF16))
        return rows, dt_row, da_rep, da_row, csf_rep, csf_row, bt, cb

    def segments(c, stage1):
        rows, dt_row, da_rep, da_row, csf_rep, csf_row, bt, cb = stage1
        tot_rep = csf_rep[q - 1:q, :]
        tot_row = csf_row[:, q - 1:q]
        cs_rep = jnp.where(fwd_lane, csf_rep, tot_rep - csf_rep + da_rep)
        cs_row = jnp.where(fwd_row, csf_row, tot_row - csf_row + da_row)
        csr_s[rows, :] = cs_rep
        et = terms3(jnp.broadcast_to(jnp.exp(tot_rep), (SUBLANE, q)))
        for d in range(2):
            dtot_s[d, c] = _dot(et, exp_s[d])
        w_row = dt_row * jnp.exp(tot_row - cs_row)
        neg_row = cs_row - jnp.log(dt_row)
        seg = lambda s: cs_rep[:, s:s + 1] - neg_row[s:s + 1, :]
        segs = []
        for pr in range(SSD_HPG // 2):
            f0, f1 = 2 * pr, 2 * pr + 1
            b0, b1 = SSD_HPG + f0, SSD_HPG + f1
            segs.append((jnp.concatenate([seg(f0), seg(f1)], axis=1),
                         jnp.concatenate([seg(b0), seg(b1)], axis=1)))
        return rows, dt_row, w_row, bt, cb, segs

    def products(c, stage2):
        rows, dt_row, w_row, bt, cb, segs = stage2
        cb2 = jnp.concatenate([cb, cb], axis=1)
        bt2 = jnp.concatenate([bt, bt], axis=1)
        xb = xs_s[rows, :].astype(BF16)
        row2 = lambda v, s0, s1: jnp.concatenate([v[s0:s0 + 1, :], v[s1:s1 + 1, :]], axis=1)
        for pr in range(SSD_HPG // 2):
            ps = slice(pr * pair_w, (pr + 1) * pair_w)
            f0, f1 = 2 * pr, 2 * pr + 1
            b0, b1 = SSD_HPG + f0, SSD_HPG + f1
            seg_f, seg_b = segs[pr]
            dec = jnp.exp(jnp.where(lower2, seg_f, seg_b)) + jnp.where(diag2, row2(dt_row, b0, b1), 0.0)
            m = (cb2 * dec).astype(BF16)
            btw_f = (bt2 * row2(w_row, f0, f1)).astype(BF16)
            btw_b = (bt2 * row2(w_row, b0, b1)).astype(BF16)
            xp = xb[:, ps]
            xh = jnp.concatenate([xp * half_s[0], xp * half_s[1]], axis=0)
            r = _dot(jnp.concatenate([m, btw_f, btw_b], axis=0), xh)
            ys_s[rows, ps] = r[:q]
            new_s[0, c, :, ps] = r[q:2 * q]
            new_s[1, c, :, ps] = r[2 * q:]

    nchunks = seq // q
    group = 4 if nchunks % 4 == 0 else 2

    def local_pass(i, carry):
        chunks = [group * i + n for n in range(group)]
        stage1 = [cums(c) for c in chunks]
        stage2 = [segments(c, s) for c, s in zip(chunks, stage1)]
        for c, s in zip(chunks, stage2):
            products(c, s)
        return carry

    lax.fori_loop(0, nchunks // group, local_pass, 0)

    for d in range(2):
        for pr in range(SSD_HPG // 2):
            ps = slice(pr * pair_w, (pr + 1) * pair_w)
            if has_h0:
                st_s[:, ps] = h0_ref[d, 2 * pr:2 * pr + 2].reshape(pair_w, SSD_STATE).T
            else:
                st_s[:, ps] = jnp.zeros((SSD_STATE, pair_w), F32)

        def carry_state(i, carry, d=d):
            c = i if d == 0 else nchunks - 1 - i
            st = st_s[...]
            ent_s[d, c] = st.astype(BF16)
            st_s[...] = st * dtot_s[d, c][0:1, :] + new_s[d, c]
            return carry

        lax.fori_loop(0, nchunks, carry_state, 0, unroll=2)
        for pr in range(SSD_HPG // 2 if has_st else 0):
            ps = slice(pr * pair_w, (pr + 1) * pair_w)
            st_ref[d, 2 * pr:2 * pr + 2] = st_s[:, ps].T.reshape(2, SSD_HEADDIM, SSD_STATE)

    def finish(c, carry):
        rows = pl.ds(pl.multiple_of(c * q, q), q)
        e1 = terms3(jnp.exp(csr_s[rows, :]))
        cc = cs_s[rows, :]
        y = ys_s[rows, :] + dsk_ref[...] * xs_s[rows, :]
        for d in range(2):
            y = y + _dot(cc, ent_s[d, c]) * _dot(e1, exp_s[d])
        y = y * _silu(z_ref[rows, :].astype(F32))
        y = y * lax.rsqrt(jnp.mean(y * y, axis=-1, keepdims=True) + EPS) * nw_ref[...]
        y_ref[rows, :] = y.astype(BF16)
        return carry

    lax.fori_loop(0, nchunks, finish, 0, unroll=2)


def _ssd(u, dt, nseq, seq, conv_w, conv_b, dt_bias, a_log, d_skip, norm_w, *,
         h0=None, h0_layer=0, st_layers=0, st_layer=0, st_prev=None):
    t = nseq * seq
    g = SSD_GROUPS
    e = SSD_HPG
    heads = g * e
    gw = e * SSD_HEADDIM
    d_inner = heads * SSD_HEADDIM
    nc = seq // SSD_CHUNK
    xb0 = d_inner // gw
    bb0 = 2 * d_inner // SSD_STATE
    per_group = lambda v: v.reshape(2, g, e).transpose(1, 0, 2).reshape(g, 2 * e)
    rep = LANE // (2 * e)
    dbg = per_group(dt_bias)
    alg = per_group(a_log)
    dbg_rep = jnp.tile(dbg, (1, rep)).reshape(g, 1, LANE)
    alg_rep = jnp.tile(alg, (1, rep)).reshape(g, 1, LANE)
    dsk = jnp.repeat(d_skip, SSD_HEADDIM).reshape(1, d_inner)
    cw = lambda width, blk0: pl.BlockSpec((CONV_W, width), lambda b, gi: (0, blk0 + gi))
    cbias = lambda width, blk0: pl.BlockSpec((1, width), lambda b, gi: (0, blk0 + gi))
    st_block = (None, None, 2, e, SSD_HEADDIM, SSD_STATE)
    in_specs = [
        pl.BlockSpec((seq, gw), lambda b, gi: (b, gi)),
        pl.BlockSpec((seq, gw), lambda b, gi: (b, xb0 + gi)),
        pl.BlockSpec((seq, SSD_STATE), lambda b, gi: (b, bb0 + gi)),
        pl.BlockSpec((seq, SSD_STATE), lambda b, gi: (b, bb0 + g + gi)),
        pl.BlockSpec((seq, LANE), lambda b, gi: (b, 0)),
        pl.BlockSpec((None, 1, LANE), lambda b, gi: (gi, 0, 0)),
        pl.BlockSpec((None, 1, LANE), lambda b, gi: (gi, 0, 0)),
        pl.BlockSpec((None, 2 * e, 1), lambda b, gi: (gi, 0, 0)),
        cw(gw, 0), cw(SSD_STATE, d_inner // SSD_STATE), cw(SSD_STATE, d_inner // SSD_STATE + g),
        cbias(gw, 0), cbias(SSD_STATE, d_inner // SSD_STATE), cbias(SSD_STATE, d_inner // SSD_STATE + g),
        cbias(gw, 0), cbias(gw, 0),
    ]
    conv_b2 = conv_b.reshape(1, -1)
    args = [u, u, u, u, dt, dbg_rep, alg_rep, alg.reshape(g, 2 * e, 1),
            conv_w, conv_w, conv_w, conv_b2, conv_b2, conv_b2, dsk, norm_w.reshape(1, d_inner)]
    if h0 is not None:
        in_specs.append(pl.BlockSpec(st_block, lambda b, gi: (b, h0_layer, 0, gi, 0, 0)))
        args.append(h0)
    out_specs = [pl.BlockSpec((seq, gw), lambda b, gi: (b, gi))]
    out_shape = [jax.ShapeDtypeStruct((t, d_inner), BF16)]
    aliases = {}
    if st_layers:
        if st_prev is not None:
            aliases = {len(args): 1}
            in_specs.append(pl.BlockSpec(memory_space=pl.ANY))
            args.append(st_prev)
        out_specs.append(pl.BlockSpec(st_block, lambda b, gi: (b, st_layer, 0, gi, 0, 0)))
        out_shape.append(jax.ShapeDtypeStruct((nseq, st_layers, 2, heads, SSD_HEADDIM, SSD_STATE), F32))
    return pl.pallas_call(
        functools.partial(_ssd_kernel, seq=seq, has_h0=h0 is not None, has_prev=st_prev is not None,
                          has_st=bool(st_layers)),
        grid=(nseq, g),
        in_specs=in_specs,
        out_specs=out_specs,
        out_shape=out_shape,
        input_output_aliases=aliases,
        scratch_shapes=[
            pltpu.VMEM((seq + 2 * HALO, gw), BF16),
            pltpu.VMEM((seq + 2 * HALO, SSD_STATE), BF16),
            pltpu.VMEM((seq + 2 * HALO, SSD_STATE), BF16),
            pltpu.VMEM((seq, gw), F32),
            pltpu.VMEM((SSD_STATE, seq), F32),
            pltpu.VMEM((seq, SSD_STATE), BF16),
            pltpu.VMEM((seq, gw), F32),
            pltpu.VMEM((SSD_STATE, gw), F32),
            pltpu.VMEM((2, SSD_CHUNK, gw), BF16),
            pltpu.VMEM((2, SSD_CHUNK, LANE), BF16),
            pltpu.VMEM((seq, LANE), F32),
            pltpu.VMEM((2, nc, SUBLANE, gw), F32),
            pltpu.VMEM((2, nc, SSD_STATE, gw), F32),
            pltpu.VMEM((2, nc, SSD_STATE, gw), BF16),
            pltpu.VMEM((SSD_CHUNK, LANE), BF16),
        ],
        compiler_params=_cparams("parallel", "parallel"),
        name="ssd",
    )(*args)


def _rope_tables(seq):
    rows = seq // GRID_W
    row = jnp.repeat(jnp.arange(rows), GRID_W).astype(F32)
    col = jnp.tile(jnp.arange(GRID_W), rows).astype(F32)
    quarter = DA_HALF // 4
    inv = ROPE_THETA ** (-jnp.arange(quarter, dtype=F32) / quarter)
    ang_r = row[:, None] * inv
    ang_c = col[:, None] * inv
    ang = jnp.concatenate([ang_r, ang_r, ang_c, ang_c], axis=-1)
    sign = jnp.tile(jnp.concatenate([-jnp.ones((quarter,), F32), jnp.ones((quarter,), F32)]), 2)
    cos = jnp.cos(ang)
    sin = jnp.sin(ang) * sign
    return jnp.tile(cos, (1, 2)), jnp.tile(sin, (1, 2))


def kernel(x_prompt, x_sample, c, cache_attn_k, cache_attn_v, state_lru, state_ssd, c_ctx, w_ada, b_ada, norm_g, lru_conv_w, lru_conv_b, lru_w_r, lru_b_r, lru_w_i, lru_b_i, lru_lambda, even_w_in, even_w_out, da_q_norm, da_k_norm, da_lambda, da_subln, ssd_w_in, ssd_conv_w, ssd_conv_b, ssd_dt_bias, ssd_a_log, ssd_d, ssd_norm_w, ssd_w_out, ffn_w_in, ffn_w_out):
    depth = w_ada.shape[0]
    batch, seq_p, d = x_prompt.shape
    dec_batch, seq_s, _ = x_sample.shape
    past = cache_attn_k.shape[2]
    heads = cache_attn_k.shape[3]
    d_rnn = lru_conv_w.shape[2]
    d_ff = ffn_w_out.shape[1]
    d_inner = ssd_w_out.shape[1]
    n_even = even_w_in.shape[0]

    cond = jnp.zeros((N_COND, d), F32).at[0].set(c_ctx).at[1:1 + dec_batch].set(c)
    mod_all = _ada(cond, w_ada, b_ada).reshape(depth, N_COND, 6, d)
    rope = _rope_tables(seq_s)
    cache_k = cache_attn_k.reshape(dec_batch, n_even, past, heads * DA_VDIM)
    cache_v = cache_attn_v.reshape(dec_batch, n_even, past, heads * DA_VDIM)

    even_in_b, even_out_b = even_w_in.astype(BF16), even_w_out.astype(BF16)
    n_odd = ssd_w_in.shape[0]
    ssd_heads = d_inner // SSD_HEADDIM
    n_main = ssd_w_in.shape[2] - 2 * ssd_heads
    ssd_in_b, ssd_out_b = ssd_w_in.astype(BF16), ssd_w_out.astype(BF16)
    hpg = ssd_heads // SSD_GROUPS
    ssd_dt_b = ssd_w_in[:, :, n_main:].astype(BF16).reshape(n_odd, d, 2, SSD_GROUPS, hpg).transpose(
        0, 1, 3, 2, 4).reshape(n_odd, d, 2 * ssd_heads)
    ffn_in_b, ffn_out_b = ffn_w_in.astype(BF16), ffn_w_out.astype(BF16)

    xs = [x_prompt.reshape(batch * seq_p, d), x_sample.reshape(dec_batch * seq_s, d)]
    shapes = [(batch, seq_p), (dec_batch, seq_s)]
    new_kv, new_lru, new_ssd = None, [], None

    for i in range(depth):
        j = i // 2
        mods = [mod_all[i, 0:1], mod_all[i, 1:1 + dec_batch]]
        g_mix = norm_g[i, 0].reshape(1, d)
        g_ffn = norm_g[i, 1].reshape(1, d)
        if i % 2 == 0:
            lam_init = 0.8 - 0.6 * math.exp(-0.3 * i)
            w_r = lru_w_r[j].astype(BF16)
            w_i = lru_w_i[j].astype(BF16)
            for s in range(2):
                nseq, seq = shapes[s]
                u = _inproj(xs[s], g_mix, mods[s], even_in_b, j, mode="plain", n_out=even_in_b.shape[2],
                            tn=2560, shift_idx=0, scale_idx=1, name="even_in")
                h0 = jnp.zeros((nseq, 2, d_rnn), F32) if s == 0 else state_lru[:, j]
                rec, s_fin = _lru(u, nseq, seq, lru_conv_w[j], lru_conv_b[j], w_r, w_i,
                                  lru_b_r[j], lru_b_i[j], lru_lambda[j], h0)
                if s == 0:
                    att, kc, vc = _attn(u, nseq, seq, heads, lam_init, da_q_norm[j], da_k_norm[j],
                                        da_lambda[j], da_subln[j], kv_layers=n_even, kv_layer=j,
                                        kv_prev=new_kv)
                    new_kv = (kc, vc)
                    new_lru.append(s_fin)
                else:
                    att = _attn(u, nseq, seq, heads, lam_init, da_q_norm[j], da_k_norm[j],
                                da_lambda[j], da_subln[j], ctx=(cache_k, cache_v, j), rope=rope)
                xs[s] = _outproj([rec, att], even_out_b, j, xs[s], mods[s], gate_idx=2, tn=1024,
                                 name="even_out")
        else:
            for s in range(2):
                nseq, seq = shapes[s]
                u, dt = _inproj(xs[s], g_mix, mods[s], ssd_in_b, j, mode="dt", n_out=n_main, tn=2048,
                                shift_idx=0, scale_idx=1, name="odd_in", w_dt=ssd_dt_b)
                ssd_args = (u, dt, nseq, seq, ssd_conv_w[j], ssd_conv_b[j], ssd_dt_bias[j],
                            ssd_a_log[j], ssd_d[j], ssd_norm_w[j])
                if s == 0:
                    y, new_ssd = _ssd(*ssd_args, st_layers=n_odd, st_layer=j, st_prev=new_ssd)
                else:
                    y, = _ssd(*ssd_args, h0=state_ssd, h0_layer=j)
                xs[s] = _outproj([y], ssd_out_b, j, xs[s], mods[s], gate_idx=2, tn=1024, name="odd_out")
        for s in range(2):
            act = _inproj(xs[s], g_ffn, mods[s], ffn_in_b, i, mode="glu", n_out=d_ff, tn=512,
                          shift_idx=3, scale_idx=4, name="ffn_in")
            xs[s] = _outproj([act], ffn_out_b, i, xs[s], mods[s], gate_idx=5, tn=512, name="ffn_out")

    return (xs[0].reshape(batch, seq_p, d), xs[1].reshape(dec_batch, seq_s, d),
            new_kv[0].reshape(batch, n_even, seq_p, heads, DA_VDIM),
            new_kv[1].reshape(batch, n_even, seq_p, heads, DA_VDIM),
            jnp.stack(new_lru, axis=1), new_ssd)
```

```python
import functools
import math

import jax
import jax.numpy as jnp
from jax import lax
from jax.experimental import pallas as pl
from jax.experimental.pallas import tpu as pltpu

F32 = jnp.float32
BF16 = jnp.bfloat16

EPS = 1e-6
GRID_W = 64
CONV_W = 4
CONV_LEFT = CONV_W // 2
LRU_BW = 128
LRU_C = 8.0
DA_HALF = 64
DA_VDIM = 2 * DA_HALF
ROPE_THETA = 10000.0
SSD_HEADDIM = 64
SSD_GROUPS = 8
SSD_HPG = 8
SSD_STATE = 128
SSD_CHUNK = 128
N_COND = 16

LANE = 128
SUBLANE = 8
VMEM_LIMIT = 56 * 1024 * 1024

ROW_TILE = 1024
MOD_ROWS = 128
CONV_ROWS = 256
CONV_BLK = 128
HALO = CONV_BLK // 2
LOG2E = 1.4426950408889634
SQRT_FLOOR = 1e-37


def _sigmoid(x):
    return 0.5 * jnp.tanh(0.5 * x) + 0.5


def _silu(x):
    return x * _sigmoid(x)


def _cparams(*sem):
    return pltpu.CompilerParams(dimension_semantics=sem, vmem_limit_bytes=VMEM_LIMIT)


def _ada_kernel(c_ref, w_ref, b_ref, o_ref):
    s = _silu(c_ref[...]).astype(BF16)
    o_ref[...] = jnp.dot(s, w_ref[...].astype(BF16), preferred_element_type=F32) + b_ref[...]


def _ada(cond, w_ada, b_ada):
    depth, d, n = w_ada.shape
    tn = 1024
    return pl.pallas_call(
        _ada_kernel,
        grid=(depth, n // tn),
        in_specs=[
            pl.BlockSpec((N_COND, d), lambda l, j: (0, 0)),
            pl.BlockSpec((None, d, tn), lambda l, j: (l, 0, j)),
            pl.BlockSpec((None, 1, tn), lambda l, j: (l, 0, j)),
        ],
        out_specs=pl.BlockSpec((None, N_COND, tn), lambda l, j: (l, 0, j)),
        out_shape=jax.ShapeDtypeStruct((depth, N_COND, n), F32),
        compiler_params=_cparams("parallel", "parallel"),
        name="ada",
    )(cond, w_ada, b_ada.reshape(depth, 1, n))


def _modulate_rows(x_ref, g_ref, mod_ref, h_ref, rows, shift_idx, scale_idx):
    gs = g_ref[...] * (1.0 + mod_ref[scale_idx:scale_idx + 1, :])
    x = x_ref[rows, :]
    ms = jnp.mean(x * x, axis=-1, keepdims=True)
    h_ref[rows, :] = (x * lax.rsqrt(ms + EPS) * gs + mod_ref[shift_idx:shift_idx + 1, :]).astype(BF16)


def _inproj_kernel(*refs, mode, shift_idx, scale_idx, nj):
    if mode == "plain":
        x_ref, g_ref, mod_ref, w_ref, o_ref, ha_ref, hb_ref = refs
    elif mode == "glu":
        x_ref, g_ref, mod_ref, wg_ref, wu_ref, o_ref, ha_ref, hb_ref = refs
    else:
        x_ref, g_ref, mod_ref, w_ref, wdt_ref, o_ref, dt_ref, ha_ref, hb_ref = refs
    i, j = pl.program_id(0), pl.program_id(1)
    nchunk = x_ref.shape[0] // MOD_ROWS

    @pl.when((i == 0) & (j == 0))
    def _():
        def body(r, carry):
            rows = pl.ds(pl.multiple_of(r * MOD_ROWS, MOD_ROWS), MOD_ROWS)
            _modulate_rows(x_ref, g_ref, mod_ref, ha_ref, rows, shift_idx, scale_idx)
            return carry

        lax.fori_loop(0, nchunk, body, 0)

    per_step = -(-nchunk // (nj - 1))
    first = jnp.minimum(jnp.maximum(j - 1, 0) * per_step, nchunk - per_step)

    def step(h_ref, next_ref):
        if mode == "dt":
            @pl.when(j == 0)
            def _():
                dt_ref[...] = jnp.dot(h_ref[...], wdt_ref[...], preferred_element_type=F32)

        h = h_ref[...]
        if mode == "glu":
            g = jnp.dot(h, wg_ref[...], preferred_element_type=F32)
            u = jnp.dot(h, wu_ref[...], preferred_element_type=F32)
            o_ref[...] = (_silu(g) * u).astype(o_ref.dtype)
        else:
            o_ref[...] = jnp.dot(h, w_ref[...], preferred_element_type=F32).astype(o_ref.dtype)
        for n in range(per_step):
            rows = pl.ds(pl.multiple_of((first + n) * MOD_ROWS, MOD_ROWS), MOD_ROWS)
            _modulate_rows(x_ref, g_ref, mod_ref, next_ref, rows, shift_idx, scale_idx)

    @pl.when(i % 2 == 0)
    def _():
        step(ha_ref, hb_ref)

    @pl.when(i % 2 == 1)
    def _():
        step(hb_ref, ha_ref)


def _inproj(x, g, mod, w, layer, *, mode, n_out, tn, shift_idx, scale_idx, name, w_dt=None):
    t, d = x.shape
    tm = ROW_TILE
    rows_per_cond = t // mod.shape[0]
    nj = n_out // tn
    ni = t // tm
    ahead = lambda i, j: jnp.minimum(i + jnp.minimum(j, 1), ni - 1)
    x_spec = pl.BlockSpec((tm, d), lambda i, j: (ahead(i, j), 0))
    g_spec = pl.BlockSpec((1, d), lambda i, j: (0, 0))
    mod_spec = pl.BlockSpec((None, 6, d), lambda i, j: ((ahead(i, j) * tm) // rows_per_cond, 0, 0))
    w_spec = pl.BlockSpec((None, d, tn), lambda i, j: (layer, 0, j))
    o_spec = pl.BlockSpec((tm, tn), lambda i, j: (i, j))
    o_shape = jax.ShapeDtypeStruct((t, n_out), BF16)
    if mode == "plain":
        in_specs = [x_spec, g_spec, mod_spec, w_spec]
        args = (x, g, mod, w)
        out_specs, out_shape = o_spec, o_shape
    elif mode == "glu":
        in_specs = [x_spec, g_spec, mod_spec, w_spec,
                    pl.BlockSpec((None, d, tn), lambda i, j: (layer, 0, nj + j))]
        args = (x, g, mod, w, w)
        out_specs, out_shape = o_spec, o_shape
    else:
        in_specs = [x_spec, g_spec, mod_spec, w_spec,
                    pl.BlockSpec((None, d, LANE), lambda i, j: (layer, 0, 0))]
        args = (x, g, mod, w, w_dt)
        out_specs = (o_spec, pl.BlockSpec((tm, LANE), lambda i, j: (i, 0)))
        out_shape = (o_shape, jax.ShapeDtypeStruct((t, LANE), F32))
    return pl.pallas_call(
        functools.partial(_inproj_kernel, mode=mode, shift_idx=shift_idx, scale_idx=scale_idx, nj=nj),
        grid=(ni, nj),
        in_specs=in_specs,
        out_specs=out_specs,
        out_shape=out_shape,
        scratch_shapes=[pltpu.VMEM((tm, d), BF16), pltpu.VMEM((tm, d), BF16)],
        compiler_params=_cparams("arbitrary", "arbitrary"),
        name=name,
    )(*args)


def _outproj_kernel(*refs, n_a, gate_idx):
    a_refs, w_refs = refs[:n_a], refs[n_a:2 * n_a]
    x_ref, mod_ref, o_ref = refs[2 * n_a:]
    acc = jnp.dot(a_refs[0][...], w_refs[0][...], preferred_element_type=F32)
    for a_ref, w_ref in zip(a_refs[1:], w_refs[1:]):
        acc = acc + jnp.dot(a_ref[...], w_ref[...], preferred_element_type=F32)
    o_ref[...] = x_ref[...] + mod_ref[gate_idx:gate_idx + 1, :] * acc


def _outproj(a_list, w, layer, x, mod, *, gate_idx, tn, name):
    t, d = x.shape
    tm = ROW_TILE
    rows_per_cond = t // mod.shape[0]
    n_a = len(a_list)
    k = a_list[0].shape[1]
    in_specs = [pl.BlockSpec((tm, k), lambda i, j: (i, 0)) for _ in a_list]
    in_specs += [pl.BlockSpec((None, k, tn), functools.partial(lambda i, j, q: (layer, q, j), q=q))
                 for q in range(n_a)]
    in_specs += [pl.BlockSpec((tm, tn), lambda i, j: (i, j)),
                 pl.BlockSpec((None, 6, tn), lambda i, j: ((i * tm) // rows_per_cond, 0, j))]
    return pl.pallas_call(
        functools.partial(_outproj_kernel, n_a=n_a, gate_idx=gate_idx),
        grid=(t // tm, d // tn),
        in_specs=in_specs,
        out_specs=pl.BlockSpec((tm, tn), lambda i, j: (i, j)),
        out_shape=jax.ShapeDtypeStruct((t, d), F32),
        compiler_params=_cparams("parallel", "parallel"),
        name=name,
    )(*a_list, *([w] * n_a), x, mod)


def _conv_blocks(src_ref, w_ref, b_ref, pad_ref, seq, width, emit):
    assert CONV_LEFT == 2 and CONV_W == 4
    zeros = jnp.zeros((HALO, width), BF16)
    pad_ref[0:HALO, 0:width] = zeros
    pad_ref[HALO + seq:2 * HALO + seq, 0:width] = zeros
    pad_ref[HALO:HALO + seq, 0:width] = src_ref[...]
    taps = [k for k in range(CONV_W) if k != CONV_LEFT]
    r_i = lax.broadcasted_iota(jnp.int32, (len(taps) * CONV_BLK, CONV_BLK), 0)
    c_i = lax.broadcasted_iota(jnp.int32, (len(taps) * CONV_BLK, CONV_BLK), 1)
    tap_i = r_i // CONV_BLK
    off = jnp.where(tap_i >= CONV_LEFT, tap_i + 1, tap_i) - CONV_LEFT
    shift = (c_i == (r_i % CONV_BLK) + off).astype(BF16)
    sub = lax.broadcasted_iota(jnp.int32, (SUBLANE, width), 0)
    edge = 2 * SUBLANE
    w = w_ref[...]
    for r0 in range(0, seq, CONV_BLK):
        lo = HALO + r0
        cur = pad_ref[lo:lo + CONV_BLK, 0:width]
        sh = jnp.dot(shift, cur, preferred_element_type=F32)
        acc = b_ref[...] + w[CONV_LEFT:CONV_LEFT + 1, :] * cur.astype(F32)
        for i, k in enumerate(taps):
            acc = acc + w[k:k + 1, :] * sh[i * CONV_BLK:(i + 1) * CONV_BLK]
        before = pad_ref[lo - edge:lo, 0:width].astype(F32)
        after = pad_ref[lo + CONV_BLK:lo + CONV_BLK + edge, 0:width].astype(F32)
        xm2, xm1, xp0 = before[edge - 2:edge - 1], before[edge - 1:edge], after[0:1]
        top = jnp.where(sub == 0, w[0:1, :] * xm2 + w[1:2, :] * xm1, jnp.where(sub == 1, w[0:1, :] * xm1, 0.0))
        bot = jnp.where(sub == SUBLANE - 1, w[3:4, :] * xp0, 0.0)
        acc = jnp.concatenate([acc[:SUBLANE] + top, acc[SUBLANE:CONV_BLK - SUBLANE],
                               acc[CONV_BLK - SUBLANE:] + bot], axis=0)
        emit(r0, acc)


def _lru_kernel(gate_ref, xr_ref, cw_ref, cb_ref, wr_ref, wi_ref, br_ref, bi_ref, lam_ref, h0_ref,
                rec_ref, sfin_ref, pad_s, xc_s, a_s, b_s, hf_s, hb_s, *, seq, width):
    def put_xc(r0, v):
        xc_s[r0:r0 + CONV_BLK, :] = v

    _conv_blocks(xr_ref, cw_ref, cb_ref, pad_s, seq, width, put_xc)
    step = min(seq, CONV_ROWS)
    tiles = step // SUBLANE

    def gates(d):
        sp = jax.nn.softplus(-lam_ref[d:d + 1, :])
        for r0 in range(0, seq, step):
            t0 = r0 // SUBLANE
            for k in range(width // LRU_BW):
                ks = slice(k * LRU_BW, (k + 1) * LRU_BW)
                xk = xc_s[r0:r0 + step, ks]
                xb = xk.astype(BF16)
                r = _sigmoid(jnp.dot(xb, wr_ref[d, k], preferred_element_type=F32) + br_ref[d:d + 1, ks])
                gi = _sigmoid(jnp.dot(xb, wi_ref[d, k], preferred_element_type=F32) + bi_ref[d:d + 1, ks])
                a = jnp.exp(-LRU_C * r * sp[:, ks])
                y = 1.0 - a * a
                b = y * lax.rsqrt(jnp.maximum(y, SQRT_FLOOR)) * (gi * xk)
                a_s[t0:t0 + tiles, :, ks] = a.reshape(tiles, SUBLANE, LRU_BW)
                b_s[t0:t0 + tiles, :, ks] = b.reshape(tiles, SUBLANE, LRU_BW)

    def scan(out_s, h0, reverse):
        ntile = seq // SUBLANE

        def body(i, h):
            j = ntile - 1 - i if reverse else i
            for r in (range(SUBLANE - 1, -1, -1) if reverse else range(SUBLANE)):
                h = a_s[j, r:r + 1, :] * h + b_s[j, r:r + 1, :]
                out_s[j, r:r + 1, :] = h
            return h

        return lax.fori_loop(0, ntile, body, h0)

    gates(0)
    sfin_ref[0:1, :] = scan(hf_s, h0_ref[0:1, :], False)
    gates(1)
    sfin_ref[1:2, :] = scan(hb_s, h0_ref[1:2, :], True)

    for r0 in range(0, seq, step):
        t0 = r0 // SUBLANE
        h = (hf_s[t0:t0 + tiles] + hb_s[t0:t0 + tiles]).reshape(step, width)
        rec_ref[r0:r0 + step, :] = (h * jax.nn.gelu(gate_ref[r0:r0 + step, :].astype(F32))).astype(BF16)


def _lru(u, nseq, seq, conv_w, conv_b, w_r, w_i, b_r, b_i, lam, h0):
    d_rnn = conv_w.shape[1]
    width = 512
    nb = d_rnn // width
    kb = width // LRU_BW
    vec = lambda rows: pl.BlockSpec((rows, width), lambda b, c: (0, c))
    return pl.pallas_call(
        functools.partial(_lru_kernel, seq=seq, width=width),
        grid=(nseq, nb),
        in_specs=[
            pl.BlockSpec((seq, width), lambda b, c: (b, c)),
            pl.BlockSpec((seq, width), lambda b, c: (b, nb + c)),
            vec(CONV_W), vec(1),
            pl.BlockSpec((2, kb, LRU_BW, LRU_BW), lambda b, c: (0, c, 0, 0)),
            pl.BlockSpec((2, kb, LRU_BW, LRU_BW), lambda b, c: (0, c, 0, 0)),
            vec(2), vec(2), vec(2),
            pl.BlockSpec((None, 2, width), lambda b, c: (b, 0, c)),
        ],
        out_specs=(pl.BlockSpec((seq, width), lambda b, c: (b, c)),
                   pl.BlockSpec((None, 2, width), lambda b, c: (b, 0, c))),
        out_shape=(jax.ShapeDtypeStruct((nseq * seq, d_rnn), BF16),
                   jax.ShapeDtypeStruct((nseq, 2, d_rnn), F32)),
        scratch_shapes=[pltpu.VMEM((seq + 2 * HALO, width), BF16), pltpu.VMEM((seq, width), F32)]
        + [pltpu.VMEM((seq // SUBLANE, SUBLANE, width), F32)] * 4,
        compiler_params=_cparams("parallel", "parallel"),
        name="lru",
    )(u, u, conv_w, conv_b.reshape(1, d_rnn), w_r, w_i, b_r, b_i, lam, h0)


def _half_rms(x, gain):
    x2 = x * x
    s0 = jnp.sum(x2[:, :DA_HALF], axis=-1, keepdims=True)
    s1 = jnp.sum(x2[:, DA_HALF:], axis=-1, keepdims=True)
    lane = lax.broadcasted_iota(jnp.int32, x.shape, 1)
    ms = jnp.where(lane < DA_HALF, s0, s1) * (1.0 / DA_HALF)
    return x * lax.rsqrt(ms + EPS) * gain


def _rope(x, cos, sin_signed):
    q = DA_HALF // 4
    lane = lax.broadcasted_iota(jnp.int32, x.shape, 1)
    rot = jnp.where((lane % (2 * q)) < q, pltpu.roll(x, LANE - q, 1), pltpu.roll(x, q, 1))
    return x * cos + rot * sin_signed


def _attn_kernel(*refs, latent, seq, past, lam_init, sub, has_prev):
    if latent:
        (q_ref, k_ref, v_ref, kc_ref, vc_ref, cosq_ref, sinq_ref, cosk_ref, sink_ref,
         qn_ref, kn_ref, dl_ref, sub_ref, o_ref, kt_s, vx_s) = refs
    else:
        if has_prev:
            refs = refs[:7] + refs[9:]
        (q_ref, k_ref, v_ref, qn_ref, kn_ref, dl_ref, sub_ref, o_ref, ko_ref, vo_ref, kt_s, vx_s) = refs
    hd = DA_VDIM

    @pl.when(pl.program_id(2) == 0)
    def _():
        vx_s[:, hd:2 * hd] = jnp.ones((seq + past, hd), BF16)
        step = min(seq, CONV_ROWS)
        for r0 in range(0, seq, step):
            rows = slice(r0, r0 + step)
            kn = _half_rms(k_ref[rows, :].astype(F32), kn_ref[...])
            if latent:
                kn = _rope(kn, cosk_ref[rows, :], sink_ref[rows, :])
            else:
                ko_ref[rows, :] = kn
                vo_ref[rows, :] = v_ref[rows, :].astype(F32)
            kt_s[:, rows] = kn.T.astype(BF16)
            vx_s[rows, 0:hd] = v_ref[rows, :]
        if latent:
            kt_s[:, seq:seq + past] = kc_ref[...].T.astype(BF16)
            vx_s[seq:seq + past, 0:hd] = vc_ref[...].astype(BF16)

    dl = dl_ref[...]
    lam = (jnp.exp(jnp.sum(dl[0:1] * dl[1:2], axis=-1, keepdims=True))
           - jnp.exp(jnp.sum(dl[2:3] * dl[3:4], axis=-1, keepdims=True)) + lam_init)
    blocks = range(0, q_ref.shape[0], sub)
    scores = []
    for r0 in blocks:
        rows = slice(r0, r0 + sub)
        qn = _half_rms(q_ref[rows, :].astype(F32), qn_ref[...])
        if latent:
            qn = _rope(qn, cosq_ref[rows, :], sinq_ref[rows, :])
        qb = (qn * (DA_HALF ** -0.5 * LOG2E)).astype(BF16)
        scores.append([jnp.dot(qb[:, m * DA_HALF:(m + 1) * DA_HALF], kt_s[m * DA_HALF:(m + 1) * DA_HALF, :],
                               preferred_element_type=F32) for m in range(2)])
    for r0, (s0, s1) in zip(blocks, scores):
        rows = slice(r0, r0 + sub)
        outs = []
        for s in (s0, s1):
            p = jnp.exp2(s - jnp.max(s, axis=-1, keepdims=True)).astype(BF16)
            outs.append(jnp.dot(p, vx_s[...], preferred_element_type=F32))
        o = outs[0][:, :hd] / outs[0][:, hd:] - lam * (outs[1][:, :hd] / outs[1][:, hd:])
        o = o * lax.rsqrt(jnp.mean(o * o, axis=-1, keepdims=True) + EPS) * sub_ref[...] * (1.0 - lam_init)
        o_ref[rows, :] = o.astype(BF16)


def _attn(u, nseq, seq, heads, lam_init, q_norm, k_norm, da_lam, subln, ctx=None, rope=None,
          kv_layers=0, kv_layer=0, kv_prev=None):
    latent = ctx is not None
    hd = DA_VDIM
    sub = 256
    tq = min(seq, 8 * sub)
    nq = seq // tq
    col0 = (u.shape[1] - 3 * heads * hd) // hd
    past = ctx[0].shape[2] if latent else 0
    q_spec = pl.BlockSpec((tq, hd), lambda b, h, i: (b * nq + i, col0 + h))
    k_spec = pl.BlockSpec((seq, hd), lambda b, h, i: (b, col0 + heads + h))
    v_spec = pl.BlockSpec((seq, hd), lambda b, h, i: (b, col0 + 2 * heads + h))
    small = lambda r, c: pl.BlockSpec((r, c), lambda b, h, i: (0, 0))
    par_specs = [small(1, hd), small(1, hd), small(4, DA_HALF), small(1, hd)]
    pars = (q_norm.reshape(1, hd), k_norm.reshape(1, hd), da_lam, subln.reshape(1, hd))
    o_spec = pl.BlockSpec((tq, hd), lambda b, h, i: (b * nq + i, h))
    o_shape = jax.ShapeDtypeStruct((nseq * seq, heads * hd), BF16)
    aliases = {}
    if latent:
        kc, vc, layer = ctx
        cos, sin = rope
        c_spec = pl.BlockSpec((None, None, past, hd), lambda b, h, i: (b, layer, 0, h))
        tq_spec = pl.BlockSpec((tq, hd), lambda b, h, i: (i, 0))
        tk_spec = pl.BlockSpec((seq, hd), lambda b, h, i: (0, 0))
        in_specs = [q_spec, k_spec, v_spec, c_spec, c_spec, tq_spec, tq_spec, tk_spec, tk_spec] + par_specs
        args = (u, u, u, kc, vc, cos, sin, cos, sin) + pars
        out_specs, out_shape = o_spec, o_shape
    else:
        in_specs = [q_spec, k_spec, v_spec] + par_specs
        args = (u, u, u) + pars
        if kv_prev is not None:
            aliases = {len(args): 1, len(args) + 1: 2}
            in_specs += [pl.BlockSpec(memory_space=pl.ANY)] * 2
            args += tuple(kv_prev)
        kv_spec = pl.BlockSpec((None, None, seq, hd), lambda b, h, i: (b, kv_layer, 0, h))
        kv_shape = jax.ShapeDtypeStruct((nseq, kv_layers, seq, heads * hd), F32)
        out_specs, out_shape = (o_spec, kv_spec, kv_spec), (o_shape, kv_shape, kv_shape)
    return pl.pallas_call(
        functools.partial(_attn_kernel, latent=latent, seq=seq, past=past, lam_init=lam_init, sub=sub,
                          has_prev=kv_prev is not None),
        grid=(nseq, heads, nq),
        in_specs=in_specs,
        out_specs=out_specs,
        out_shape=out_shape,
        input_output_aliases=aliases,
        scratch_shapes=[pltpu.VMEM((hd, seq + past), BF16), pltpu.VMEM((seq + past, 2 * hd), BF16)],
        compiler_params=_cparams("parallel", "parallel", "arbitrary"),
        name="attn",
    )(*args)


def _split3(x):
    hi = x.astype(BF16)
    r1 = x - hi.astype(F32)
    mid = r1.astype(BF16)
    lo = (r1 - mid.astype(F32)).astype(BF16)
    return hi, mid, lo


def _dot(a, b):
    return jnp.dot(a, b, preferred_element_type=F32)


def _ssd_kernel(*refs, seq, has_h0, has_prev, has_st):
    (z_ref, x_ref, b_ref, c_ref, dt_ref, dbc_ref, alc_ref, alr_ref,
     cwx_ref, cwb_ref, cwc_ref, cbx_ref, cbb_ref, cbc_ref, dsk_ref, nw_ref) = refs[:16]
    rest = refs[16:]
    if has_h0:
        h0_ref, rest = rest[0], rest[1:]
    if has_prev:
        rest = rest[1:]
    y_ref, rest = rest[0], rest[1:]
    if has_st:
        st_ref, rest = rest[0], rest[1:]
    (pad_s, padb_s, padc_s, xs_s, bt_s, cs_s, ys_s, st_s, exp_s, half_s,
     csr_s, dtot_s, new_s, ent_s, sel_s) = rest

    q = SSD_CHUNK
    nslot = 2 * SSD_HPG
    pair_w = 2 * SSD_HEADDIM
    gw = xs_s.shape[1]

    def put_x(r0, v):
        xs_s[r0:r0 + CONV_BLK, :] = _silu(v)

    def put_b(r0, v):
        bt_s[:, r0:r0 + CONV_BLK] = _silu(v).T

    def put_c(r0, v):
        cs_s[r0:r0 + CONV_BLK, :] = _silu(v).astype(BF16)

    _conv_blocks(x_ref, cwx_ref, cbx_ref, pad_s, seq, gw, put_x)
    _conv_blocks(b_ref, cwb_ref, cbb_ref, padb_s, seq, SSD_STATE, put_b)
    _conv_blocks(c_ref, cwc_ref, cbc_ref, padc_s, seq, SSD_STATE, put_c)

    li = lax.broadcasted_iota(jnp.int32, (q, q), 0)
    si = lax.broadcasted_iota(jnp.int32, (q, q), 1)
    lower = (li >= si)
    upper = (li <= si)
    lower_b = lower.astype(BF16)
    upper_b = upper.astype(BF16)
    a_rep = -jnp.exp(alc_ref[...])
    a_row = -jnp.exp(alr_ref[...])
    for i in range(2):
        half_s[i] = (si // SSD_HEADDIM == i).astype(BF16)
    sel_s[...] = (li == pl.program_id(1) * nslot + si % nslot).astype(BF16)
    lower2 = jnp.concatenate([lower, lower], axis=1)
    diag2 = jnp.concatenate([li == si, li == si], axis=1)
    ki = lax.broadcasted_iota(jnp.int32, (q, gw), 0)
    ji = lax.broadcasted_iota(jnp.int32, (q, gw), 1)
    for d in range(2):
        exp_s[d] = ((ki % nslot == d * SSD_HPG + ji // SSD_HEADDIM) & (ki < 3 * nslot)).astype(BF16)

    def terms3(v):
        lane = lax.broadcasted_iota(jnp.int32, v.shape, 1)
        hi = v.astype(BF16).astype(F32)
        mid = (v - hi).astype(BF16).astype(F32)
        lo = v - hi - mid
        return jnp.where(lane < nslot, hi, jnp.where(lane < 2 * nslot, mid, jnp.where(
            lane < 3 * nslot, lo, 0.0))).astype(BF16)

    fwd_lane = (si % nslot) < SSD_HPG
    fwd_row = lax.broadcasted_iota(jnp.int32, (nslot, q), 0) < SSD_HPG

    def cums(c):
        rows = pl.ds(pl.multiple_of(c * q, q), q)
        raw_rep = sum(_dot(part, sel_s[...]) for part in _split3(dt_ref[rows, :]))
        dt_rep = jax.nn.softplus(raw_rep + dbc_ref[...])
        da_rep = dt_rep * a_rep
        dt_row = dt_rep.T[0:nslot]
        da_row = dt_row * a_row
        csf_rep = sum(_dot(lower_b, part) for part in _split3(da_rep))
        csf_row = sum(_dot(part, upper_b) for part in _split3(da_row))
        bt = bt_s[:, rows]
        cb = _dot(cs_s[rows, :], bt.astype(BF16))
        return rows, dt_row, da_rep, da_row, csf_rep, csf_row, bt, cb

    def segments(c, stage1):
        rows, dt_row, da_rep, da_row, csf_rep, csf_row, bt, cb = stage1
        tot_rep = csf_rep[q - 1:q, :]
        tot_row = csf_row[:, q - 1:q]
        cs_rep = jnp.where(fwd_lane, csf_rep, tot_rep - csf_rep + da_rep)
        cs_row = jnp.where(fwd_row, csf_row, tot_row - csf_row + da_row)
        csr_s[rows, :] = cs_rep
        et = terms3(jnp.broadcast_to(jnp.exp(tot_rep), (SUBLANE, q)))
        for d in range(2):
            dtot_s[d, c] = _dot(et, exp_s[d])
        w_row = dt_row * jnp.exp(tot_row - cs_row)
        neg_row = cs_row - jnp.log(dt_row)
        seg = lambda s: cs_rep[:, s:s + 1] - neg_row[s:s + 1, :]
        segs = []
        for pr in range(SSD_HPG // 2):
            f0, f1 = 2 * pr, 2 * pr + 1
            b0, b1 = SSD_HPG + f0, SSD_HPG + f1
            segs.append((jnp.concatenate([seg(f0), seg(f1)], axis=1),
                         jnp.concatenate([seg(b0), seg(b1)], axis=1)))
        return rows, dt_row, w_row, bt, cb, segs

    def products(c, stage2):
        rows, dt_row, w_row, bt, cb, segs = stage2
        cb2 = jnp.concatenate([cb, cb], axis=1)
        bt2 = jnp.concatenate([bt, bt], axis=1)
        xb = xs_s[rows, :].astype(BF16)
        row2 = lambda v, s0, s1: jnp.concatenate([v[s0:s0 + 1, :], v[s1:s1 + 1, :]], axis=1)
        for pr in range(SSD_HPG // 2):
            ps = slice(pr * pair_w, (pr + 1) * pair_w)
            f0, f1 = 2 * pr, 2 * pr + 1
            b0, b1 = SSD_HPG + f0, SSD_HPG + f1
            seg_f, seg_b = segs[pr]
            dec = jnp.exp(jnp.where(lower2, seg_f, seg_b)) + jnp.where(diag2, row2(dt_row, b0, b1), 0.0)
            m = (cb2 * dec).astype(BF16)
            btw_f = (bt2 * row2(w_row, f0, f1)).astype(BF16)
            btw_b = (bt2 * row2(w_row, b0, b1)).astype(BF16)
            xp = xb[:, ps]
            xh = jnp.concatenate([xp * half_s[0], xp * half_s[1]], axis=0)
            r = _dot(jnp.concatenate([m, btw_f, btw_b], axis=0), xh)
            ys_s[rows, ps] = r[:q]
            new_s[0, c, :, ps] = r[q:2 * q]
            new_s[1, c, :, ps] = r[2 * q:]

    nchunks = seq // q
    group = 4 if nchunks % 4 == 0 else 2

    def local_pass(i, carry):
        chunks = [group * i + n for n in range(group)]
        stage1 = [cums(c) for c in chunks]
        stage2 = [segments(c, s) for c, s in zip(chunks, stage1)]
        for c, s in zip(chunks, stage2):
            products(c, s)
        return carry

    lax.fori_loop(0, nchunks // group, local_pass, 0)

    for d in range(2):
        for pr in range(SSD_HPG // 2):
            ps = slice(pr * pair_w, (pr + 1) * pair_w)
            if has_h0:
                st_s[:, ps] = h0_ref[d, 2 * pr:2 * pr + 2].reshape(pair_w, SSD_STATE).T
            else:
                st_s[:, ps] = jnp.zeros((SSD_STATE, pair_w), F32)

        def carry_state(i, carry, d=d):
            c = i if d == 0 else nchunks - 1 - i
            st = st_s[...]
            ent_s[d, c] = st.astype(BF16)
            st_s[...] = st * dtot_s[d, c][0:1, :] + new_s[d, c]
            return carry

        lax.fori_loop(0, nchunks, carry_state, 0, unroll=2)
        for pr in range(SSD_HPG // 2 if has_st else 0):
            ps = slice(pr * pair_w, (pr + 1) * pair_w)
            st_ref[d, 2 * pr:2 * pr + 2] = st_s[:, ps].T.reshape(2, SSD_HEADDIM, SSD_STATE)

    def finish(c, carry):
        rows = pl.ds(pl.multiple_of(c * q, q), q)
        e1 = terms3(jnp.exp(csr_s[rows, :]))
        cc = cs_s[rows, :]
        y = ys_s[rows, :] + dsk_ref[...] * xs_s[rows, :]
        for d in range(2):
            y = y + _dot(cc, ent_s[d, c]) * _dot(e1, exp_s[d])
        y = y * _silu(z_ref[rows, :].astype(F32))
        y = y * lax.rsqrt(jnp.mean(y * y, axis=-1, keepdims=True) + EPS) * nw_ref[...]
        y_ref[rows, :] = y.astype(BF16)
        return carry

    lax.fori_loop(0, nchunks, finish, 0, unroll=2)


def _ssd(u, dt, nseq, seq, conv_w, conv_b, dt_bias, a_log, d_skip, norm_w, *,
         h0=None, h0_layer=0, st_layers=0, st_layer=0, st_prev=None):
    t = nseq * seq
    g = SSD_GROUPS
    e = SSD_HPG
    heads = g * e
    gw = e * SSD_HEADDIM
    d_inner = heads * SSD_HEADDIM
    nc = seq // SSD_CHUNK
    xb0 = d_inner // gw
    bb0 = 2 * d_inner // SSD_STATE
    per_group = lambda v: v.reshape(2, g, e).transpose(1, 0, 2).reshape(g, 2 * e)
    rep = LANE // (2 * e)
    dbg = per_group(dt_bias)
    alg = per_group(a_log)
    dbg_rep = jnp.tile(dbg, (1, rep)).reshape(g, 1, LANE)
    alg_rep = jnp.tile(alg, (1, rep)).reshape(g, 1, LANE)
    dsk = jnp.repeat(d_skip, SSD_HEADDIM).reshape(1, d_inner)
    cw = lambda width, blk0: pl.BlockSpec((CONV_W, width), lambda b, gi: (0, blk0 + gi))
    cbias = lambda width, blk0: pl.BlockSpec((1, width), lambda b, gi: (0, blk0 + gi))
    st_block = (None, None, 2, e, SSD_HEADDIM, SSD_STATE)
    in_specs = [
        pl.BlockSpec((seq, gw), lambda b, gi: (b, gi)),
        pl.BlockSpec((seq, gw), lambda b, gi: (b, xb0 + gi)),
        pl.BlockSpec((seq, SSD_STATE), lambda b, gi: (b, bb0 + gi)),
        pl.BlockSpec((seq, SSD_STATE), lambda b, gi: (b, bb0 + g + gi)),
        pl.BlockSpec((seq, LANE), lambda b, gi: (b, 0)),
        pl.BlockSpec((None, 1, LANE), lambda b, gi: (gi, 0, 0)),
        pl.BlockSpec((None, 1, LANE), lambda b, gi: (gi, 0, 0)),
        pl.BlockSpec((None, 2 * e, 1), lambda b, gi: (gi, 0, 0)),
        cw(gw, 0), cw(SSD_STATE, d_inner // SSD_STATE), cw(SSD_STATE, d_inner // SSD_STATE + g),
        cbias(gw, 0), cbias(SSD_STATE, d_inner // SSD_STATE), cbias(SSD_STATE, d_inner // SSD_STATE + g),
        cbias(gw, 0), cbias(gw, 0),
    ]
    conv_b2 = conv_b.reshape(1, -1)
    args = [u, u, u, u, dt, dbg_rep, alg_rep, alg.reshape(g, 2 * e, 1),
            conv_w, conv_w, conv_w, conv_b2, conv_b2, conv_b2, dsk, norm_w.reshape(1, d_inner)]
    if h0 is not None:
        in_specs.append(pl.BlockSpec(st_block, lambda b, gi: (b, h0_layer, 0, gi, 0, 0)))
        args.append(h0)
    out_specs = [pl.BlockSpec((seq, gw), lambda b, gi: (b, gi))]
    out_shape = [jax.ShapeDtypeStruct((t, d_inner), BF16)]
    aliases = {}
    if st_layers:
        if st_prev is not None:
            aliases = {len(args): 1}
            in_specs.append(pl.BlockSpec(memory_space=pl.ANY))
            args.append(st_prev)
        out_specs.append(pl.BlockSpec(st_block, lambda b, gi: (b, st_layer, 0, gi, 0, 0)))
        out_shape.append(jax.ShapeDtypeStruct((nseq, st_layers, 2, heads, SSD_HEADDIM, SSD_STATE), F32))
    return pl.pallas_call(
        functools.partial(_ssd_kernel, seq=seq, has_h0=h0 is not None, has_prev=st_prev is not None,
                          has_st=bool(st_layers)),
        grid=(nseq, g),
        in_specs=in_specs,
        out_specs=out_specs,
        out_shape=out_shape,
        input_output_aliases=aliases,
        scratch_shapes=[
            pltpu.VMEM((seq + 2 * HALO, gw), BF16),
            pltpu.VMEM((seq + 2 * HALO, SSD_STATE), BF16),
            pltpu.VMEM((seq + 2 * HALO, SSD_STATE), BF16),
            pltpu.VMEM((seq, gw), F32),
            pltpu.VMEM((SSD_STATE, seq), F32),
            pltpu.VMEM((seq, SSD_STATE), BF16),
            pltpu.VMEM((seq, gw), F32),
            pltpu.VMEM((SSD_STATE, gw), F32),
            pltpu.VMEM((2, SSD_CHUNK, gw), BF16),
            pltpu.VMEM((2, SSD_CHUNK, LANE), BF16),
            pltpu.VMEM((seq, LANE), F32),
            pltpu.VMEM((2, nc, SUBLANE, gw), F32),
            pltpu.VMEM((2, nc, SSD_STATE, gw), F32),
            pltpu.VMEM((2, nc, SSD_STATE, gw), BF16),
            pltpu.VMEM((SSD_CHUNK, LANE), BF16),
        ],
        compiler_params=_cparams("parallel", "parallel"),
        name="ssd",
    )(*args)


def _rope_tables(seq):
    rows = seq // GRID_W
    row = jnp.repeat(jnp.arange(rows), GRID_W).astype(F32)
    col = jnp.tile(jnp.arange(GRID_W), rows).astype(F32)
    quarter = DA_HALF // 4
    inv = ROPE_THETA ** (-jnp.arange(quarter, dtype=F32) / quarter)
    ang_r = row[:, None] * inv
    ang_c = col[:, None] * inv
    ang = jnp.concatenate([ang_r, ang_r, ang_c, ang_c], axis=-1)
    sign = jnp.tile(jnp.concatenate([-jnp.ones((quarter,), F32), jnp.ones((quarter,), F32)]), 2)
    cos = jnp.cos(ang)
    sin = jnp.sin(ang) * sign
    return jnp.tile(cos, (1, 2)), jnp.tile(sin, (1, 2))


def kernel(x_prompt, x_sample, c, cache_attn_k, cache_attn_v, state_lru, state_ssd, c_ctx, w_ada, b_ada, norm_g, lru_conv_w, lru_conv_b, lru_w_r, lru_b_r, lru_w_i, lru_b_i, lru_lambda, even_w_in, even_w_out, da_q_norm, da_k_norm, da_lambda, da_subln, ssd_w_in, ssd_conv_w, ssd_conv_b, ssd_dt_bias, ssd_a_log, ssd_d, ssd_norm_w, ssd_w_out, ffn_w_in, ffn_w_out):
    depth = w_ada.shape[0]
    batch, seq_p, d = x_prompt.shape
    dec_batch, seq_s, _ = x_sample.shape
    past = cache_attn_k.shape[2]
    heads = cache_attn_k.shape[3]
    d_rnn = lru_conv_w.shape[2]
    d_ff = ffn_w_out.shape[1]
    d_inner = ssd_w_out.shape[1]
    n_even = even_w_in.shape[0]

    cond = jnp.zeros((N_COND, d), F32).at[0].set(c_ctx).at[1:1 + dec_batch].set(c)
    mod_all = _ada(cond, w_ada, b_ada).reshape(depth, N_COND, 6, d)
    rope = _rope_tables(seq_s)
    cache_k = cache_attn_k.reshape(dec_batch, n_even, past, heads * DA_VDIM)
    cache_v = cache_attn_v.reshape(dec_batch, n_even, past, heads * DA_VDIM)

    even_in_b, even_out_b = even_w_in.astype(BF16), even_w_out.astype(BF16)
    n_odd = ssd_w_in.shape[0]
    ssd_heads = d_inner // SSD_HEADDIM
    n_main = ssd_w_in.shape[2] - 2 * ssd_heads
    ssd_in_b, ssd_out_b = ssd_w_in.astype(BF16), ssd_w_out.astype(BF16)
    hpg = ssd_heads // SSD_GROUPS
    ssd_dt_b = ssd_w_in[:, :, n_main:].astype(BF16).reshape(n_odd, d, 2, SSD_GROUPS, hpg).transpose(
        0, 1, 3, 2, 4).reshape(n_odd, d, 2 * ssd_heads)
    ffn_in_b, ffn_out_b = ffn_w_in.astype(BF16), ffn_w_out.astype(BF16)

    xs = [x_prompt.reshape(batch * seq_p, d), x_sample.reshape(dec_batch * seq_s, d)]
    shapes = [(batch, seq_p), (dec_batch, seq_s)]
    new_kv, new_lru, new_ssd = None, [], None

    for i in range(depth):
        j = i // 2
        mods = [mod_all[i, 0:1], mod_all[i, 1:1 + dec_batch]]
        g_mix = norm_g[i, 0].reshape(1, d)
        g_ffn = norm_g[i, 1].reshape(1, d)
        if i % 2 == 0:
            lam_init = 0.8 - 0.6 * math.exp(-0.3 * i)
            w_r = lru_w_r[j].astype(BF16)
            w_i = lru_w_i[j].astype(BF16)
            for s in range(2):
                nseq, seq = shapes[s]
                u = _inproj(xs[s], g_mix, mods[s], even_in_b, j, mode="plain", n_out=even_in_b.shape[2],
                            tn=1280, shift_idx=0, scale_idx=1, name="even_in")
                h0 = jnp.zeros((nseq, 2, d_rnn), F32) if s == 0 else state_lru[:, j]
                rec, s_fin = _lru(u, nseq, seq, lru_conv_w[j], lru_conv_b[j], w_r, w_i,
                                  lru_b_r[j], lru_b_i[j], lru_lambda[j], h0)
                if s == 0:
                    att, kc, vc = _attn(u, nseq, seq, heads, lam_init, da_q_norm[j], da_k_norm[j],
                                        da_lambda[j], da_subln[j], kv_layers=n_even, kv_layer=j,
                                        kv_prev=new_kv)
                    new_kv = (kc, vc)
                    new_lru.append(s_fin)
                else:
                    att = _attn(u, nseq, seq, heads, lam_init, da_q_norm[j], da_k_norm[j],
                                da_lambda[j], da_subln[j], ctx=(cache_k, cache_v, j), rope=rope)
                xs[s] = _outproj([rec, att], even_out_b, j, xs[s], mods[s], gate_idx=2, tn=1024,
                                 name="even_out")
        else:
            for s in range(2):
                nseq, seq = shapes[s]
                u, dt = _inproj(xs[s], g_mix, mods[s], ssd_in_b, j, mode="dt", n_out=n_main, tn=2048,
                                shift_idx=0, scale_idx=1, name="odd_in", w_dt=ssd_dt_b)
                ssd_args = (u, dt, nseq, seq, ssd_conv_w[j], ssd_conv_b[j], ssd_dt_bias[j],
                            ssd_a_log[j], ssd_d[j], ssd_norm_w[j])
                if s == 0:
                    y, new_ssd = _ssd(*ssd_args, st_layers=n_odd, st_layer=j, st_prev=new_ssd)
                else:
                    y, = _ssd(*ssd_args, h0=state_ssd, h0_layer=j)
                xs[s] = _outproj([y], ssd_out_b, j, xs[s], mods[s], gate_idx=2, tn=1024, name="odd_out")
        for s in range(2):
            act = _inproj(xs[s], g_ffn, mods[s], ffn_in_b, i, mode="glu", n_out=d_ff, tn=512,
                          shift_idx=3, scale_idx=4, name="ffn_in")
            xs[s] = _outproj([act], ffn_out_b, i, xs[s], mods[s], gate_idx=5, tn=512, name="ffn_out")

    return (xs[0].reshape(batch, seq_p, d), xs[1].reshape(dec_batch, seq_s, d),
            new_kv[0].reshape(batch, n_even, seq_p, heads, DA_VDIM),
            new_kv[1].reshape(batch, n_even, seq_p, heads, DA_VDIM),
            jnp.stack(new_lru, axis=1), new_ssd)
```

```python
import functools
import math

import jax
import jax.numpy as jnp
from jax import lax
from jax.experimental import pallas as pl
from jax.experimental.pallas import tpu as pltpu

F32 = jnp.float32
BF16 = jnp.bfloat16

EPS = 1e-6
GRID_W = 64
CONV_W = 4
CONV_LEFT = CONV_W // 2
LRU_BW = 128
LRU_C = 8.0
DA_HALF = 64
DA_VDIM = 2 * DA_HALF
ROPE_THETA = 10000.0
SSD_HEADDIM = 64
SSD_GROUPS = 8
SSD_HPG = 8
SSD_STATE = 128
SSD_CHUNK = 128
N_COND = 16

LANE = 128
SUBLANE = 8
VMEM_LIMIT = 56 * 1024 * 1024

ROW_TILE = 1024
MOD_ROWS = 128
CONV_ROWS = 256
CONV_BLK = 128
HALO = CONV_BLK // 2
LOG2E = 1.4426950408889634
SQRT_FLOOR = 1e-37


def _sigmoid(x):
    return 0.5 * jnp.tanh(0.5 * x) + 0.5


def _silu(x):
    h = 0.5 * x
    return h + h * jnp.tanh(h)


def _cparams(*sem):
    return pltpu.CompilerParams(dimension_semantics=sem, vmem_limit_bytes=VMEM_LIMIT)


def _ada_kernel(c_ref, w_ref, b_ref, o_ref):
    s = _silu(c_ref[...]).astype(BF16)
    o_ref[...] = jnp.dot(s, w_ref[...].astype(BF16), preferred_element_type=F32) + b_ref[...]


def _ada(cond, w_ada, b_ada):
    depth, d, n = w_ada.shape
    tn = 1024
    return pl.pallas_call(
        _ada_kernel,
        grid=(depth, n // tn),
        in_specs=[
            pl.BlockSpec((N_COND, d), lambda l, j: (0, 0)),
            pl.BlockSpec((None, d, tn), lambda l, j: (l, 0, j)),
            pl.BlockSpec((None, 1, tn), lambda l, j: (l, 0, j)),
        ],
        out_specs=pl.BlockSpec((None, N_COND, tn), lambda l, j: (l, 0, j)),
        out_shape=jax.ShapeDtypeStruct((depth, N_COND, n), F32),
        compiler_params=_cparams("parallel", "parallel"),
        name="ada",
    )(cond, w_ada, b_ada.reshape(depth, 1, n))


def _modulate_into(x_ref, g_ref, mod_ref, h_ref, shift_idx, scale_idx):
    gs = g_ref[...] * (1.0 + mod_ref[scale_idx:scale_idx + 1, :])
    shift = mod_ref[shift_idx:shift_idx + 1, :]

    def body(r, carry):
        rows = pl.ds(pl.multiple_of(r * MOD_ROWS, MOD_ROWS), MOD_ROWS)
        x = x_ref[rows, :]
        ms = jnp.mean(x * x, axis=-1, keepdims=True)
        h_ref[rows, :] = (x * lax.rsqrt(ms + EPS) * gs + shift).astype(BF16)
        return carry

    lax.fori_loop(0, x_ref.shape[0] // MOD_ROWS, body, 0)


def _inproj_kernel(*refs, mode, shift_idx, scale_idx):
    if mode == "plain":
        x_ref, g_ref, mod_ref, w_ref, o_ref, h_ref = refs
    elif mode == "glu":
        x_ref, g_ref, mod_ref, wg_ref, wu_ref, o_ref, h_ref = refs
    else:
        x_ref, g_ref, mod_ref, w_ref, wdt_ref, o_ref, dt_ref, h_ref = refs

    @pl.when(pl.program_id(1) == 0)
    def _():
        _modulate_into(x_ref, g_ref, mod_ref, h_ref, shift_idx, scale_idx)
        if mode == "dt":
            dt_ref[...] = jnp.dot(h_ref[...], wdt_ref[...], preferred_element_type=F32)

    h = h_ref[...]
    if mode == "glu":
        g = jnp.dot(h, wg_ref[...], preferred_element_type=F32)
        u = jnp.dot(h, wu_ref[...], preferred_element_type=F32)
        o_ref[...] = (_silu(g) * u).astype(o_ref.dtype)
    else:
        o_ref[...] = jnp.dot(h, w_ref[...], preferred_element_type=F32).astype(o_ref.dtype)


def _inproj(x, g, mod, w, layer, *, mode, n_out, tn, shift_idx, scale_idx, name, w_dt=None):
    t, d = x.shape
    tm = ROW_TILE
    rows_per_cond = t // mod.shape[0]
    nj = n_out // tn
    x_spec = pl.BlockSpec((tm, d), lambda i, j: (i, 0))
    g_spec = pl.BlockSpec((1, d), lambda i, j: (0, 0))
    mod_spec = pl.BlockSpec((None, 6, d), lambda i, j: ((i * tm) // rows_per_cond, 0, 0))
    w_spec = pl.BlockSpec((None, d, tn), lambda i, j: (layer, 0, j))
    o_spec = pl.BlockSpec((tm, tn), lambda i, j: (i, j))
    o_shape = jax.ShapeDtypeStruct((t, n_out), BF16)
    if mode == "plain":
        in_specs = [x_spec, g_spec, mod_spec, w_spec]
        args = (x, g, mod, w)
        out_specs, out_shape = o_spec, o_shape
    elif mode == "glu":
        in_specs = [x_spec, g_spec, mod_spec, w_spec,
                    pl.BlockSpec((None, d, tn), lambda i, j: (layer, 0, nj + j))]
        args = (x, g, mod, w, w)
        out_specs, out_shape = o_spec, o_shape
    else:
        in_specs = [x_spec, g_spec, mod_spec, w_spec,
                    pl.BlockSpec((None, d, LANE), lambda i, j: (layer, 0, 0))]
        args = (x, g, mod, w, w_dt)
        out_specs = (o_spec, pl.BlockSpec((tm, LANE), lambda i, j: (i, 0)))
        out_shape = (o_shape, jax.ShapeDtypeStruct((t, LANE), F32))
    return pl.pallas_call(
        functools.partial(_inproj_kernel, mode=mode, shift_idx=shift_idx, scale_idx=scale_idx),
        grid=(t // tm, nj),
        in_specs=in_specs,
        out_specs=out_specs,
        out_shape=out_shape,
        scratch_shapes=[pltpu.VMEM((tm, d), BF16)],
        compiler_params=_cparams("parallel", "arbitrary"),
        name=name,
    )(*args)


def _outproj_kernel(*refs, n_a, gate_idx):
    a_refs, w_refs = refs[:n_a], refs[n_a:2 * n_a]
    x_ref, mod_ref, o_ref = refs[2 * n_a:]
    acc = jnp.dot(a_refs[0][...], w_refs[0][...], preferred_element_type=F32)
    for a_ref, w_ref in zip(a_refs[1:], w_refs[1:]):
        acc = acc + jnp.dot(a_ref[...], w_ref[...], preferred_element_type=F32)
    o_ref[...] = x_ref[...] + mod_ref[gate_idx:gate_idx + 1, :] * acc


def _outproj(a_list, w, layer, x, mod, *, gate_idx, tn, name):
    t, d = x.shape
    tm = ROW_TILE
    rows_per_cond = t // mod.shape[0]
    n_a = len(a_list)
    k = a_list[0].shape[1]
    in_specs = [pl.BlockSpec((tm, k), lambda i, j: (i, 0)) for _ in a_list]
    in_specs += [pl.BlockSpec((None, k, tn), functools.partial(lambda i, j, q: (layer, q, j), q=q))
                 for q in range(n_a)]
    in_specs += [pl.BlockSpec((tm, tn), lambda i, j: (i, j)),
                 pl.BlockSpec((None, 6, tn), lambda i, j: ((i * tm) // rows_per_cond, 0, j))]
    return pl.pallas_call(
        functools.partial(_outproj_kernel, n_a=n_a, gate_idx=gate_idx),
        grid=(t // tm, d // tn),
        in_specs=in_specs,
        out_specs=pl.BlockSpec((tm, tn), lambda i, j: (i, j)),
        out_shape=jax.ShapeDtypeStruct((t, d), F32),
        compiler_params=_cparams("parallel", "parallel"),
        name=name,
    )(*a_list, *([w] * n_a), x, mod)


def _conv_blocks(src_ref, w_ref, b_ref, pad_ref, seq, width, emit):
    assert CONV_LEFT == 2 and CONV_W == 4
    zeros = jnp.zeros((HALO, width), BF16)
    pad_ref[0:HALO, 0:width] = zeros
    pad_ref[HALO + seq:2 * HALO + seq, 0:width] = zeros
    pad_ref[HALO:HALO + seq, 0:width] = src_ref[...]
    taps = [k for k in range(CONV_W) if k != CONV_LEFT]
    r_i = lax.broadcasted_iota(jnp.int32, (len(taps) * CONV_BLK, CONV_BLK), 0)
    c_i = lax.broadcasted_iota(jnp.int32, (len(taps) * CONV_BLK, CONV_BLK), 1)
    tap_i = r_i // CONV_BLK
    off = jnp.where(tap_i >= CONV_LEFT, tap_i + 1, tap_i) - CONV_LEFT
    shift = (c_i == (r_i % CONV_BLK) + off).astype(BF16)
    sub = lax.broadcasted_iota(jnp.int32, (SUBLANE, width), 0)
    edge = 2 * SUBLANE
    w = w_ref[...]
    for r0 in range(0, seq, CONV_BLK):
        lo = HALO + r0
        cur = pad_ref[lo:lo + CONV_BLK, 0:width]
        sh = jnp.dot(shift, cur, preferred_element_type=F32)
        acc = b_ref[...] + w[CONV_LEFT:CONV_LEFT + 1, :] * cur.astype(F32)
        for i, k in enumerate(taps):
            acc = acc + w[k:k + 1, :] * sh[i * CONV_BLK:(i + 1) * CONV_BLK]
        before = pad_ref[lo - edge:lo, 0:width].astype(F32)
        after = pad_ref[lo + CONV_BLK:lo + CONV_BLK + edge, 0:width].astype(F32)
        xm2, xm1, xp0 = before[edge - 2:edge - 1], before[edge - 1:edge], after[0:1]
        top = jnp.where(sub == 0, w[0:1, :] * xm2 + w[1:2, :] * xm1, jnp.where(sub == 1, w[0:1, :] * xm1, 0.0))
        bot = jnp.where(sub == SUBLANE - 1, w[3:4, :] * xp0, 0.0)
        acc = jnp.concatenate([acc[:SUBLANE] + top, acc[SUBLANE:CONV_BLK - SUBLANE],
                               acc[CONV_BLK - SUBLANE:] + bot], axis=0)
        emit(r0, acc)


def _lru_kernel(gate_ref, xr_ref, cw_ref, cb_ref, wr_ref, wi_ref, br_ref, bi_ref, lam_ref, h0_ref,
                rec_ref, sfin_ref, pad_s, xc_s, a_s, b_s, hf_s, hb_s, *, seq, width):
    def put_xc(r0, v):
        xc_s[r0:r0 + CONV_BLK, :] = v

    _conv_blocks(xr_ref, cw_ref, cb_ref, pad_s, seq, width, put_xc)
    step = min(seq, CONV_ROWS)
    tiles = step // SUBLANE

    def gates(d):
        sp = jax.nn.softplus(-lam_ref[d:d + 1, :])
        for r0 in range(0, seq, step):
            t0 = r0 // SUBLANE
            for k in range(width // LRU_BW):
                ks = slice(k * LRU_BW, (k + 1) * LRU_BW)
                xk = xc_s[r0:r0 + step, ks]
                xb = xk.astype(BF16)
                r = _sigmoid(jnp.dot(xb, wr_ref[d, k], preferred_element_type=F32) + br_ref[d:d + 1, ks])
                gi = _sigmoid(jnp.dot(xb, wi_ref[d, k], preferred_element_type=F32) + bi_ref[d:d + 1, ks])
                a = jnp.exp(-LRU_C * r * sp[:, ks])
                y = 1.0 - a * a
                b = y * lax.rsqrt(jnp.maximum(y, SQRT_FLOOR)) * (gi * xk)
                a_s[t0:t0 + tiles, :, ks] = a.reshape(tiles, SUBLANE, LRU_BW)
                b_s[t0:t0 + tiles, :, ks] = b.reshape(tiles, SUBLANE, LRU_BW)

    def scan(out_s, h0, reverse):
        ntile = seq // SUBLANE

        def body(i, h):
            j = ntile - 1 - i if reverse else i
            for r in (range(SUBLANE - 1, -1, -1) if reverse else range(SUBLANE)):
                h = a_s[j, r:r + 1, :] * h + b_s[j, r:r + 1, :]
                out_s[j, r:r + 1, :] = h
            return h

        return lax.fori_loop(0, ntile, body, h0)

    gates(0)
    sfin_ref[0:1, :] = scan(hf_s, h0_ref[0:1, :], False)
    gates(1)
    sfin_ref[1:2, :] = scan(hb_s, h0_ref[1:2, :], True)

    for r0 in range(0, seq, step):
        t0 = r0 // SUBLANE
        h = (hf_s[t0:t0 + tiles] + hb_s[t0:t0 + tiles]).reshape(step, width)
        rec_ref[r0:r0 + step, :] = (h * jax.nn.gelu(gate_ref[r0:r0 + step, :].astype(F32))).astype(BF16)


def _lru(u, nseq, seq, conv_w, conv_b, w_r, w_i, b_r, b_i, lam, h0):
    d_rnn = conv_w.shape[1]
    width = 512
    nb = d_rnn // width
    kb = width // LRU_BW
    vec = lambda rows: pl.BlockSpec((rows, width), lambda b, c: (0, c))
    return pl.pallas_call(
        functools.partial(_lru_kernel, seq=seq, width=width),
        grid=(nseq, nb),
        in_specs=[
            pl.BlockSpec((seq, width), lambda b, c: (b, c)),
            pl.BlockSpec((seq, width), lambda b, c: (b, nb + c)),
            vec(CONV_W), vec(1),
            pl.BlockSpec((2, kb, LRU_BW, LRU_BW), lambda b, c: (0, c, 0, 0)),
            pl.BlockSpec((2, kb, LRU_BW, LRU_BW), lambda b, c: (0, c, 0, 0)),
            vec(2), vec(2), vec(2),
            pl.BlockSpec((None, 2, width), lambda b, c: (b, 0, c)),
        ],
        out_specs=(pl.BlockSpec((seq, width), lambda b, c: (b, c)),
                   pl.BlockSpec((None, 2, width), lambda b, c: (b, 0, c))),
        out_shape=(jax.ShapeDtypeStruct((nseq * seq, d_rnn), BF16),
                   jax.ShapeDtypeStruct((nseq, 2, d_rnn), F32)),
        scratch_shapes=[pltpu.VMEM((seq + 2 * HALO, width), BF16), pltpu.VMEM((seq, width), F32)]
        + [pltpu.VMEM((seq // SUBLANE, SUBLANE, width), F32)] * 4,
        compiler_params=_cparams("parallel", "parallel"),
        name="lru",
    )(u, u, conv_w, conv_b.reshape(1, d_rnn), w_r, w_i, b_r, b_i, lam, h0)


def _half_rms(x, gain):
    x2 = x * x
    s0 = jnp.sum(x2[:, :DA_HALF], axis=-1, keepdims=True)
    s1 = jnp.sum(x2[:, DA_HALF:], axis=-1, keepdims=True)
    lane = lax.broadcasted_iota(jnp.int32, x.shape, 1)
    ms = jnp.where(lane < DA_HALF, s0, s1) * (1.0 / DA_HALF)
    return x * lax.rsqrt(ms + EPS) * gain


def _rope(x, cos, sin_signed):
    q = DA_HALF // 4
    lane = lax.broadcasted_iota(jnp.int32, x.shape, 1)
    rot = jnp.where((lane % (2 * q)) < q, pltpu.roll(x, LANE - q, 1), pltpu.roll(x, q, 1))
    return x * cos + rot * sin_signed


def _attn_kernel(*refs, latent, seq, past, lam_init, sub, has_prev):
    if latent:
        (q_ref, k_ref, v_ref, kc_ref, vc_ref, cosq_ref, sinq_ref, cosk_ref, sink_ref,
         qn_ref, kn_ref, dl_ref, sub_ref, o_ref, kt_s, vx_s) = refs
    else:
        if has_prev:
            refs = refs[:7] + refs[9:]
        (q_ref, k_ref, v_ref, qn_ref, kn_ref, dl_ref, sub_ref, o_ref, ko_ref, vo_ref, kt_s, vx_s) = refs
    hd = DA_VDIM
    dl = dl_ref[...]
    lam = (jnp.exp(jnp.sum(dl[0:1] * dl[1:2], axis=-1, keepdims=True))
           - jnp.exp(jnp.sum(dl[2:3] * dl[3:4], axis=-1, keepdims=True)) + lam_init)

    for hh in range(kt_s.shape[0]):
        hs = slice(hh * hd, (hh + 1) * hd)
        vx_s[hh, :, hd:2 * hd] = jnp.ones((seq + past, hd), BF16)
        step = min(seq, CONV_ROWS)
        for r0 in range(0, seq, step):
            rows = slice(r0, r0 + step)
            kn = _half_rms(k_ref[rows, hs].astype(F32), kn_ref[...])
            if latent:
                kn = _rope(kn, cosk_ref[rows, :], sink_ref[rows, :])
            else:
                ko_ref[rows, hs] = kn
                vo_ref[rows, hs] = v_ref[rows, hs].astype(F32)
            kt_s[hh, :, rows] = kn.T.astype(BF16)
            vx_s[hh, rows, 0:hd] = v_ref[rows, hs]
        if latent:
            kt_s[hh, :, seq:seq + past] = kc_ref[...].T.astype(BF16)
            vx_s[hh, seq:seq + past, 0:hd] = vc_ref[...].astype(BF16)

        blocks = range(0, seq, sub)
        scores = []
        for r0 in blocks:
            rows = slice(r0, r0 + sub)
            qn = _half_rms(q_ref[rows, hs].astype(F32), qn_ref[...])
            if latent:
                qn = _rope(qn, cosq_ref[rows, :], sinq_ref[rows, :])
            qb = (qn * (DA_HALF ** -0.5 * LOG2E)).astype(BF16)
            scores.append([jnp.dot(qb[:, m * DA_HALF:(m + 1) * DA_HALF],
                                   kt_s[hh, m * DA_HALF:(m + 1) * DA_HALF, :],
                                   preferred_element_type=F32) for m in range(2)])
        for r0, (s0, s1) in zip(blocks, scores):
            rows = slice(r0, r0 + sub)
            outs = []
            for s in (s0, s1):
                p = jnp.exp2(s - jnp.max(s, axis=-1, keepdims=True)).astype(BF16)
                outs.append(jnp.dot(p, vx_s[hh], preferred_element_type=F32))
            o = outs[0][:, :hd] / outs[0][:, hd:] - lam * (outs[1][:, :hd] / outs[1][:, hd:])
            o = o * lax.rsqrt(jnp.mean(o * o, axis=-1, keepdims=True) + EPS) * sub_ref[...] * (1.0 - lam_init)
            o_ref[rows, hs] = o.astype(BF16)


def _attn(u, nseq, seq, heads, lam_init, q_norm, k_norm, da_lam, subln, ctx=None, rope=None,
          kv_layers=0, kv_layer=0, kv_prev=None):
    latent = ctx is not None
    hd = DA_VDIM
    sub = 256
    hps = 1 if latent else heads
    wid = hps * hd
    nh = heads // hps
    col0 = (u.shape[1] - 3 * heads * hd) // wid
    past = ctx[0].shape[2] if latent else 0
    q_spec = pl.BlockSpec((seq, wid), lambda b, h: (b, col0 + h))
    k_spec = pl.BlockSpec((seq, wid), lambda b, h: (b, col0 + nh + h))
    v_spec = pl.BlockSpec((seq, wid), lambda b, h: (b, col0 + 2 * nh + h))
    small = lambda r, c: pl.BlockSpec((r, c), lambda b, h: (0, 0))
    par_specs = [small(1, hd), small(1, hd), small(4, DA_HALF), small(1, hd)]
    pars = (q_norm.reshape(1, hd), k_norm.reshape(1, hd), da_lam, subln.reshape(1, hd))
    o_spec = pl.BlockSpec((seq, wid), lambda b, h: (b, h))
    o_shape = jax.ShapeDtypeStruct((nseq * seq, heads * hd), BF16)
    aliases = {}
    if latent:
        kc, vc, layer = ctx
        cos, sin = rope
        c_spec = pl.BlockSpec((None, None, past, hd), lambda b, h: (b, layer, 0, h))
        t_spec = pl.BlockSpec((seq, hd), lambda b, h: (0, 0))
        in_specs = [q_spec, k_spec, v_spec, c_spec, c_spec, t_spec, t_spec, t_spec, t_spec] + par_specs
        args = (u, u, u, kc, vc, cos, sin, cos, sin) + pars
        out_specs, out_shape = o_spec, o_shape
    else:
        in_specs = [q_spec, k_spec, v_spec] + par_specs
        args = (u, u, u) + pars
        if kv_prev is not None:
            aliases = {len(args): 1, len(args) + 1: 2}
            in_specs += [pl.BlockSpec(memory_space=pl.ANY)] * 2
            args += tuple(kv_prev)
        kv_spec = pl.BlockSpec((None, None, seq, wid), lambda b, h: (b, kv_layer, 0, h))
        kv_shape = jax.ShapeDtypeStruct((nseq, kv_layers, seq, heads * hd), F32)
        out_specs, out_shape = (o_spec, kv_spec, kv_spec), (o_shape, kv_shape, kv_shape)
    return pl.pallas_call(
        functools.partial(_attn_kernel, latent=latent, seq=seq, past=past, lam_init=lam_init, sub=sub,
                          has_prev=kv_prev is not None),
        grid=(nseq, nh),
        in_specs=in_specs,
        out_specs=out_specs,
        out_shape=out_shape,
        input_output_aliases=aliases,
        scratch_shapes=[pltpu.VMEM((hps, hd, seq + past), BF16), pltpu.VMEM((hps, seq + past, 2 * hd), BF16)],
        compiler_params=_cparams("parallel", "parallel"),
        name="attn",
    )(*args)


def _split3(x):
    hi = x.astype(BF16)
    r1 = x - hi.astype(F32)
    mid = r1.astype(BF16)
    lo = (r1 - mid.astype(F32)).astype(BF16)
    return hi, mid, lo


def _dot(a, b):
    return jnp.dot(a, b, preferred_element_type=F32)


def _ssd_kernel(*refs, seq, has_h0, has_prev, has_st):
    (z_ref, x_ref, b_ref, c_ref, dt_ref, dbc_ref, alc_ref, alr_ref,
     cwx_ref, cwb_ref, cwc_ref, cbx_ref, cbb_ref, cbc_ref, dsk_ref, nw_ref) = refs[:16]
    rest = refs[16:]
    if has_h0:
        h0_ref, rest = rest[0], rest[1:]
    if has_prev:
        rest = rest[1:]
    y_ref, rest = rest[0], rest[1:]
    if has_st:
        st_ref, rest = rest[0], rest[1:]
    (pad_s, padb_s, padc_s, xs_s, bt_s, cs_s, ys_s, st_s, exp_s, half_s,
     csr_s, dtot_s, new_s, ent_s, sel_s) = rest

    q = SSD_CHUNK
    nslot = 2 * SSD_HPG
    pair_w = 2 * SSD_HEADDIM
    gw = xs_s.shape[1]

    def put_x(r0, v):
        xs_s[r0:r0 + CONV_BLK, :] = _silu(v)

    def put_b(r0, v):
        bt_s[:, r0:r0 + CONV_BLK] = _silu(v).T

    def put_c(r0, v):
        cs_s[r0:r0 + CONV_BLK, :] = _silu(v).astype(BF16)

    _conv_blocks(x_ref, cwx_ref, cbx_ref, pad_s, seq, gw, put_x)
    _conv_blocks(b_ref, cwb_ref, cbb_ref, padb_s, seq, SSD_STATE, put_b)
    _conv_blocks(c_ref, cwc_ref, cbc_ref, padc_s, seq, SSD_STATE, put_c)

    li = lax.broadcasted_iota(jnp.int32, (q, q), 0)
    si = lax.broadcasted_iota(jnp.int32, (q, q), 1)
    lower = (li >= si)
    upper = (li <= si)
    lower_b = lower.astype(BF16)
    upper_b = upper.astype(BF16)
    a_rep = -jnp.exp(alc_ref[...])
    a_row = -jnp.exp(alr_ref[...])
    for i in range(2):
        half_s[i] = (si // SSD_HEADDIM == i).astype(BF16)
    sel_s[...] = (li == pl.program_id(1) * nslot + si % nslot).astype(BF16)
    lower2 = jnp.concatenate([lower, lower], axis=1)
    diag2 = jnp.concatenate([li == si, li == si], axis=1)
    ki = lax.broadcasted_iota(jnp.int32, (q, gw), 0)
    ji = lax.broadcasted_iota(jnp.int32, (q, gw), 1)
    for d in range(2):
        exp_s[d] = ((ki % nslot == d * SSD_HPG + ji // SSD_HEADDIM) & (ki < 3 * nslot)).astype(BF16)

    def terms3(v):
        lane = lax.broadcasted_iota(jnp.int32, v.shape, 1)
        hi = v.astype(BF16).astype(F32)
        mid = (v - hi).astype(BF16).astype(F32)
        lo = v - hi - mid
        return jnp.where(lane < nslot, hi, jnp.where(lane < 2 * nslot, mid, jnp.where(
            lane < 3 * nslot, lo, 0.0))).astype(BF16)

    fwd_lane = (si % nslot) < SSD_HPG
    fwd_row = lax.broadcasted_iota(jnp.int32, (nslot, q), 0) < SSD_HPG

    def cums(c):
        rows = pl.ds(pl.multiple_of(c * q, q), q)
        raw_rep = sum(_dot(part, sel_s[...]) for part in _split3(dt_ref[rows, :]))
        dt_rep = jax.nn.softplus(raw_rep + dbc_ref[...])
        da_rep = dt_rep * a_rep
        dt_row = dt_rep.T[0:nslot]
        da_row = dt_row * a_row
        csf_rep = sum(_dot(lower_b, part) for part in _split3(da_rep))
        csf_row = sum(_dot(part, upper_b) for part in _split3(da_row))
        bt = bt_s[:, rows]
        cb = _dot(cs_s[rows, :], bt.astype(BF16))
        return rows, dt_row, da_rep, da_row, csf_rep, csf_row, bt, cb

    def segments(c, stage1):
        rows, dt_row, da_rep, da_row, csf_rep, csf_row, bt, cb = stage1
        tot_rep = csf_rep[q - 1:q, :]
        tot_row = csf_row[:, q - 1:q]
        cs_rep = jnp.where(fwd_lane, csf_rep, tot_rep - csf_rep + da_rep)
        cs_row = jnp.where(fwd_row, csf_row, tot_row - csf_row + da_row)
        csr_s[rows, :] = cs_rep
        et = terms3(jnp.broadcast_to(jnp.exp(tot_rep), (SUBLANE, q)))
        for d in range(2):
            dtot_s[d, c] = _dot(et, exp_s[d])
        w_row = dt_row * jnp.exp(tot_row - cs_row)
        neg_row = cs_row - jnp.log(dt_row)
        seg = lambda s: cs_rep[:, s:s + 1] - neg_row[s:s + 1, :]
        segs = []
        for pr in range(SSD_HPG // 2):
            f0, f1 = 2 * pr, 2 * pr + 1
            b0, b1 = SSD_HPG + f0, SSD_HPG + f1
            segs.append((jnp.concatenate([seg(f0), seg(f1)], axis=1),
                         jnp.concatenate([seg(b0), seg(b1)], axis=1)))
        return rows, dt_row, w_row, bt, cb, segs

    def products(c, stage2):
        rows, dt_row, w_row, bt, cb, segs = stage2
        cb2 = jnp.concatenate([cb, cb], axis=1)
        bt2 = jnp.concatenate([bt, bt], axis=1)
        xb = xs_s[rows, :].astype(BF16)
        row2 = lambda v, s0, s1: jnp.concatenate([v[s0:s0 + 1, :], v[s1:s1 + 1, :]], axis=1)
        for pr in range(SSD_HPG // 2):
            ps = slice(pr * pair_w, (pr + 1) * pair_w)
            f0, f1 = 2 * pr, 2 * pr + 1
            b0, b1 = SSD_HPG + f0, SSD_HPG + f1
            seg_f, seg_b = segs[pr]
            dec = jnp.exp(jnp.where(lower2, seg_f, seg_b)) + jnp.where(diag2, row2(dt_row, b0, b1), 0.0)
            m = (cb2 * dec).astype(BF16)
            btw_f = (bt2 * row2(w_row, f0, f1)).astype(BF16)
            btw_b = (bt2 * row2(w_row, b0, b1)).astype(BF16)
            xp = xb[:, ps]
            xh = jnp.concatenate([xp * half_s[0], xp * half_s[1]], axis=0)
            r = _dot(jnp.concatenate([m, btw_f, btw_b], axis=0), xh)
            ys_s[rows, ps] = r[:q]
            new_s[0, c, :, ps] = r[q:2 * q]
            new_s[1, c, :, ps] = r[2 * q:]

    nchunks = seq // q
    group = 4 if nchunks % 4 == 0 else 2

    def local_pass(i, carry):
        chunks = [group * i + n for n in range(group)]
        stage1 = [cums(c) for c in chunks]
        stage2 = [segments(c, s) for c, s in zip(chunks, stage1)]
        for c, s in zip(chunks, stage2):
            products(c, s)
        return carry

    lax.fori_loop(0, nchunks // group, local_pass, 0)

    for d in range(2):
        for pr in range(SSD_HPG // 2):
            ps = slice(pr * pair_w, (pr + 1) * pair_w)
            if has_h0:
                st_s[:, ps] = h0_ref[d, 2 * pr:2 * pr + 2].reshape(pair_w, SSD_STATE).T
            else:
                st_s[:, ps] = jnp.zeros((SSD_STATE, pair_w), F32)

        def carry_state(i, carry, d=d):
            c = i if d == 0 else nchunks - 1 - i
            st = st_s[...]
            ent_s[d, c] = st.astype(BF16)
            st_s[...] = st * dtot_s[d, c][0:1, :] + new_s[d, c]
            return carry

        lax.fori_loop(0, nchunks, carry_state, 0, unroll=2)
        for pr in range(SSD_HPG // 2 if has_st else 0):
            ps = slice(pr * pair_w, (pr + 1) * pair_w)
            st_ref[d, 2 * pr:2 * pr + 2] = st_s[:, ps].T.reshape(2, SSD_HEADDIM, SSD_STATE)

    def finish(c, carry):
        rows = pl.ds(pl.multiple_of(c * q, q), q)
        e1 = terms3(jnp.exp(csr_s[rows, :]))
        cc = cs_s[rows, :]
        y = ys_s[rows, :] + dsk_ref[...] * xs_s[rows, :]
        for d in range(2):
            y = y + _dot(cc, ent_s[d, c]) * _dot(e1, exp_s[d])
        y = y * _silu(z_ref[rows, :].astype(F32))
        y = y * lax.rsqrt(jnp.mean(y * y, axis=-1, keepdims=True) + EPS) * nw_ref[...]
        y_ref[rows, :] = y.astype(BF16)
        return carry

    lax.fori_loop(0, nchunks, finish, 0, unroll=2)


def _ssd(u, dt, nseq, seq, conv_w, conv_b, dt_bias, a_log, d_skip, norm_w, *,
         h0=None, h0_layer=0, st_layers=0, st_layer=0, st_prev=None):
    t = nseq * seq
    g = SSD_GROUPS
    e = SSD_HPG
    heads = g * e
    gw = e * SSD_HEADDIM
    d_inner = heads * SSD_HEADDIM
    nc = seq // SSD_CHUNK
    xb0 = d_inner // gw
    bb0 = 2 * d_inner // SSD_STATE
    per_group = lambda v: v.reshape(2, g, e).transpose(1, 0, 2).reshape(g, 2 * e)
    rep = LANE // (2 * e)
    dbg = per_group(dt_bias)
    alg = per_group(a_log)
    dbg_rep = jnp.tile(dbg, (1, rep)).reshape(g, 1, LANE)
    alg_rep = jnp.tile(alg, (1, rep)).reshape(g, 1, LANE)
    dsk = jnp.repeat(d_skip, SSD_HEADDIM).reshape(1, d_inner)
    cw = lambda width, blk0: pl.BlockSpec((CONV_W, width), lambda b, gi: (0, blk0 + gi))
    cbias = lambda width, blk0: pl.BlockSpec((1, width), lambda b, gi: (0, blk0 + gi))
    st_block = (None, None, 2, e, SSD_HEADDIM, SSD_STATE)
    in_specs = [
        pl.BlockSpec((seq, gw), lambda b, gi: (b, gi)),
        pl.BlockSpec((seq, gw), lambda b, gi: (b, xb0 + gi)),
        pl.BlockSpec((seq, SSD_STATE), lambda b, gi: (b, bb0 + gi)),
        pl.BlockSpec((seq, SSD_STATE), lambda b, gi: (b, bb0 + g + gi)),
        pl.BlockSpec((seq, LANE), lambda b, gi: (b, 0)),
        pl.BlockSpec((None, 1, LANE), lambda b, gi: (gi, 0, 0)),
        pl.BlockSpec((None, 1, LANE), lambda b, gi: (gi, 0, 0)),
        pl.BlockSpec((None, 2 * e, 1), lambda b, gi: (gi, 0, 0)),
        cw(gw, 0), cw(SSD_STATE, d_inner // SSD_STATE), cw(SSD_STATE, d_inner // SSD_STATE + g),
        cbias(gw, 0), cbias(SSD_STATE, d_inner // SSD_STATE), cbias(SSD_STATE, d_inner // SSD_STATE + g),
        cbias(gw, 0), cbias(gw, 0),
    ]
    conv_b2 = conv_b.reshape(1, -1)
    args = [u, u, u, u, dt, dbg_rep, alg_rep, alg.reshape(g, 2 * e, 1),
            conv_w, conv_w, conv_w, conv_b2, conv_b2, conv_b2, dsk, norm_w.reshape(1, d_inner)]
    if h0 is not None:
        in_specs.append(pl.BlockSpec(st_block, lambda b, gi: (b, h0_layer, 0, gi, 0, 0)))
        args.append(h0)
    out_specs = [pl.BlockSpec((seq, gw), lambda b, gi: (b, gi))]
    out_shape = [jax.ShapeDtypeStruct((t, d_inner), BF16)]
    aliases = {}
    if st_layers:
        if st_prev is not None:
            aliases = {len(args): 1}
            in_specs.append(pl.BlockSpec(memory_space=pl.ANY))
            args.append(st_prev)
        out_specs.append(pl.BlockSpec(st_block, lambda b, gi: (b, st_layer, 0, gi, 0, 0)))
        out_shape.append(jax.ShapeDtypeStruct((nseq, st_layers, 2, heads, SSD_HEADDIM, SSD_STATE), F32))
    return pl.pallas_call(
        functools.partial(_ssd_kernel, seq=seq, has_h0=h0 is not None, has_prev=st_prev is not None,
                          has_st=bool(st_layers)),
        grid=(nseq, g),
        in_specs=in_specs,
        out_specs=out_specs,
        out_shape=out_shape,
        input_output_aliases=aliases,
        scratch_shapes=[
            pltpu.VMEM((seq + 2 * HALO, gw), BF16),
            pltpu.VMEM((seq + 2 * HALO, SSD_STATE), BF16),
            pltpu.VMEM((seq + 2 * HALO, SSD_STATE), BF16),
            pltpu.VMEM((seq, gw), F32),
            pltpu.VMEM((SSD_STATE, seq), F32),
            pltpu.VMEM((seq, SSD_STATE), BF16),
            pltpu.VMEM((seq, gw), F32),
            pltpu.VMEM((SSD_STATE, gw), F32),
            pltpu.VMEM((2, SSD_CHUNK, gw), BF16),
            pltpu.VMEM((2, SSD_CHUNK, LANE), BF16),
            pltpu.VMEM((seq, LANE), F32),
            pltpu.VMEM((2, nc, SUBLANE, gw), F32),
            pltpu.VMEM((2, nc, SSD_STATE, gw), F32),
            pltpu.VMEM((2, nc, SSD_STATE, gw), BF16),
            pltpu.VMEM((SSD_CHUNK, LANE), BF16),
        ],
        compiler_params=_cparams("parallel", "parallel"),
        name="ssd",
    )(*args)


def _rope_tables(seq):
    rows = seq // GRID_W
    row = jnp.repeat(jnp.arange(rows), GRID_W).astype(F32)
    col = jnp.tile(jnp.arange(GRID_W), rows).astype(F32)
    quarter = DA_HALF // 4
    inv = ROPE_THETA ** (-jnp.arange(quarter, dtype=F32) / quarter)
    ang_r = row[:, None] * inv
    ang_c = col[:, None] * inv
    ang = jnp.concatenate([ang_r, ang_r, ang_c, ang_c], axis=-1)
    sign = jnp.tile(jnp.concatenate([-jnp.ones((quarter,), F32), jnp.ones((quarter,), F32)]), 2)
    cos = jnp.cos(ang)
    sin = jnp.sin(ang) * sign
    return jnp.tile(cos, (1, 2)), jnp.tile(sin, (1, 2))


def kernel(x_prompt, x_sample, c, cache_attn_k, cache_attn_v, state_lru, state_ssd, c_ctx, w_ada, b_ada, norm_g, lru_conv_w, lru_conv_b, lru_w_r, lru_b_r, lru_w_i, lru_b_i, lru_lambda, even_w_in, even_w_out, da_q_norm, da_k_norm, da_lambda, da_subln, ssd_w_in, ssd_conv_w, ssd_conv_b, ssd_dt_bias, ssd_a_log, ssd_d, ssd_norm_w, ssd_w_out, ffn_w_in, ffn_w_out):
    depth = w_ada.shape[0]
    batch, seq_p, d = x_prompt.shape
    dec_batch, seq_s, _ = x_sample.shape
    past = cache_attn_k.shape[2]
    heads = cache_attn_k.shape[3]
    d_rnn = lru_conv_w.shape[2]
    d_ff = ffn_w_out.shape[1]
    d_inner = ssd_w_out.shape[1]
    n_even = even_w_in.shape[0]

    cond = jnp.zeros((N_COND, d), F32).at[0].set(c_ctx).at[1:1 + dec_batch].set(c)
    mod_all = _ada(cond, w_ada, b_ada).reshape(depth, N_COND, 6, d)
    rope = _rope_tables(seq_s)
    cache_k = cache_attn_k.reshape(dec_batch, n_even, past, heads * DA_VDIM)
    cache_v = cache_attn_v.reshape(dec_batch, n_even, past, heads * DA_VDIM)

    even_in_b, even_out_b = even_w_in.astype(BF16), even_w_out.astype(BF16)
    n_odd = ssd_w_in.shape[0]
    ssd_heads = d_inner // SSD_HEADDIM
    n_main = ssd_w_in.shape[2] - 2 * ssd_heads
    ssd_in_b, ssd_out_b = ssd_w_in.astype(BF16), ssd_w_out.astype(BF16)
    hpg = ssd_heads // SSD_GROUPS
    ssd_dt_b = ssd_w_in[:, :, n_main:].astype(BF16).reshape(n_odd, d, 2, SSD_GROUPS, hpg).transpose(
        0, 1, 3, 2, 4).reshape(n_odd, d, 2 * ssd_heads)
    ffn_in_b, ffn_out_b = ffn_w_in.astype(BF16), ffn_w_out.astype(BF16)

    xs = [x_prompt.reshape(batch * seq_p, d), x_sample.reshape(dec_batch * seq_s, d)]
    shapes = [(batch, seq_p), (dec_batch, seq_s)]
    new_kv, new_lru, new_ssd = None, [], None

    for i in range(depth):
        j = i // 2
        mods = [mod_all[i, 0:1], mod_all[i, 1:1 + dec_batch]]
        g_mix = norm_g[i, 0].reshape(1, d)
        g_ffn = norm_g[i, 1].reshape(1, d)
        if i % 2 == 0:
            lam_init = 0.8 - 0.6 * math.exp(-0.3 * i)
            w_r = lru_w_r[j].astype(BF16)
            w_i = lru_w_i[j].astype(BF16)
            for s in range(2):
                nseq, seq = shapes[s]
                u = _inproj(xs[s], g_mix, mods[s], even_in_b, j, mode="plain", n_out=even_in_b.shape[2],
                            tn=2560, shift_idx=0, scale_idx=1, name="even_in")
                h0 = jnp.zeros((nseq, 2, d_rnn), F32) if s == 0 else state_lru[:, j]
                rec, s_fin = _lru(u, nseq, seq, lru_conv_w[j], lru_conv_b[j], w_r, w_i,
                                  lru_b_r[j], lru_b_i[j], lru_lambda[j], h0)
                if s == 0:
                    att, kc, vc = _attn(u, nseq, seq, heads, lam_init, da_q_norm[j], da_k_norm[j],
                                        da_lambda[j], da_subln[j], kv_layers=n_even, kv_layer=j,
                                        kv_prev=new_kv)
                    new_kv = (kc, vc)
                    new_lru.append(s_fin)
                else:
                    att = _attn(u, nseq, seq, heads, lam_init, da_q_norm[j], da_k_norm[j],
                                da_lambda[j], da_subln[j], ctx=(cache_k, cache_v, j), rope=rope)
                xs[s] = _outproj([rec, att], even_out_b, j, xs[s], mods[s], gate_idx=2, tn=1024,
                                 name="even_out")
        else:
            for s in range(2):
                nseq, seq = shapes[s]
                u, dt = _inproj(xs[s], g_mix, mods[s], ssd_in_b, j, mode="dt", n_out=n_main, tn=2048,
                                shift_idx=0, scale_idx=1, name="odd_in", w_dt=ssd_dt_b)
                ssd_args = (u, dt, nseq, seq, ssd_conv_w[j], ssd_conv_b[j], ssd_dt_bias[j],
                            ssd_a_log[j], ssd_d[j], ssd_norm_w[j])
                if s == 0:
                    y, new_ssd = _ssd(*ssd_args, st_layers=n_odd, st_layer=j, st_prev=new_ssd)
                else:
                    y, = _ssd(*ssd_args, h0=state_ssd, h0_layer=j)
                xs[s] = _outproj([y], ssd_out_b, j, xs[s], mods[s], gate_idx=2, tn=1024, name="odd_out")
        for s in range(2):
            act = _inproj(xs[s], g_ffn, mods[s], ffn_in_b, i, mode="glu", n_out=d_ff, tn=512,
                          shift_idx=3, scale_idx=4, name="ffn_in")
            xs[s] = _outproj([act], ffn_out_b, i, xs[s], mods[s], gate_idx=5, tn=512, name="ffn_out")

    return (xs[0].reshape(batch, seq_p, d), xs[1].reshape(dec_batch, seq_s, d),
            new_kv[0].reshape(batch, n_even, seq_p, heads, DA_VDIM),
            new_kv[1].reshape(batch, n_even, seq_p, heads, DA_VDIM),
            jnp.stack(new_lru, axis=1), new_ssd)
```

```python
import functools
import math

import jax
import jax.numpy as jnp
from jax import lax
from jax.experimental import pallas as pl
from jax.experimental.pallas import tpu as pltpu

F32 = jnp.float32
BF16 = jnp.bfloat16

EPS = 1e-6
GRID_W = 64
CONV_W = 4
CONV_LEFT = CONV_W // 2
LRU_BW = 128
LRU_C = 8.0
DA_HALF = 64
DA_VDIM = 2 * DA_HALF
ROPE_THETA = 10000.0
SSD_HEADDIM = 64
SSD_GROUPS = 8
SSD_HPG = 8
SSD_STATE = 128
SSD_CHUNK = 128
N_COND = 16

LANE = 128
SUBLANE = 8
VMEM_LIMIT = 56 * 1024 * 1024

ROW_TILE = 1024
MOD_ROWS = 128
CONV_ROWS = 256
CONV_BLK = 128
HALO = CONV_BLK // 2
LOG2E = 1.4426950408889634
SQRT_FLOOR = 1e-37


def _sigmoid(x):
    return 0.5 * jnp.tanh(0.5 * x) + 0.5


def _silu(x):
    h = 0.5 * x
    return h + h * jnp.tanh(h)


def _cparams(*sem):
    return pltpu.CompilerParams(dimension_semantics=sem, vmem_limit_bytes=VMEM_LIMIT)


def _ada_kernel(c_ref, w_ref, b_ref, o_ref):
    s = _silu(c_ref[...]).astype(BF16)
    o_ref[...] = jnp.dot(s, w_ref[...].astype(BF16), preferred_element_type=F32) + b_ref[...]


def _ada(cond, w_ada, b_ada):
    depth, d, n = w_ada.shape
    tn = 1024
    return pl.pallas_call(
        _ada_kernel,
        grid=(depth, n // tn),
        in_specs=[
            pl.BlockSpec((N_COND, d), lambda l, j: (0, 0)),
            pl.BlockSpec((None, d, tn), lambda l, j: (l, 0, j)),
            pl.BlockSpec((None, 1, tn), lambda l, j: (l, 0, j)),
        ],
        out_specs=pl.BlockSpec((None, N_COND, tn), lambda l, j: (l, 0, j)),
        out_shape=jax.ShapeDtypeStruct((depth, N_COND, n), F32),
        compiler_params=_cparams("parallel", "parallel"),
        name="ada",
    )(cond, w_ada, b_ada.reshape(depth, 1, n))


def _modulate_into(x_ref, g_ref, mod_ref, h_ref, shift_idx, scale_idx):
    gs = g_ref[...] * (1.0 + mod_ref[scale_idx:scale_idx + 1, :])
    shift = mod_ref[shift_idx:shift_idx + 1, :]

    def body(r, carry):
        rows = pl.ds(pl.multiple_of(r * MOD_ROWS, MOD_ROWS), MOD_ROWS)
        x = x_ref[rows, :]
        ms = jnp.mean(x * x, axis=-1, keepdims=True)
        h_ref[rows, :] = (x * lax.rsqrt(ms + EPS) * gs + shift).astype(BF16)
        return carry

    lax.fori_loop(0, x_ref.shape[0] // MOD_ROWS, body, 0)


def _inproj_kernel(*refs, mode, shift_idx, scale_idx):
    if mode == "plain":
        x_ref, g_ref, mod_ref, w_ref, o_ref, h_ref = refs
    elif mode == "glu":
        x_ref, g_ref, mod_ref, wg_ref, wu_ref, o_ref, h_ref = refs
    else:
        x_ref, g_ref, mod_ref, w_ref, wdt_ref, o_ref, dt_ref, h_ref = refs

    @pl.when(pl.program_id(1) == 0)
    def _():
        _modulate_into(x_ref, g_ref, mod_ref, h_ref, shift_idx, scale_idx)
        if mode == "dt":
            dt_ref[...] = jnp.dot(h_ref[...], wdt_ref[...], preferred_element_type=F32)

    h = h_ref[...]
    if mode == "glu":
        g = jnp.dot(h, wg_ref[...], preferred_element_type=F32)
        u = jnp.dot(h, wu_ref[...], preferred_element_type=F32)
        o_ref[...] = (_silu(g) * u).astype(o_ref.dtype)
    else:
        o_ref[...] = jnp.dot(h, w_ref[...], preferred_element_type=F32).astype(o_ref.dtype)


def _inproj(x, g, mod, w, layer, *, mode, n_out, tn, shift_idx, scale_idx, name, w_dt=None):
    t, d = x.shape
    tm = ROW_TILE
    rows_per_cond = t // mod.shape[0]
    nj = n_out // tn
    x_spec = pl.BlockSpec((tm, d), lambda i, j: (i, 0))
    g_spec = pl.BlockSpec((1, d), lambda i, j: (0, 0))
    mod_spec = pl.BlockSpec((None, 6, d), lambda i, j: ((i * tm) // rows_per_cond, 0, 0))
    w_spec = pl.BlockSpec((None, d, tn), lambda i, j: (layer, 0, j))
    o_spec = pl.BlockSpec((tm, tn), lambda i, j: (i, j))
    o_shape = jax.ShapeDtypeStruct((t, n_out), BF16)
    if mode == "plain":
        in_specs = [x_spec, g_spec, mod_spec, w_spec]
        args = (x, g, mod, w)
        out_specs, out_shape = o_spec, o_shape
    elif mode == "glu":
        in_specs = [x_spec, g_spec, mod_spec, w_spec,
                    pl.BlockSpec((None, d, tn), lambda i, j: (layer, 0, nj + j))]
        args = (x, g, mod, w, w)
        out_specs, out_shape = o_spec, o_shape
    else:
        in_specs = [x_spec, g_spec, mod_spec, w_spec,
                    pl.BlockSpec((None, d, LANE), lambda i, j: (layer, 0, 0))]
        args = (x, g, mod, w, w_dt)
        out_specs = (o_spec, pl.BlockSpec((tm, LANE), lambda i, j: (i, 0)))
        out_shape = (o_shape, jax.ShapeDtypeStruct((t, LANE), F32))
    return pl.pallas_call(
        functools.partial(_inproj_kernel, mode=mode, shift_idx=shift_idx, scale_idx=scale_idx),
        grid=(t // tm, nj),
        in_specs=in_specs,
        out_specs=out_specs,
        out_shape=out_shape,
        scratch_shapes=[pltpu.VMEM((tm, d), BF16)],
        compiler_params=_cparams("parallel", "arbitrary"),
        name=name,
    )(*args)


def _outproj_kernel(*refs, n_a, gate_idx):
    a_refs, w_refs = refs[:n_a], refs[n_a:2 * n_a]
    x_ref, mod_ref, o_ref = refs[2 * n_a:]
    acc = jnp.dot(a_refs[0][...], w_refs[0][...], preferred_element_type=F32)
    for a_ref, w_ref in zip(a_refs[1:], w_refs[1:]):
        acc = acc + jnp.dot(a_ref[...], w_ref[...], preferred_element_type=F32)
    o_ref[...] = x_ref[...] + mod_ref[gate_idx:gate_idx + 1, :] * acc


def _outproj(a_list, w, layer, x, mod, *, gate_idx, tn, name):
    t, d = x.shape
    tm = ROW_TILE
    rows_per_cond = t // mod.shape[0]
    n_a = len(a_list)
    k = a_list[0].shape[1]
    in_specs = [pl.BlockSpec((tm, k), lambda i, j: (i, 0)) for _ in a_list]
    in_specs += [pl.BlockSpec((None, k, tn), functools.partial(lambda i, j, q: (layer, q, j), q=q))
                 for q in range(n_a)]
    in_specs += [pl.BlockSpec((tm, tn), lambda i, j: (i, j)),
                 pl.BlockSpec((None, 6, tn), lambda i, j: ((i * tm) // rows_per_cond, 0, j))]
    return pl.pallas_call(
        functools.partial(_outproj_kernel, n_a=n_a, gate_idx=gate_idx),
        grid=(t // tm, d // tn),
        in_specs=in_specs,
        out_specs=pl.BlockSpec((tm, tn), lambda i, j: (i, j)),
        out_shape=jax.ShapeDtypeStruct((t, d), F32),
        compiler_params=_cparams("parallel", "parallel"),
        name=name,
    )(*a_list, *([w] * n_a), x, mod)


def _conv_blocks(src_ref, w_ref, b_ref, pad_ref, seq, width, emit):
    assert CONV_LEFT == 2 and CONV_W == 4
    zeros = jnp.zeros((HALO, width), BF16)
    pad_ref[0:HALO, 0:width] = zeros
    pad_ref[HALO + seq:2 * HALO + seq, 0:width] = zeros
    pad_ref[HALO:HALO + seq, 0:width] = src_ref[...]
    taps = [k for k in range(CONV_W) if k != CONV_LEFT]
    r_i = lax.broadcasted_iota(jnp.int32, (len(taps) * CONV_BLK, CONV_BLK), 0)
    c_i = lax.broadcasted_iota(jnp.int32, (len(taps) * CONV_BLK, CONV_BLK), 1)
    tap_i = r_i // CONV_BLK
    off = jnp.where(tap_i >= CONV_LEFT, tap_i + 1, tap_i) - CONV_LEFT
    shift = (c_i == (r_i % CONV_BLK) + off).astype(BF16)
    sub = lax.broadcasted_iota(jnp.int32, (SUBLANE, width), 0)
    edge = 2 * SUBLANE
    w = w_ref[...]
    for r0 in range(0, seq, CONV_BLK):
        lo = HALO + r0
        cur = pad_ref[lo:lo + CONV_BLK, 0:width]
        sh = jnp.dot(shift, cur, preferred_element_type=F32)
        acc = b_ref[...] + w[CONV_LEFT:CONV_LEFT + 1, :] * cur.astype(F32)
        for i, k in enumerate(taps):
            acc = acc + w[k:k + 1, :] * sh[i * CONV_BLK:(i + 1) * CONV_BLK]
        before = pad_ref[lo - edge:lo, 0:width].astype(F32)
        after = pad_ref[lo + CONV_BLK:lo + CONV_BLK + edge, 0:width].astype(F32)
        xm2, xm1, xp0 = before[edge - 2:edge - 1], before[edge - 1:edge], after[0:1]
        top = jnp.where(sub == 0, w[0:1, :] * xm2 + w[1:2, :] * xm1, jnp.where(sub == 1, w[0:1, :] * xm1, 0.0))
        bot = jnp.where(sub == SUBLANE - 1, w[3:4, :] * xp0, 0.0)
        acc = jnp.concatenate([acc[:SUBLANE] + top, acc[SUBLANE:CONV_BLK - SUBLANE],
                               acc[CONV_BLK - SUBLANE:] + bot], axis=0)
        emit(r0, acc)


def _lru_kernel(gate_ref, xr_ref, cw_ref, cb_ref, wr_ref, wi_ref, br_ref, bi_ref, lam_ref, h0_ref,
                rec_ref, sfin_ref, pad_s, xc_s, a_s, b_s, h_s, *, seq, width):
    def put_xc(r0, v):
        xc_s[r0:r0 + CONV_BLK, :] = v

    _conv_blocks(xr_ref, cw_ref, cb_ref, pad_s, seq, width, put_xc)
    step = min(seq, CONV_ROWS)
    tiles = step // SUBLANE

    def gates(d):
        sp = jax.nn.softplus(-lam_ref[d:d + 1, :])
        for r0 in range(0, seq, step):
            t0 = r0 // SUBLANE
            for k in range(width // LRU_BW):
                ks = slice(k * LRU_BW, (k + 1) * LRU_BW)
                xk = xc_s[r0:r0 + step, ks]
                xb = xk.astype(BF16)
                r = _sigmoid(jnp.dot(xb, wr_ref[d, k], preferred_element_type=F32) + br_ref[d:d + 1, ks])
                gi = _sigmoid(jnp.dot(xb, wi_ref[d, k], preferred_element_type=F32) + bi_ref[d:d + 1, ks])
                a = jnp.exp(-LRU_C * r * sp[:, ks])
                y = 1.0 - a * a
                b = y * lax.rsqrt(jnp.maximum(y, SQRT_FLOOR)) * (gi * xk)
                a_s[d, t0:t0 + tiles, :, ks] = a.reshape(tiles, SUBLANE, LRU_BW)
                b_s[d, t0:t0 + tiles, :, ks] = b.reshape(tiles, SUBLANE, LRU_BW)

    gates(0)
    gates(1)
    ntile = seq // SUBLANE

    def scan_tile(i, carry):
        hf, hb = carry
        jb = ntile - 1 - i
        for r in range(SUBLANE):
            rb = SUBLANE - 1 - r
            hf = a_s[0, i, r:r + 1, :] * hf + b_s[0, i, r:r + 1, :]
            h_s[0, i, r:r + 1, :] = hf
            hb = a_s[1, jb, rb:rb + 1, :] * hb + b_s[1, jb, rb:rb + 1, :]
            h_s[1, jb, rb:rb + 1, :] = hb
        return hf, hb

    hf, hb = lax.fori_loop(0, ntile, scan_tile, (h0_ref[0:1, :], h0_ref[1:2, :]))
    sfin_ref[0:1, :] = hf
    sfin_ref[1:2, :] = hb

    for r0 in range(0, seq, step):
        t0 = r0 // SUBLANE
        h = (h_s[0, t0:t0 + tiles] + h_s[1, t0:t0 + tiles]).reshape(step, width)
        rec_ref[r0:r0 + step, :] = (h * jax.nn.gelu(gate_ref[r0:r0 + step, :].astype(F32))).astype(BF16)


def _lru(u, nseq, seq, conv_w, conv_b, w_r, w_i, b_r, b_i, lam, h0):
    d_rnn = conv_w.shape[1]
    width = 512
    nb = d_rnn // width
    kb = width // LRU_BW
    vec = lambda rows: pl.BlockSpec((rows, width), lambda b, c: (0, c))
    return pl.pallas_call(
        functools.partial(_lru_kernel, seq=seq, width=width),
        grid=(nseq, nb),
        in_specs=[
            pl.BlockSpec((seq, width), lambda b, c: (b, c)),
            pl.BlockSpec((seq, width), lambda b, c: (b, nb + c)),
            vec(CONV_W), vec(1),
            pl.BlockSpec((2, kb, LRU_BW, LRU_BW), lambda b, c: (0, c, 0, 0)),
            pl.BlockSpec((2, kb, LRU_BW, LRU_BW), lambda b, c: (0, c, 0, 0)),
            vec(2), vec(2), vec(2),
            pl.BlockSpec((None, 2, width), lambda b, c: (b, 0, c)),
        ],
        out_specs=(pl.BlockSpec((seq, width), lambda b, c: (b, c)),
                   pl.BlockSpec((None, 2, width), lambda b, c: (b, 0, c))),
        out_shape=(jax.ShapeDtypeStruct((nseq * seq, d_rnn), BF16),
                   jax.ShapeDtypeStruct((nseq, 2, d_rnn), F32)),
        scratch_shapes=[pltpu.VMEM((seq + 2 * HALO, width), BF16), pltpu.VMEM((seq, width), F32)]
        + [pltpu.VMEM((2, seq // SUBLANE, SUBLANE, width), F32)] * 3,
        compiler_params=_cparams("parallel", "parallel"),
        name="lru",
    )(u, u, conv_w, conv_b.reshape(1, d_rnn), w_r, w_i, b_r, b_i, lam, h0)


def _half_rms(x, gain):
    x2 = x * x
    s0 = jnp.sum(x2[:, :DA_HALF], axis=-1, keepdims=True)
    s1 = jnp.sum(x2[:, DA_HALF:], axis=-1, keepdims=True)
    lane = lax.broadcasted_iota(jnp.int32, x.shape, 1)
    ms = jnp.where(lane < DA_HALF, s0, s1) * (1.0 / DA_HALF)
    return x * lax.rsqrt(ms + EPS) * gain


def _rope(x, cos, sin_signed):
    q = DA_HALF // 4
    lane = lax.broadcasted_iota(jnp.int32, x.shape, 1)
    rot = jnp.where((lane % (2 * q)) < q, pltpu.roll(x, LANE - q, 1), pltpu.roll(x, q, 1))
    return x * cos + rot * sin_signed


def _attn_kernel(*refs, latent, seq, past, lam_init, sub, has_prev):
    if latent:
        (q_ref, k_ref, v_ref, kc_ref, vc_ref, cosq_ref, sinq_ref, cosk_ref, sink_ref,
         qn_ref, kn_ref, dl_ref, sub_ref, o_ref, kt_s, vx_s) = refs
    else:
        if has_prev:
            refs = refs[:7] + refs[9:]
        (q_ref, k_ref, v_ref, qn_ref, kn_ref, dl_ref, sub_ref, o_ref, ko_ref, vo_ref, kt_s, vx_s) = refs
    hd = DA_VDIM
    dl = dl_ref[...]
    lam = (jnp.exp(jnp.sum(dl[0:1] * dl[1:2], axis=-1, keepdims=True))
           - jnp.exp(jnp.sum(dl[2:3] * dl[3:4], axis=-1, keepdims=True)) + lam_init)

    for hh in range(kt_s.shape[0]):
        hs = slice(hh * hd, (hh + 1) * hd)
        vx_s[hh, :, hd:2 * hd] = jnp.ones((seq + past, hd), BF16)
        step = min(seq, CONV_ROWS)
        for r0 in range(0, seq, step):
            rows = slice(r0, r0 + step)
            kn = _half_rms(k_ref[rows, hs].astype(F32), kn_ref[...])
            if latent:
                kn = _rope(kn, cosk_ref[rows, :], sink_ref[rows, :])
            else:
                ko_ref[rows, hs] = kn
                vo_ref[rows, hs] = v_ref[rows, hs].astype(F32)
            kt_s[hh, :, rows] = kn.T.astype(BF16)
            vx_s[hh, rows, 0:hd] = v_ref[rows, hs]
        if latent:
            kt_s[hh, :, seq:seq + past] = kc_ref[...].T.astype(BF16)
            vx_s[hh, seq:seq + past, 0:hd] = vc_ref[...].astype(BF16)

        blocks = range(0, seq, sub)
        scores = []
        for r0 in blocks:
            rows = slice(r0, r0 + sub)
            qn = _half_rms(q_ref[rows, hs].astype(F32), qn_ref[...])
            if latent:
                qn = _rope(qn, cosq_ref[rows, :], sinq_ref[rows, :])
            qb = (qn * (DA_HALF ** -0.5 * LOG2E)).astype(BF16)
            scores.append([jnp.dot(qb[:, m * DA_HALF:(m + 1) * DA_HALF],
                                   kt_s[hh, m * DA_HALF:(m + 1) * DA_HALF, :],
                                   preferred_element_type=F32) for m in range(2)])
        for r0, (s0, s1) in zip(blocks, scores):
            rows = slice(r0, r0 + sub)
            outs = []
            for s in (s0, s1):
                p = jnp.exp2(s - jnp.max(s, axis=-1, keepdims=True)).astype(BF16)
                outs.append(jnp.dot(p, vx_s[hh], preferred_element_type=F32))
            o = outs[0][:, :hd] / outs[0][:, hd:] - lam * (outs[1][:, :hd] / outs[1][:, hd:])
            o = o * lax.rsqrt(jnp.mean(o * o, axis=-1, keepdims=True) + EPS) * sub_ref[...] * (1.0 - lam_init)
            o_ref[rows, hs] = o.astype(BF16)


def _attn(u, nseq, seq, heads, lam_init, q_norm, k_norm, da_lam, subln, ctx=None, rope=None,
          kv_layers=0, kv_layer=0, kv_prev=None):
    latent = ctx is not None
    hd = DA_VDIM
    sub = 256
    hps = 1 if latent else heads
    wid = hps * hd
    nh = heads // hps
    col0 = (u.shape[1] - 3 * heads * hd) // wid
    past = ctx[0].shape[2] if latent else 0
    q_spec = pl.BlockSpec((seq, wid), lambda b, h: (b, col0 + h))
    k_spec = pl.BlockSpec((seq, wid), lambda b, h: (b, col0 + nh + h))
    v_spec = pl.BlockSpec((seq, wid), lambda b, h: (b, col0 + 2 * nh + h))
    small = lambda r, c: pl.BlockSpec((r, c), lambda b, h: (0, 0))
    par_specs = [small(1, hd), small(1, hd), small(4, DA_HALF), small(1, hd)]
    pars = (q_norm.reshape(1, hd), k_norm.reshape(1, hd), da_lam, subln.reshape(1, hd))
    o_spec = pl.BlockSpec((seq, wid), lambda b, h: (b, h))
    o_shape = jax.ShapeDtypeStruct((nseq * seq, heads * hd), BF16)
    aliases = {}
    if latent:
        kc, vc, layer = ctx
        cos, sin = rope
        c_spec = pl.BlockSpec((None, None, past, hd), lambda b, h: (b, layer, 0, h))
        t_spec = pl.BlockSpec((seq, hd), lambda b, h: (0, 0))
        in_specs = [q_spec, k_spec, v_spec, c_spec, c_spec, t_spec, t_spec, t_spec, t_spec] + par_specs
        args = (u, u, u, kc, vc, cos, sin, cos, sin) + pars
        out_specs, out_shape = o_spec, o_shape
    else:
        in_specs = [q_spec, k_spec, v_spec] + par_specs
        args = (u, u, u) + pars
        if kv_prev is not None:
            aliases = {len(args): 1, len(args) + 1: 2}
            in_specs += [pl.BlockSpec(memory_space=pl.ANY)] * 2
            args += tuple(kv_prev)
        kv_spec = pl.BlockSpec((None, None, seq, wid), lambda b, h: (b, kv_layer, 0, h))
        kv_shape = jax.ShapeDtypeStruct((nseq, kv_layers, seq, heads * hd), F32)
        out_specs, out_shape = (o_spec, kv_spec, kv_spec), (o_shape, kv_shape, kv_shape)
    return pl.pallas_call(
        functools.partial(_attn_kernel, latent=latent, seq=seq, past=past, lam_init=lam_init, sub=sub,
                          has_prev=kv_prev is not None),
        grid=(nseq, nh),
        in_specs=in_specs,
        out_specs=out_specs,
        out_shape=out_shape,
        input_output_aliases=aliases,
        scratch_shapes=[pltpu.VMEM((hps, hd, seq + past), BF16), pltpu.VMEM((hps, seq + past, 2 * hd), BF16)],
        compiler_params=_cparams("parallel", "parallel"),
        name="attn",
    )(*args)


def _split3(x):
    hi = x.astype(BF16)
    r1 = x - hi.astype(F32)
    mid = r1.astype(BF16)
    lo = (r1 - mid.astype(F32)).astype(BF16)
    return hi, mid, lo


def _dot(a, b):
    return jnp.dot(a, b, preferred_element_type=F32)


def _ssd_kernel(*refs, seq, has_h0, has_prev, has_st):
    (z_ref, x_ref, b_ref, c_ref, dt_ref, dbc_ref, alc_ref, alr_ref,
     cwx_ref, cwb_ref, cwc_ref, cbx_ref, cbb_ref, cbc_ref, dsk_ref, nw_ref) = refs[:16]
    rest = refs[16:]
    if has_h0:
        h0_ref, rest = rest[0], rest[1:]
    if has_prev:
        rest = rest[1:]
    y_ref, rest = rest[0], rest[1:]
    if has_st:
        st_ref, rest = rest[0], rest[1:]
    (pad_s, padb_s, padc_s, xs_s, bt_s, cs_s, ys_s, st_s, exp_s, half_s,
     csr_s, dtot_s, new_s, ent_s, sel_s) = rest

    q = SSD_CHUNK
    nslot = 2 * SSD_HPG
    pair_w = 2 * SSD_HEADDIM
    gw = xs_s.shape[1]

    def put_x(r0, v):
        xs_s[r0:r0 + CONV_BLK, :] = _silu(v)

    def put_b(r0, v):
        bt_s[:, r0:r0 + CONV_BLK] = _silu(v).T

    def put_c(r0, v):
        cs_s[r0:r0 + CONV_BLK, :] = _silu(v).astype(BF16)

    _conv_blocks(x_ref, cwx_ref, cbx_ref, pad_s, seq, gw, put_x)
    _conv_blocks(b_ref, cwb_ref, cbb_ref, padb_s, seq, SSD_STATE, put_b)
    _conv_blocks(c_ref, cwc_ref, cbc_ref, padc_s, seq, SSD_STATE, put_c)

    li = lax.broadcasted_iota(jnp.int32, (q, q), 0)
    si = lax.broadcasted_iota(jnp.int32, (q, q), 1)
    lower = (li >= si)
    upper = (li <= si)
    lower_b = lower.astype(BF16)
    upper_b = upper.astype(BF16)
    a_rep = -jnp.exp(alc_ref[...])
    a_row = -jnp.exp(alr_ref[...])
    for i in range(2):
        half_s[i] = (si // SSD_HEADDIM == i).astype(BF16)
    sel_s[...] = (li == pl.program_id(1) * nslot + si % nslot).astype(BF16)
    lower2 = jnp.concatenate([lower, lower], axis=1)
    diag2 = jnp.concatenate([li == si, li == si], axis=1)
    ki = lax.broadcasted_iota(jnp.int32, (q, gw), 0)
    ji = lax.broadcasted_iota(jnp.int32, (q, gw), 1)
    for d in range(2):
        exp_s[d] = ((ki % nslot == d * SSD_HPG + ji // SSD_HEADDIM) & (ki < 3 * nslot)).astype(BF16)

    def terms3(v):
        lane = lax.broadcasted_iota(jnp.int32, v.shape, 1)
        hi = v.astype(BF16).astype(F32)
        mid = (v - hi).astype(BF16).astype(F32)
        lo = v - hi - mid
        return jnp.where(lane < nslot, hi, jnp.where(lane < 2 * nslot, mid, jnp.where(
            lane < 3 * nslot, lo, 0.0))).astype(BF16)

    fwd_lane = (si % nslot) < SSD_HPG
    fwd_row = lax.broadcasted_iota(jnp.int32, (nslot, q), 0) < SSD_HPG

    def cums(c):
        rows = pl.ds(pl.multiple_of(c * q, q), q)
        raw_rep = sum(_dot(part, sel_s[...]) for part in _split3(dt_ref[rows, :]))
        dt_rep = jax.nn.softplus(raw_rep + dbc_ref[...])
        da_rep = dt_rep * a_rep
        dt_row = dt_rep.T[0:nslot]
        da_row = dt_row * a_row
        csf_rep = sum(_dot(lower_b, part) for part in _split3(da_rep))
        csf_row = sum(_dot(part, upper_b) for part in _split3(da_row))
        bt = bt_s[:, rows]
        cb = _dot(cs_s[rows, :], bt.astype(BF16))
        return rows, dt_row, da_rep, da_row, csf_rep, csf_row, bt, cb

    def segments(c, stage1):
        rows, dt_row, da_rep, da_row, csf_rep, csf_row, bt, cb = stage1
        tot_rep = csf_rep[q - 1:q, :]
        tot_row = csf_row[:, q - 1:q]
        cs_rep = jnp.where(fwd_lane, csf_rep, tot_rep - csf_rep + da_rep)
        cs_row = jnp.where(fwd_row, csf_row, tot_row - csf_row + da_row)
        csr_s[rows, :] = cs_rep
        et = terms3(jnp.broadcast_to(jnp.exp(tot_rep), (SUBLANE, q)))
        for d in range(2):
            dtot_s[d, c] = _dot(et, exp_s[d])
        w_row = dt_row * jnp.exp(tot_row - cs_row)
        neg_row = cs_row - jnp.log(dt_row)
        seg = lambda s: cs_rep[:, s:s + 1] - neg_row[s:s + 1, :]
        segs = []
        for pr in range(SSD_HPG // 2):
            f0, f1 = 2 * pr, 2 * pr + 1
            b0, b1 = SSD_HPG + f0, SSD_HPG + f1
            segs.append((jnp.concatenate([seg(f0), seg(f1)], axis=1),
                         jnp.concatenate([seg(b0), seg(b1)], axis=1)))
        return rows, dt_row, w_row, bt, cb, segs

    def products(c, stage2):
        rows, dt_row, w_row, bt, cb, segs = stage2
        cb2 = jnp.concatenate([cb, cb], axis=1)
        bt2 = jnp.concatenate([bt, bt], axis=1)
        xb = xs_s[rows, :].astype(BF16)
        row2 = lambda v, s0, s1: jnp.concatenate([v[s0:s0 + 1, :], v[s1:s1 + 1, :]], axis=1)
        for pr in range(SSD_HPG // 2):
            ps = slice(pr * pair_w, (pr + 1) * pair_w)
            f0, f1 = 2 * pr, 2 * pr + 1
            b0, b1 = SSD_HPG + f0, SSD_HPG + f1
            seg_f, seg_b = segs[pr]
            dec = jnp.exp(jnp.where(lower2, seg_f, seg_b)) + jnp.where(diag2, row2(dt_row, b0, b1), 0.0)
            m = (cb2 * dec).astype(BF16)
            btw_f = (bt2 * row2(w_row, f0, f1)).astype(BF16)
            btw_b = (bt2 * row2(w_row, b0, b1)).astype(BF16)
            xp = xb[:, ps]
            xh = jnp.concatenate([xp * half_s[0], xp * half_s[1]], axis=0)
            r = _dot(jnp.concatenate([m, btw_f, btw_b], axis=0), xh)
            ys_s[rows, ps] = r[:q]
            new_s[0, c, :, ps] = r[q:2 * q]
            new_s[1, c, :, ps] = r[2 * q:]

    nchunks = seq // q
    group = 4 if nchunks % 4 == 0 else 2

    def local_pass(i, carry):
        chunks = [group * i + n for n in range(group)]
        stage1 = [cums(c) for c in chunks]
        stage2 = [segments(c, s) for c, s in zip(chunks, stage1)]
        for c, s in zip(chunks, stage2):
            products(c, s)
        return carry

    lax.fori_loop(0, nchunks // group, local_pass, 0)

    for d in range(2):
        for pr in range(SSD_HPG // 2):
            ps = slice(pr * pair_w, (pr + 1) * pair_w)
            if has_h0:
                st_s[:, ps] = h0_ref[d, 2 * pr:2 * pr + 2].reshape(pair_w, SSD_STATE).T
            else:
                st_s[:, ps] = jnp.zeros((SSD_STATE, pair_w), F32)

        def carry_state(i, carry, d=d):
            c = i if d == 0 else nchunks - 1 - i
            st = st_s[...]
            ent_s[d, c] = st.astype(BF16)
            st_s[...] = st * dtot_s[d, c][0:1, :] + new_s[d, c]
            return carry

        lax.fori_loop(0, nchunks, carry_state, 0, unroll=2)
        for pr in range(SSD_HPG // 2 if has_st else 0):
            ps = slice(pr * pair_w, (pr + 1) * pair_w)
            st_ref[d, 2 * pr:2 * pr + 2] = st_s[:, ps].T.reshape(2, SSD_HEADDIM, SSD_STATE)

    def finish(i, carry):
        chunks = [group * i + n for n in range(group)]
        rows = [pl.ds(pl.multiple_of(c * q, q), q) for c in chunks]
        offs = [[_dot(cs_s[r, :], ent_s[d, c]) for d in range(2)] for c, r in zip(chunks, rows)]
        e1s = [terms3(jnp.exp(csr_s[r, :])) for r in rows]
        spreads = [[_dot(e1, exp_s[d]) for d in range(2)] for e1 in e1s]
        for r, off, spread in zip(rows, offs, spreads):
            y = ys_s[r, :] + dsk_ref[...] * xs_s[r, :] + off[0] * spread[0] + off[1] * spread[1]
            y = y * _silu(z_ref[r, :].astype(F32))
            y = y * lax.rsqrt(jnp.mean(y * y, axis=-1, keepdims=True) + EPS) * nw_ref[...]
            y_ref[r, :] = y.astype(BF16)
        return carry

    lax.fori_loop(0, nchunks // group, finish, 0)


def _ssd(u, dt, nseq, seq, conv_w, conv_b, dt_bias, a_log, d_skip, norm_w, *,
         h0=None, h0_layer=0, st_layers=0, st_layer=0, st_prev=None):
    t = nseq * seq
    g = SSD_GROUPS
    e = SSD_HPG
    heads = g * e
    gw = e * SSD_HEADDIM
    d_inner = heads * SSD_HEADDIM
    nc = seq // SSD_CHUNK
    xb0 = d_inner // gw
    bb0 = 2 * d_inner // SSD_STATE
    per_group = lambda v: v.reshape(2, g, e).transpose(1, 0, 2).reshape(g, 2 * e)
    rep = LANE // (2 * e)
    dbg = per_group(dt_bias)
    alg = per_group(a_log)
    dbg_rep = jnp.tile(dbg, (1, rep)).reshape(g, 1, LANE)
    alg_rep = jnp.tile(alg, (1, rep)).reshape(g, 1, LANE)
    dsk = jnp.repeat(d_skip, SSD_HEADDIM).reshape(1, d_inner)
    cw = lambda width, blk0: pl.BlockSpec((CONV_W, width), lambda b, gi: (0, blk0 + gi))
    cbias = lambda width, blk0: pl.BlockSpec((1, width), lambda b, gi: (0, blk0 + gi))
    st_block = (None, None, 2, e, SSD_HEADDIM, SSD_STATE)
    in_specs = [
        pl.BlockSpec((seq, gw), lambda b, gi: (b, gi)),
        pl.BlockSpec((seq, gw), lambda b, gi: (b, xb0 + gi)),
        pl.BlockSpec((seq, SSD_STATE), lambda b, gi: (b, bb0 + gi)),
        pl.BlockSpec((seq, SSD_STATE), lambda b, gi: (b, bb0 + g + gi)),
        pl.BlockSpec((seq, LANE), lambda b, gi: (b, 0)),
        pl.BlockSpec((None, 1, LANE), lambda b, gi: (gi, 0, 0)),
        pl.BlockSpec((None, 1, LANE), lambda b, gi: (gi, 0, 0)),
        pl.BlockSpec((None, 2 * e, 1), lambda b, gi: (gi, 0, 0)),
        cw(gw, 0), cw(SSD_STATE, d_inner // SSD_STATE), cw(SSD_STATE, d_inner // SSD_STATE + g),
        cbias(gw, 0), cbias(SSD_STATE, d_inner // SSD_STATE), cbias(SSD_STATE, d_inner // SSD_STATE + g),
        cbias(gw, 0), cbias(gw, 0),
    ]
    conv_b2 = conv_b.reshape(1, -1)
    args = [u, u, u, u, dt, dbg_rep, alg_rep, alg.reshape(g, 2 * e, 1),
            conv_w, conv_w, conv_w, conv_b2, conv_b2, conv_b2, dsk, norm_w.reshape(1, d_inner)]
    if h0 is not None:
        in_specs.append(pl.BlockSpec(st_block, lambda b, gi: (b, h0_layer, 0, gi, 0, 0)))
        args.append(h0)
    out_specs = [pl.BlockSpec((seq, gw), lambda b, gi: (b, gi))]
    out_shape = [jax.ShapeDtypeStruct((t, d_inner), BF16)]
    aliases = {}
    if st_layers:
        if st_prev is not None:
            aliases = {len(args): 1}
            in_specs.append(pl.BlockSpec(memory_space=pl.ANY))
            args.append(st_prev)
        out_specs.append(pl.BlockSpec(st_block, lambda b, gi: (b, st_layer, 0, gi, 0, 0)))
        out_shape.append(jax.ShapeDtypeStruct((nseq, st_layers, 2, heads, SSD_HEADDIM, SSD_STATE), F32))
    return pl.pallas_call(
        functools.partial(_ssd_kernel, seq=seq, has_h0=h0 is not None, has_prev=st_prev is not None,
                          has_st=bool(st_layers)),
        grid=(nseq, g),
        in_specs=in_specs,
        out_specs=out_specs,
        out_shape=out_shape,
        input_output_aliases=aliases,
        scratch_shapes=[
            pltpu.VMEM((seq + 2 * HALO, gw), BF16),
            pltpu.VMEM((seq + 2 * HALO, SSD_STATE), BF16),
            pltpu.VMEM((seq + 2 * HALO, SSD_STATE), BF16),
            pltpu.VMEM((seq, gw), F32),
            pltpu.VMEM((SSD_STATE, seq), F32),
            pltpu.VMEM((seq, SSD_STATE), BF16),
            pltpu.VMEM((seq, gw), F32),
            pltpu.VMEM((SSD_STATE, gw), F32),
            pltpu.VMEM((2, SSD_CHUNK, gw), BF16),
            pltpu.VMEM((2, SSD_CHUNK, LANE), BF16),
            pltpu.VMEM((seq, LANE), F32),
            pltpu.VMEM((2, nc, SUBLANE, gw), F32),
            pltpu.VMEM((2, nc, SSD_STATE, gw), F32),
            pltpu.VMEM((2, nc, SSD_STATE, gw), BF16),
            pltpu.VMEM((SSD_CHUNK, LANE), BF16),
        ],
        compiler_params=_cparams("parallel", "parallel"),
        name="ssd",
    )(*args)


def _rope_tables(seq):
    rows = seq // GRID_W
    row = jnp.repeat(jnp.arange(rows), GRID_W).astype(F32)
    col = jnp.tile(jnp.arange(GRID_W), rows).astype(F32)
    quarter = DA_HALF // 4
    inv = ROPE_THETA ** (-jnp.arange(quarter, dtype=F32) / quarter)
    ang_r = row[:, None] * inv
    ang_c = col[:, None] * inv
    ang = jnp.concatenate([ang_r, ang_r, ang_c, ang_c], axis=-1)
    sign = jnp.tile(jnp.concatenate([-jnp.ones((quarter,), F32), jnp.ones((quarter,), F32)]), 2)
    cos = jnp.cos(ang)
    sin = jnp.sin(ang) * sign
    return jnp.tile(cos, (1, 2)), jnp.tile(sin, (1, 2))


def kernel(x_prompt, x_sample, c, cache_attn_k, cache_attn_v, state_lru, state_ssd, c_ctx, w_ada, b_ada, norm_g, lru_conv_w, lru_conv_b, lru_w_r, lru_b_r, lru_w_i, lru_b_i, lru_lambda, even_w_in, even_w_out, da_q_norm, da_k_norm, da_lambda, da_subln, ssd_w_in, ssd_conv_w, ssd_conv_b, ssd_dt_bias, ssd_a_log, ssd_d, ssd_norm_w, ssd_w_out, ffn_w_in, ffn_w_out):
    depth = w_ada.shape[0]
    batch, seq_p, d = x_prompt.shape
    dec_batch, seq_s, _ = x_sample.shape
    past = cache_attn_k.shape[2]
    heads = cache_attn_k.shape[3]
    d_rnn = lru_conv_w.shape[2]
    d_ff = ffn_w_out.shape[1]
    d_inner = ssd_w_out.shape[1]
    n_even = even_w_in.shape[0]

    cond = jnp.zeros((N_COND, d), F32).at[0].set(c_ctx).at[1:1 + dec_batch].set(c)
    mod_all = _ada(cond, w_ada, b_ada).reshape(depth, N_COND, 6, d)
    rope = _rope_tables(seq_s)
    cache_k = cache_attn_k.reshape(dec_batch, n_even, past, heads * DA_VDIM)
    cache_v = cache_attn_v.reshape(dec_batch, n_even, past, heads * DA_VDIM)

    even_in_b, even_out_b = even_w_in.astype(BF16), even_w_out.astype(BF16)
    n_odd = ssd_w_in.shape[0]
    ssd_heads = d_inner // SSD_HEADDIM
    n_main = ssd_w_in.shape[2] - 2 * ssd_heads
    ssd_in_b, ssd_out_b = ssd_w_in.astype(BF16), ssd_w_out.astype(BF16)
    hpg = ssd_heads // SSD_GROUPS
    ssd_dt_b = ssd_w_in[:, :, n_main:].astype(BF16).reshape(n_odd, d, 2, SSD_GROUPS, hpg).transpose(
        0, 1, 3, 2, 4).reshape(n_odd, d, 2 * ssd_heads)
    ffn_in_b, ffn_out_b = ffn_w_in.astype(BF16), ffn_w_out.astype(BF16)

    xs = [x_prompt.reshape(batch * seq_p, d), x_sample.reshape(dec_batch * seq_s, d)]
    shapes = [(batch, seq_p), (dec_batch, seq_s)]
    new_kv, new_lru, new_ssd = None, [], None

    for i in range(depth):
        j = i // 2
        mods = [mod_all[i, 0:1], mod_all[i, 1:1 + dec_batch]]
        g_mix = norm_g[i, 0].reshape(1, d)
        g_ffn = norm_g[i, 1].reshape(1, d)
        if i % 2 == 0:
            lam_init = 0.8 - 0.6 * math.exp(-0.3 * i)
            w_r = lru_w_r[j].astype(BF16)
            w_i = lru_w_i[j].astype(BF16)
            for s in range(2):
                nseq, seq = shapes[s]
                u = _inproj(xs[s], g_mix, mods[s], even_in_b, j, mode="plain", n_out=even_in_b.shape[2],
                            tn=2560, shift_idx=0, scale_idx=1, name="even_in")
                h0 = jnp.zeros((nseq, 2, d_rnn), F32) if s == 0 else state_lru[:, j]
                rec, s_fin = _lru(u, nseq, seq, lru_conv_w[j], lru_conv_b[j], w_r, w_i,
                                  lru_b_r[j], lru_b_i[j], lru_lambda[j], h0)
                if s == 0:
                    att, kc, vc = _attn(u, nseq, seq, heads, lam_init, da_q_norm[j], da_k_norm[j],
                                        da_lambda[j], da_subln[j], kv_layers=n_even, kv_layer=j,
                                        kv_prev=new_kv)
                    new_kv = (kc, vc)
                    new_lru.append(s_fin)
                else:
                    att = _attn(u, nseq, seq, heads, lam_init, da_q_norm[j], da_k_norm[j],
                                da_lambda[j], da_subln[j], ctx=(cache_k, cache_v, j), rope=rope)
                xs[s] = _outproj([rec, att], even_out_b, j, xs[s], mods[s], gate_idx=2, tn=1024,
                                 name="even_out")
        else:
            for s in range(2):
                nseq, seq = shapes[s]
                u, dt = _inproj(xs[s], g_mix, mods[s], ssd_in_b, j, mode="dt", n_out=n_main, tn=2048,
                                shift_idx=0, scale_idx=1, name="odd_in", w_dt=ssd_dt_b)
                ssd_args = (u, dt, nseq, seq, ssd_conv_w[j], ssd_conv_b[j], ssd_dt_bias[j],
                            ssd_a_log[j], ssd_d[j], ssd_norm_w[j])
                if s == 0:
                    y, new_ssd = _ssd(*ssd_args, st_layers=n_odd, st_layer=j, st_prev=new_ssd)
                else:
                    y, = _ssd(*ssd_args, h0=state_ssd, h0_layer=j)
                xs[s] = _outproj([y], ssd_out_b, j, xs[s], mods[s], gate_idx=2, tn=1024, name="odd_out")
        for s in range(2):
            act = _inproj(xs[s], g_ffn, mods[s], ffn_in_b, i, mode="glu", n_out=d_ff, tn=512,
                          shift_idx=3, scale_idx=4, name="ffn_in")
            xs[s] = _outproj([act], ffn_out_b, i, xs[s], mods[s], gate_idx=5, tn=512, name="ffn_out")

    return (xs[0].reshape(batch, seq_p, d), xs[1].reshape(dec_batch, seq_s, d),
            new_kv[0].reshape(batch, n_even, seq_p, heads, DA_VDIM),
            new_kv[1].reshape(batch, n_even, seq_p, heads, DA_VDIM),
            jnp.stack(new_lru, axis=1), new_ssd)
```

```python
import functools
import math

import jax
import jax.numpy as jnp
from jax import lax
from jax.experimental import pallas as pl
from jax.experimental.pallas import tpu as pltpu

F32 = jnp.float32
BF16 = jnp.bfloat16

EPS = 1e-6
GRID_W = 64
CONV_W = 4
CONV_LEFT = CONV_W // 2
LRU_BW = 128
LRU_C = 8.0
DA_HALF = 64
DA_VDIM = 2 * DA_HALF
ROPE_THETA = 10000.0
SSD_HEADDIM = 64
SSD_GROUPS = 8
SSD_HPG = 8
SSD_STATE = 128
SSD_CHUNK = 128
N_COND = 16

LANE = 128
SUBLANE = 8
VMEM_LIMIT = 56 * 1024 * 1024

ROW_TILE = 1024
MOD_ROWS = 128
CONV_ROWS = 256
CONV_BLK = 128
HALO = CONV_BLK // 2
LOG2E = 1.4426950408889634
SQRT_FLOOR = 1e-37


def _sigmoid(x):
    return 0.5 * jnp.tanh(0.5 * x) + 0.5


def _silu(x):
    h = 0.5 * x
    return h + h * jnp.tanh(h)


def _cparams(*sem):
    return pltpu.CompilerParams(dimension_semantics=sem, vmem_limit_bytes=VMEM_LIMIT)


def _ada_kernel(c_ref, w_ref, b_ref, o_ref):
    s = _silu(c_ref[...]).astype(BF16)
    o_ref[...] = jnp.dot(s, w_ref[...].astype(BF16), preferred_element_type=F32) + b_ref[...]


def _ada(cond, w_ada, b_ada):
    depth, d, n = w_ada.shape
    tn = 1024
    return pl.pallas_call(
        _ada_kernel,
        grid=(depth, n // tn),
        in_specs=[
            pl.BlockSpec((N_COND, d), lambda l, j: (0, 0)),
            pl.BlockSpec((None, d, tn), lambda l, j: (l, 0, j)),
            pl.BlockSpec((None, 1, tn), lambda l, j: (l, 0, j)),
        ],
        out_specs=pl.BlockSpec((None, N_COND, tn), lambda l, j: (l, 0, j)),
        out_shape=jax.ShapeDtypeStruct((depth, N_COND, n), F32),
        compiler_params=_cparams("parallel", "parallel"),
        name="ada",
    )(cond, w_ada, b_ada.reshape(depth, 1, n))


def _modulate_into(x_ref, g_ref, mod_ref, h_ref, shift_idx, scale_idx):
    gs = g_ref[...] * (1.0 + mod_ref[scale_idx:scale_idx + 1, :])
    shift = mod_ref[shift_idx:shift_idx + 1, :]

    def body(r, carry):
        rows = pl.ds(pl.multiple_of(r * MOD_ROWS, MOD_ROWS), MOD_ROWS)
        x = x_ref[rows, :]
        ms = jnp.mean(x * x, axis=-1, keepdims=True)
        h_ref[rows, :] = (x * lax.rsqrt(ms + EPS) * gs + shift).astype(BF16)
        return carry

    lax.fori_loop(0, x_ref.shape[0] // MOD_ROWS, body, 0)


def _inproj_kernel(*refs, mode, shift_idx, scale_idx):
    if mode == "plain":
        x_ref, g_ref, mod_ref, w_ref, o_ref, h_ref = refs
    elif mode == "glu":
        x_ref, g_ref, mod_ref, wg_ref, wu_ref, o_ref, h_ref = refs
    else:
        x_ref, g_ref, mod_ref, w_ref, wdt_ref, o_ref, dt_ref, h_ref = refs

    @pl.when(pl.program_id(1) == 0)
    def _():
        _modulate_into(x_ref, g_ref, mod_ref, h_ref, shift_idx, scale_idx)
        if mode == "dt":
            dt_ref[...] = jnp.dot(h_ref[...], wdt_ref[...], preferred_element_type=F32)

    h = h_ref[...]
    if mode == "glu":
        g = jnp.dot(h, wg_ref[...], preferred_element_type=F32)
        u = jnp.dot(h, wu_ref[...], preferred_element_type=F32)
        o_ref[...] = (_silu(g) * u).astype(o_ref.dtype)
    else:
        o_ref[...] = jnp.dot(h, w_ref[...], preferred_element_type=F32).astype(o_ref.dtype)


def _inproj(x, g, mod, w, layer, *, mode, n_out, tn, shift_idx, scale_idx, name, w_dt=None):
    t, d = x.shape
    tm = ROW_TILE
    rows_per_cond = t // mod.shape[0]
    nj = n_out // tn
    x_spec = pl.BlockSpec((tm, d), lambda i, j: (i, 0))
    g_spec = pl.BlockSpec((1, d), lambda i, j: (0, 0))
    mod_spec = pl.BlockSpec((None, 6, d), lambda i, j: ((i * tm) // rows_per_cond, 0, 0))
    w_spec = pl.BlockSpec((None, d, tn), lambda i, j: (layer, 0, j))
    o_spec = pl.BlockSpec((tm, tn), lambda i, j: (i, j))
    o_shape = jax.ShapeDtypeStruct((t, n_out), BF16)
    if mode == "plain":
        in_specs = [x_spec, g_spec, mod_spec, w_spec]
        args = (x, g, mod, w)
        out_specs, out_shape = o_spec, o_shape
    elif mode == "glu":
        in_specs = [x_spec, g_spec, mod_spec,
                    pl.BlockSpec((None, None, d, tn), lambda i, j: (layer, j, 0, 0)),
                    pl.BlockSpec((None, None, d, tn), lambda i, j: (layer, nj + j, 0, 0))]
        args = (x, g, mod, w, w)
        out_specs, out_shape = o_spec, o_shape
    else:
        in_specs = [x_spec, g_spec, mod_spec, w_spec,
                    pl.BlockSpec((None, d, LANE), lambda i, j: (layer, 0, 0))]
        args = (x, g, mod, w, w_dt)
        out_specs = (o_spec, pl.BlockSpec((tm, LANE), lambda i, j: (i, 0)))
        out_shape = (o_shape, jax.ShapeDtypeStruct((t, LANE), F32))
    return pl.pallas_call(
        functools.partial(_inproj_kernel, mode=mode, shift_idx=shift_idx, scale_idx=scale_idx),
        grid=(t // tm, nj),
        in_specs=in_specs,
        out_specs=out_specs,
        out_shape=out_shape,
        scratch_shapes=[pltpu.VMEM((tm, d), BF16)],
        compiler_params=_cparams("parallel", "arbitrary"),
        name=name,
    )(*args)


def _outproj_kernel(*refs, n_a, gate_idx):
    a_refs, w_refs = refs[:n_a], refs[n_a:2 * n_a]
    x_ref, mod_ref, o_ref = refs[2 * n_a:]
    acc = jnp.dot(a_refs[0][...], w_refs[0][...], preferred_element_type=F32)
    for a_ref, w_ref in zip(a_refs[1:], w_refs[1:]):
        acc = acc + jnp.dot(a_ref[...], w_ref[...], preferred_element_type=F32)
    o_ref[...] = x_ref[...] + mod_ref[gate_idx:gate_idx + 1, :] * acc


def _outproj(a_list, w, layer, x, mod, *, gate_idx, tn, name, tile_major=False):
    t, d = x.shape
    tm = ROW_TILE
    rows_per_cond = t // mod.shape[0]
    n_a = len(a_list)
    k = a_list[0].shape[1]
    in_specs = [pl.BlockSpec((tm, k), lambda i, j: (i, 0)) for _ in a_list]
    if tile_major:
        in_specs += [pl.BlockSpec((None, None, k, tn), lambda i, j: (layer, j, 0, 0))]
    else:
        in_specs += [pl.BlockSpec((None, k, tn), functools.partial(lambda i, j, q: (layer, q, j), q=q))
                     for q in range(n_a)]
    in_specs += [pl.BlockSpec((tm, tn), lambda i, j: (i, j)),
                 pl.BlockSpec((None, 6, tn), lambda i, j: ((i * tm) // rows_per_cond, 0, j))]
    return pl.pallas_call(
        functools.partial(_outproj_kernel, n_a=n_a, gate_idx=gate_idx),
        grid=(t // tm, d // tn),
        in_specs=in_specs,
        out_specs=pl.BlockSpec((tm, tn), lambda i, j: (i, j)),
        out_shape=jax.ShapeDtypeStruct((t, d), F32),
        compiler_params=_cparams("parallel", "parallel"),
        name=name,
    )(*a_list, *([w] * n_a), x, mod)


def _conv_blocks(src_ref, w_ref, b_ref, pad_ref, seq, width, emit):
    assert CONV_LEFT == 2 and CONV_W == 4
    zeros = jnp.zeros((HALO, width), BF16)
    pad_ref[0:HALO, 0:width] = zeros
    pad_ref[HALO + seq:2 * HALO + seq, 0:width] = zeros
    pad_ref[HALO:HALO + seq, 0:width] = src_ref[...]
    taps = [k for k in range(CONV_W) if k != CONV_LEFT]
    r_i = lax.broadcasted_iota(jnp.int32, (len(taps) * CONV_BLK, CONV_BLK), 0)
    c_i = lax.broadcasted_iota(jnp.int32, (len(taps) * CONV_BLK, CONV_BLK), 1)
    tap_i = r_i // CONV_BLK
    off = jnp.where(tap_i >= CONV_LEFT, tap_i + 1, tap_i) - CONV_LEFT
    shift = (c_i == (r_i % CONV_BLK) + off).astype(BF16)
    sub = lax.broadcasted_iota(jnp.int32, (SUBLANE, width), 0)
    edge = 2 * SUBLANE
    w = w_ref[...]
    for r0 in range(0, seq, CONV_BLK):
        lo = HALO + r0
        cur = pad_ref[lo:lo + CONV_BLK, 0:width]
        sh = jnp.dot(shift, cur, preferred_element_type=F32)
        acc = b_ref[...] + w[CONV_LEFT:CONV_LEFT + 1, :] * cur.astype(F32)
        for i, k in enumerate(taps):
            acc = acc + w[k:k + 1, :] * sh[i * CONV_BLK:(i + 1) * CONV_BLK]
        before = pad_ref[lo - edge:lo, 0:width].astype(F32)
        after = pad_ref[lo + CONV_BLK:lo + CONV_BLK + edge, 0:width].astype(F32)
        xm2, xm1, xp0 = before[edge - 2:edge - 1], before[edge - 1:edge], after[0:1]
        top = jnp.where(sub == 0, w[0:1, :] * xm2 + w[1:2, :] * xm1, jnp.where(sub == 1, w[0:1, :] * xm1, 0.0))
        bot = jnp.where(sub == SUBLANE - 1, w[3:4, :] * xp0, 0.0)
        acc = jnp.concatenate([acc[:SUBLANE] + top, acc[SUBLANE:CONV_BLK - SUBLANE],
                               acc[CONV_BLK - SUBLANE:] + bot], axis=0)
        emit(r0, acc)


def _lru_kernel(gate_ref, xr_ref, cw_ref, cb_ref, wr_ref, wi_ref, br_ref, bi_ref, lam_ref, h0_ref,
                rec_ref, sfin_ref, pad_s, xc_s, a_s, b_s, h_s, *, seq, width):
    def put_xc(r0, v):
        xc_s[r0:r0 + CONV_BLK, :] = v

    _conv_blocks(xr_ref, cw_ref, cb_ref, pad_s, seq, width, put_xc)
    step = min(seq, CONV_ROWS)
    tiles = step // SUBLANE

    def gates(d):
        sp = jax.nn.softplus(-lam_ref[d:d + 1, :])
        for r0 in range(0, seq, step):
            t0 = r0 // SUBLANE
            for k in range(width // LRU_BW):
                ks = slice(k * LRU_BW, (k + 1) * LRU_BW)
                xk = xc_s[r0:r0 + step, ks]
                xb = xk.astype(BF16)
                r = _sigmoid(jnp.dot(xb, wr_ref[d, k], preferred_element_type=F32) + br_ref[d:d + 1, ks])
                gi = _sigmoid(jnp.dot(xb, wi_ref[d, k], preferred_element_type=F32) + bi_ref[d:d + 1, ks])
                a = jnp.exp(-LRU_C * r * sp[:, ks])
                y = 1.0 - a * a
                b = y * lax.rsqrt(jnp.maximum(y, SQRT_FLOOR)) * (gi * xk)
                a_s[d, t0:t0 + tiles, :, ks] = a.reshape(tiles, SUBLANE, LRU_BW)
                b_s[d, t0:t0 + tiles, :, ks] = b.reshape(tiles, SUBLANE, LRU_BW)

    gates(0)
    gates(1)
    ntile = seq // SUBLANE

    def scan_tile(i, carry):
        hf, hb = carry
        jb = ntile - 1 - i
        for r in range(SUBLANE):
            rb = SUBLANE - 1 - r
            hf = a_s[0, i, r:r + 1, :] * hf + b_s[0, i, r:r + 1, :]
            h_s[0, i, r:r + 1, :] = hf
            hb = a_s[1, jb, rb:rb + 1, :] * hb + b_s[1, jb, rb:rb + 1, :]
            h_s[1, jb, rb:rb + 1, :] = hb
        return hf, hb

    hf, hb = lax.fori_loop(0, ntile, scan_tile, (h0_ref[0:1, :], h0_ref[1:2, :]))
    sfin_ref[0:1, :] = hf
    sfin_ref[1:2, :] = hb

    for r0 in range(0, seq, step):
        t0 = r0 // SUBLANE
        h = (h_s[0, t0:t0 + tiles] + h_s[1, t0:t0 + tiles]).reshape(step, width)
        rec_ref[r0:r0 + step, :] = (h * jax.nn.gelu(gate_ref[r0:r0 + step, :].astype(F32))).astype(BF16)


def _lru(u, nseq, seq, conv_w, conv_b, w_r, w_i, b_r, b_i, lam, h0):
    d_rnn = conv_w.shape[1]
    width = 512
    nb = d_rnn // width
    kb = width // LRU_BW
    vec = lambda rows: pl.BlockSpec((rows, width), lambda b, c: (0, c))
    return pl.pallas_call(
        functools.partial(_lru_kernel, seq=seq, width=width),
        grid=(nseq, nb),
        in_specs=[
            pl.BlockSpec((seq, width), lambda b, c: (b, c)),
            pl.BlockSpec((seq, width), lambda b, c: (b, nb + c)),
            vec(CONV_W), vec(1),
            pl.BlockSpec((2, kb, LRU_BW, LRU_BW), lambda b, c: (0, c, 0, 0)),
            pl.BlockSpec((2, kb, LRU_BW, LRU_BW), lambda b, c: (0, c, 0, 0)),
            vec(2), vec(2), vec(2),
            pl.BlockSpec((None, 2, width), lambda b, c: (b, 0, c)),
        ],
        out_specs=(pl.BlockSpec((seq, width), lambda b, c: (b, c)),
                   pl.BlockSpec((None, 2, width), lambda b, c: (b, 0, c))),
        out_shape=(jax.ShapeDtypeStruct((nseq * seq, d_rnn), BF16),
                   jax.ShapeDtypeStruct((nseq, 2, d_rnn), F32)),
        scratch_shapes=[pltpu.VMEM((seq + 2 * HALO, width), BF16), pltpu.VMEM((seq, width), F32)]
        + [pltpu.VMEM((2, seq // SUBLANE, SUBLANE, width), F32)] * 3,
        compiler_params=_cparams("parallel", "parallel"),
        name="lru",
    )(u, u, conv_w, conv_b.reshape(1, d_rnn), w_r, w_i, b_r, b_i, lam, h0)


def _half_rms(x, gain):
    x2 = x * x
    s0 = jnp.sum(x2[:, :DA_HALF], axis=-1, keepdims=True)
    s1 = jnp.sum(x2[:, DA_HALF:], axis=-1, keepdims=True)
    lane = lax.broadcasted_iota(jnp.int32, x.shape, 1)
    ms = jnp.where(lane < DA_HALF, s0, s1) * (1.0 / DA_HALF)
    return x * lax.rsqrt(ms + EPS) * gain


def _rope(x, cos, sin_signed):
    q = DA_HALF // 4
    lane = lax.broadcasted_iota(jnp.int32, x.shape, 1)
    rot = jnp.where((lane % (2 * q)) < q, pltpu.roll(x, LANE - q, 1), pltpu.roll(x, q, 1))
    return x * cos + rot * sin_signed


def _attn_kernel(*refs, latent, seq, past, lam_init, sub, has_prev):
    if latent:
        (q_ref, k_ref, v_ref, kc_ref, vc_ref, cosq_ref, sinq_ref, cosk_ref, sink_ref,
         qn_ref, kn_ref, dl_ref, sub_ref, o_ref, kt_s, vx_s) = refs
    else:
        if has_prev:
            refs = refs[:7] + refs[9:]
        (q_ref, k_ref, v_ref, qn_ref, kn_ref, dl_ref, sub_ref, o_ref, ko_ref, vo_ref, kt_s, vx_s) = refs
    hd = DA_VDIM
    dl = dl_ref[...]
    lam = (jnp.exp(jnp.sum(dl[0:1] * dl[1:2], axis=-1, keepdims=True))
           - jnp.exp(jnp.sum(dl[2:3] * dl[3:4], axis=-1, keepdims=True)) + lam_init)

    for hh in range(kt_s.shape[0]):
        hs = slice(hh * hd, (hh + 1) * hd)
        vx_s[hh, :, hd:2 * hd] = jnp.ones((seq + past, hd), BF16)
        step = min(seq, CONV_ROWS)
        for r0 in range(0, seq, step):
            rows = slice(r0, r0 + step)
            kn = _half_rms(k_ref[rows, hs].astype(F32), kn_ref[...])
            if latent:
                kn = _rope(kn, cosk_ref[rows, :], sink_ref[rows, :])
            else:
                ko_ref[rows, hs] = kn
                vo_ref[rows, hs] = v_ref[rows, hs].astype(F32)
            kt_s[hh, :, rows] = kn.T.astype(BF16)
            vx_s[hh, rows, 0:hd] = v_ref[rows, hs]
        if latent:
            kt_s[hh, :, seq:seq + past] = kc_ref[...].T.astype(BF16)
            vx_s[hh, seq:seq + past, 0:hd] = vc_ref[...].astype(BF16)

        blocks = range(0, seq, sub)
        scores = []
        for r0 in blocks:
            rows = slice(r0, r0 + sub)
            qn = _half_rms(q_ref[rows, hs].astype(F32), qn_ref[...])
            if latent:
                qn = _rope(qn, cosq_ref[rows, :], sinq_ref[rows, :])
            qb = (qn * (DA_HALF ** -0.5 * LOG2E)).astype(BF16)
            scores.append([jnp.dot(qb[:, m * DA_HALF:(m + 1) * DA_HALF],
                                   kt_s[hh, m * DA_HALF:(m + 1) * DA_HALF, :],
                                   preferred_element_type=F32) for m in range(2)])
        for r0, (s0, s1) in zip(blocks, scores):
            rows = slice(r0, r0 + sub)
            outs = []
            for s in (s0, s1):
                p = jnp.exp2(s - jnp.max(s, axis=-1, keepdims=True)).astype(BF16)
                outs.append(jnp.dot(p, vx_s[hh], preferred_element_type=F32))
            o = outs[0][:, :hd] / outs[0][:, hd:] - lam * (outs[1][:, :hd] / outs[1][:, hd:])
            o = o * lax.rsqrt(jnp.mean(o * o, axis=-1, keepdims=True) + EPS) * sub_ref[...] * (1.0 - lam_init)
            o_ref[rows, hs] = o.astype(BF16)


def _attn(u, nseq, seq, heads, lam_init, q_norm, k_norm, da_lam, subln, ctx=None, rope=None,
          kv_layers=0, kv_layer=0, kv_prev=None):
    latent = ctx is not None
    hd = DA_VDIM
    sub = 256
    hps = 1 if latent else heads
    wid = hps * hd
    nh = heads // hps
    col0 = (u.shape[1] - 3 * heads * hd) // wid
    past = ctx[0].shape[2] if latent else 0
    q_spec = pl.BlockSpec((seq, wid), lambda b, h: (b, col0 + h))
    k_spec = pl.BlockSpec((seq, wid), lambda b, h: (b, col0 + nh + h))
    v_spec = pl.BlockSpec((seq, wid), lambda b, h: (b, col0 + 2 * nh + h))
    small = lambda r, c: pl.BlockSpec((r, c), lambda b, h: (0, 0))
    par_specs = [small(1, hd), small(1, hd), small(4, DA_HALF), small(1, hd)]
    pars = (q_norm.reshape(1, hd), k_norm.reshape(1, hd), da_lam, subln.reshape(1, hd))
    o_spec = pl.BlockSpec((seq, wid), lambda b, h: (b, h))
    o_shape = jax.ShapeDtypeStruct((nseq * seq, heads * hd), BF16)
    aliases = {}
    if latent:
        kc, vc, layer = ctx
        cos, sin = rope
        c_spec = pl.BlockSpec((None, None, past, hd), lambda b, h: (b, layer, 0, h))
        t_spec = pl.BlockSpec((seq, hd), lambda b, h: (0, 0))
        in_specs = [q_spec, k_spec, v_spec, c_spec, c_spec, t_spec, t_spec, t_spec, t_spec] + par_specs
        args = (u, u, u, kc, vc, cos, sin, cos, sin) + pars
        out_specs, out_shape = o_spec, o_shape
    else:
        in_specs = [q_spec, k_spec, v_spec] + par_specs
        args = (u, u, u) + pars
        if kv_prev is not None:
            aliases = {len(args): 1, len(args) + 1: 2}
            in_specs += [pl.BlockSpec(memory_space=pl.ANY)] * 2
            args += tuple(kv_prev)
        kv_spec = pl.BlockSpec((None, None, seq, wid), lambda b, h: (b, kv_layer, 0, h))
        kv_shape = jax.ShapeDtypeStruct((nseq, kv_layers, seq, heads * hd), F32)
        out_specs, out_shape = (o_spec, kv_spec, kv_spec), (o_shape, kv_shape, kv_shape)
    return pl.pallas_call(
        functools.partial(_attn_kernel, latent=latent, seq=seq, past=past, lam_init=lam_init, sub=sub,
                          has_prev=kv_prev is not None),
        grid=(nseq, nh),
        in_specs=in_specs,
        out_specs=out_specs,
        out_shape=out_shape,
        input_output_aliases=aliases,
        scratch_shapes=[pltpu.VMEM((hps, hd, seq + past), BF16), pltpu.VMEM((hps, seq + past, 2 * hd), BF16)],
        compiler_params=_cparams("parallel", "parallel"),
        name="attn",
    )(*args)


def _split3(x):
    hi = x.astype(BF16)
    r1 = x - hi.astype(F32)
    mid = r1.astype(BF16)
    lo = (r1 - mid.astype(F32)).astype(BF16)
    return hi, mid, lo


def _dot(a, b):
    return jnp.dot(a, b, preferred_element_type=F32)


def _ssd_kernel(*refs, seq, has_h0, has_prev, has_st):
    (z_ref, x_ref, b_ref, c_ref, dt_ref, dbc_ref, alc_ref, alr_ref,
     cwx_ref, cwb_ref, cwc_ref, cbx_ref, cbb_ref, cbc_ref, dsk_ref, nw_ref) = refs[:16]
    rest = refs[16:]
    if has_h0:
        h0_ref, rest = rest[0], rest[1:]
    if has_prev:
        rest = rest[1:]
    y_ref, rest = rest[0], rest[1:]
    if has_st:
        st_ref, rest = rest[0], rest[1:]
    (pad_s, padb_s, padc_s, xs_s, bt_s, cs_s, ys_s, st_s, exp_s, half_s,
     csr_s, dtot_s, new_s, ent_s, sel_s) = rest

    q = SSD_CHUNK
    nslot = 2 * SSD_HPG
    pair_w = 2 * SSD_HEADDIM
    gw = xs_s.shape[1]

    def put_x(r0, v):
        xs_s[r0:r0 + CONV_BLK, :] = _silu(v)

    def put_b(r0, v):
        bt_s[:, r0:r0 + CONV_BLK] = _silu(v).T

    def put_c(r0, v):
        cs_s[r0:r0 + CONV_BLK, :] = _silu(v).astype(BF16)

    _conv_blocks(x_ref, cwx_ref, cbx_ref, pad_s, seq, gw, put_x)
    _conv_blocks(b_ref, cwb_ref, cbb_ref, padb_s, seq, SSD_STATE, put_b)
    _conv_blocks(c_ref, cwc_ref, cbc_ref, padc_s, seq, SSD_STATE, put_c)

    li = lax.broadcasted_iota(jnp.int32, (q, q), 0)
    si = lax.broadcasted_iota(jnp.int32, (q, q), 1)
    lower = (li >= si)
    upper = (li <= si)
    lower_b = lower.astype(BF16)
    upper_b = upper.astype(BF16)
    a_rep = -jnp.exp(alc_ref[...])
    a_row = -jnp.exp(alr_ref[...])
    for i in range(2):
        half_s[i] = (si // SSD_HEADDIM == i).astype(BF16)
    sel_s[...] = (li == pl.program_id(1) * nslot + si % nslot).astype(BF16)
    lower2 = jnp.concatenate([lower, lower], axis=1)
    diag2 = jnp.concatenate([li == si, li == si], axis=1)
    ki = lax.broadcasted_iota(jnp.int32, (q, gw), 0)
    ji = lax.broadcasted_iota(jnp.int32, (q, gw), 1)
    for d in range(2):
        exp_s[d] = ((ki % nslot == d * SSD_HPG + ji // SSD_HEADDIM) & (ki < 3 * nslot)).astype(BF16)

    def terms3(v):
        lane = lax.broadcasted_iota(jnp.int32, v.shape, 1)
        hi = v.astype(BF16).astype(F32)
        mid = (v - hi).astype(BF16).astype(F32)
        lo = v - hi - mid
        return jnp.where(lane < nslot, hi, jnp.where(lane < 2 * nslot, mid, jnp.where(
            lane < 3 * nslot, lo, 0.0))).astype(BF16)

    fwd_lane = (si % nslot) < SSD_HPG
    fwd_row = lax.broadcasted_iota(jnp.int32, (nslot, q), 0) < SSD_HPG

    def cums(c):
        rows = pl.ds(pl.multiple_of(c * q, q), q)
        raw_rep = sum(_dot(part, sel_s[...]) for part in _split3(dt_ref[rows, :]))
        dt_rep = jax.nn.softplus(raw_rep + dbc_ref[...])
        da_rep = dt_rep * a_rep
        dt_row = dt_rep.T[0:nslot]
        da_row = dt_row * a_row
        csf_rep = sum(_dot(lower_b, part) for part in _split3(da_rep))
        csf_row = sum(_dot(part, upper_b) for part in _split3(da_row))
        bt = bt_s[:, rows]
        cb = _dot(cs_s[rows, :], bt.astype(BF16))
        return rows, dt_row, da_rep, da_row, csf_rep, csf_row, bt, cb

    def segments(c, stage1):
        rows, dt_row, da_rep, da_row, csf_rep, csf_row, bt, cb = stage1
        tot_rep = csf_rep[q - 1:q, :]
        tot_row = csf_row[:, q - 1:q]
        cs_rep = jnp.where(fwd_lane, csf_rep, tot_rep - csf_rep + da_rep)
        cs_row = jnp.where(fwd_row, csf_row, tot_row - csf_row + da_row)
        csr_s[rows, :] = cs_rep
        et = terms3(jnp.broadcast_to(jnp.exp(tot_rep), (SUBLANE, q)))
        for d in range(2):
            dtot_s[d, c] = _dot(et, exp_s[d])
        w_row = dt_row * jnp.exp(tot_row - cs_row)
        neg_row = cs_row - jnp.log(dt_row)
        seg = lambda s: cs_rep[:, s:s + 1] - neg_row[s:s + 1, :]
        segs = []
        for pr in range(SSD_HPG // 2):
            f0, f1 = 2 * pr, 2 * pr + 1
            b0, b1 = SSD_HPG + f0, SSD_HPG + f1
            segs.append((jnp.concatenate([seg(f0), seg(f1)], axis=1),
                         jnp.concatenate([seg(b0), seg(b1)], axis=1)))
        return rows, dt_row, w_row, bt, cb, segs

    def products(c, stage2):
        rows, dt_row, w_row, bt, cb, segs = stage2
        cb2 = jnp.concatenate([cb, cb], axis=1)
        bt2 = jnp.concatenate([bt, bt], axis=1)
        xb = xs_s[rows, :].astype(BF16)
        row2 = lambda v, s0, s1: jnp.concatenate([v[s0:s0 + 1, :], v[s1:s1 + 1, :]], axis=1)
        for pr in range(SSD_HPG // 2):
            ps = slice(pr * pair_w, (pr + 1) * pair_w)
            f0, f1 = 2 * pr, 2 * pr + 1
            b0, b1 = SSD_HPG + f0, SSD_HPG + f1
            seg_f, seg_b = segs[pr]
            dec = jnp.exp(jnp.where(lower2, seg_f, seg_b)) + jnp.where(diag2, row2(dt_row, b0, b1), 0.0)
            m = (cb2 * dec).astype(BF16)
            btw_f = (bt2 * row2(w_row, f0, f1)).astype(BF16)
            btw_b = (bt2 * row2(w_row, b0, b1)).astype(BF16)
            xp = xb[:, ps]
            xh = jnp.concatenate([xp * half_s[0], xp * half_s[1]], axis=0)
            r = _dot(jnp.concatenate([m, btw_f, btw_b], axis=0), xh)
            ys_s[rows, ps] = r[:q]
            new_s[0, c, :, ps] = r[q:2 * q]
            new_s[1, c, :, ps] = r[2 * q:]

    nchunks = seq // q
    group = 4 if nchunks % 4 == 0 else 2

    def local_pass(i, carry):
        chunks = [group * i + n for n in range(group)]
        stage1 = [cums(c) for c in chunks]
        stage2 = [segments(c, s) for c, s in zip(chunks, stage1)]
        for c, s in zip(chunks, stage2):
            products(c, s)
        return carry

    lax.fori_loop(0, nchunks // group, local_pass, 0)

    for d in range(2):
        for pr in range(SSD_HPG // 2):
            ps = slice(pr * pair_w, (pr + 1) * pair_w)
            if has_h0:
                st_s[:, ps] = h0_ref[d, 2 * pr:2 * pr + 2].reshape(pair_w, SSD_STATE).T
            else:
                st_s[:, ps] = jnp.zeros((SSD_STATE, pair_w), F32)

        def carry_state(i, carry, d=d):
            c = i if d == 0 else nchunks - 1 - i
            st = st_s[...]
            ent_s[d, c] = st.astype(BF16)
            st_s[...] = st * dtot_s[d, c][0:1, :] + new_s[d, c]
            return carry

        lax.fori_loop(0, nchunks, carry_state, 0, unroll=2)
        for pr in range(SSD_HPG // 2 if has_st else 0):
            ps = slice(pr * pair_w, (pr + 1) * pair_w)
            st_ref[d, 2 * pr:2 * pr + 2] = st_s[:, ps].T.reshape(2, SSD_HEADDIM, SSD_STATE)

    def finish(i, carry):
        chunks = [group * i + n for n in range(group)]
        rows = [pl.ds(pl.multiple_of(c * q, q), q) for c in chunks]
        offs = [[_dot(cs_s[r, :], ent_s[d, c]) for d in range(2)] for c, r in zip(chunks, rows)]
        e1s = [terms3(jnp.exp(csr_s[r, :])) for r in rows]
        spreads = [[_dot(e1, exp_s[d]) for d in range(2)] for e1 in e1s]
        for r, off, spread in zip(rows, offs, spreads):
            y = ys_s[r, :] + dsk_ref[...] * xs_s[r, :] + off[0] * spread[0] + off[1] * spread[1]
            y = y * _silu(z_ref[r, :].astype(F32))
            y = y * lax.rsqrt(jnp.mean(y * y, axis=-1, keepdims=True) + EPS) * nw_ref[...]
            y_ref[r, :] = y.astype(BF16)
        return carry

    lax.fori_loop(0, nchunks // group, finish, 0)


def _ssd(u, dt, nseq, seq, conv_w, conv_b, dt_bias, a_log, d_skip, norm_w, *,
         h0=None, h0_layer=0, st_layers=0, st_layer=0, st_prev=None):
    t = nseq * seq
    g = SSD_GROUPS
    e = SSD_HPG
    heads = g * e
    gw = e * SSD_HEADDIM
    d_inner = heads * SSD_HEADDIM
    nc = seq // SSD_CHUNK
    xb0 = d_inner // gw
    bb0 = 2 * d_inner // SSD_STATE
    per_group = lambda v: v.reshape(2, g, e).transpose(1, 0, 2).reshape(g, 2 * e)
    rep = LANE // (2 * e)
    dbg = per_group(dt_bias)
    alg = per_group(a_log)
    dbg_rep = jnp.tile(dbg, (1, rep)).reshape(g, 1, LANE)
    alg_rep = jnp.tile(alg, (1, rep)).reshape(g, 1, LANE)
    dsk = jnp.repeat(d_skip, SSD_HEADDIM).reshape(1, d_inner)
    cw = lambda width, blk0: pl.BlockSpec((CONV_W, width), lambda b, gi: (0, blk0 + gi))
    cbias = lambda width, blk0: pl.BlockSpec((1, width), lambda b, gi: (0, blk0 + gi))
    st_block = (None, None, 2, e, SSD_HEADDIM, SSD_STATE)
    in_specs = [
        pl.BlockSpec((seq, gw), lambda b, gi: (b, gi)),
        pl.BlockSpec((seq, gw), lambda b, gi: (b, xb0 + gi)),
        pl.BlockSpec((seq, SSD_STATE), lambda b, gi: (b, bb0 + gi)),
        pl.BlockSpec((seq, SSD_STATE), lambda b, gi: (b, bb0 + g + gi)),
        pl.BlockSpec((seq, LANE), lambda b, gi: (b, 0)),
        pl.BlockSpec((None, 1, LANE), lambda b, gi: (gi, 0, 0)),
        pl.BlockSpec((None, 1, LANE), lambda b, gi: (gi, 0, 0)),
        pl.BlockSpec((None, 2 * e, 1), lambda b, gi: (gi, 0, 0)),
        cw(gw, 0), cw(SSD_STATE, d_inner // SSD_STATE), cw(SSD_STATE, d_inner // SSD_STATE + g),
        cbias(gw, 0), cbias(SSD_STATE, d_inner // SSD_STATE), cbias(SSD_STATE, d_inner // SSD_STATE + g),
        cbias(gw, 0), cbias(gw, 0),
    ]
    conv_b2 = conv_b.reshape(1, -1)
    args = [u, u, u, u, dt, dbg_rep, alg_rep, alg.reshape(g, 2 * e, 1),
            conv_w, conv_w, conv_w, conv_b2, conv_b2, conv_b2, dsk, norm_w.reshape(1, d_inner)]
    if h0 is not None:
        in_specs.append(pl.BlockSpec(st_block, lambda b, gi: (b, h0_layer, 0, gi, 0, 0)))
        args.append(h0)
    out_specs = [pl.BlockSpec((seq, gw), lambda b, gi: (b, gi))]
    out_shape = [jax.ShapeDtypeStruct((t, d_inner), BF16)]
    aliases = {}
    if st_layers:
        if st_prev is not None:
            aliases = {len(args): 1}
            in_specs.append(pl.BlockSpec(memory_space=pl.ANY))
            args.append(st_prev)
        out_specs.append(pl.BlockSpec(st_block, lambda b, gi: (b, st_layer, 0, gi, 0, 0)))
        out_shape.append(jax.ShapeDtypeStruct((nseq, st_layers, 2, heads, SSD_HEADDIM, SSD_STATE), F32))
    return pl.pallas_call(
        functools.partial(_ssd_kernel, seq=seq, has_h0=h0 is not None, has_prev=st_prev is not None,
                          has_st=bool(st_layers)),
        grid=(nseq, g),
        in_specs=in_specs,
        out_specs=out_specs,
        out_shape=out_shape,
        input_output_aliases=aliases,
        scratch_shapes=[
            pltpu.VMEM((seq + 2 * HALO, gw), BF16),
            pltpu.VMEM((seq + 2 * HALO, SSD_STATE), BF16),
            pltpu.VMEM((seq + 2 * HALO, SSD_STATE), BF16),
            pltpu.VMEM((seq, gw), F32),
            pltpu.VMEM((SSD_STATE, seq), F32),
            pltpu.VMEM((seq, SSD_STATE), BF16),
            pltpu.VMEM((seq, gw), F32),
            pltpu.VMEM((SSD_STATE, gw), F32),
            pltpu.VMEM((2, SSD_CHUNK, gw), BF16),
            pltpu.VMEM((2, SSD_CHUNK, LANE), BF16),
            pltpu.VMEM((seq, LANE), F32),
            pltpu.VMEM((2, nc, SUBLANE, gw), F32),
            pltpu.VMEM((2, nc, SSD_STATE, gw), F32),
            pltpu.VMEM((2, nc, SSD_STATE, gw), BF16),
            pltpu.VMEM((SSD_CHUNK, LANE), BF16),
        ],
        compiler_params=_cparams("parallel", "parallel"),
        name="ssd",
    )(*args)


def _rope_tables(seq):
    rows = seq // GRID_W
    row = jnp.repeat(jnp.arange(rows), GRID_W).astype(F32)
    col = jnp.tile(jnp.arange(GRID_W), rows).astype(F32)
    quarter = DA_HALF // 4
    inv = ROPE_THETA ** (-jnp.arange(quarter, dtype=F32) / quarter)
    ang_r = row[:, None] * inv
    ang_c = col[:, None] * inv
    ang = jnp.concatenate([ang_r, ang_r, ang_c, ang_c], axis=-1)
    sign = jnp.tile(jnp.concatenate([-jnp.ones((quarter,), F32), jnp.ones((quarter,), F32)]), 2)
    cos = jnp.cos(ang)
    sin = jnp.sin(ang) * sign
    return jnp.tile(cos, (1, 2)), jnp.tile(sin, (1, 2))


def kernel(x_prompt, x_sample, c, cache_attn_k, cache_attn_v, state_lru, state_ssd, c_ctx, w_ada, b_ada, norm_g, lru_conv_w, lru_conv_b, lru_w_r, lru_b_r, lru_w_i, lru_b_i, lru_lambda, even_w_in, even_w_out, da_q_norm, da_k_norm, da_lambda, da_subln, ssd_w_in, ssd_conv_w, ssd_conv_b, ssd_dt_bias, ssd_a_log, ssd_d, ssd_norm_w, ssd_w_out, ffn_w_in, ffn_w_out):
    depth = w_ada.shape[0]
    batch, seq_p, d = x_prompt.shape
    dec_batch, seq_s, _ = x_sample.shape
    past = cache_attn_k.shape[2]
    heads = cache_attn_k.shape[3]
    d_rnn = lru_conv_w.shape[2]
    d_ff = ffn_w_out.shape[1]
    d_inner = ssd_w_out.shape[1]
    n_even = even_w_in.shape[0]

    cond = jnp.zeros((N_COND, d), F32).at[0].set(c_ctx).at[1:1 + dec_batch].set(c)
    mod_all = _ada(cond, w_ada, b_ada).reshape(depth, N_COND, 6, d)
    rope = _rope_tables(seq_s)
    cache_k = cache_attn_k.reshape(dec_batch, n_even, past, heads * DA_VDIM)
    cache_v = cache_attn_v.reshape(dec_batch, n_even, past, heads * DA_VDIM)

    even_in_b, even_out_b = even_w_in.astype(BF16), even_w_out.astype(BF16)
    n_odd = ssd_w_in.shape[0]
    ssd_heads = d_inner // SSD_HEADDIM
    n_main = ssd_w_in.shape[2] - 2 * ssd_heads
    ssd_in_b, ssd_out_b = ssd_w_in.astype(BF16), ssd_w_out.astype(BF16)
    hpg = ssd_heads // SSD_GROUPS
    ssd_dt_b = ssd_w_in[:, :, n_main:].astype(BF16).reshape(n_odd, d, 2, SSD_GROUPS, hpg).transpose(
        0, 1, 3, 2, 4).reshape(n_odd, d, 2 * ssd_heads)
    ffn_tn = 512
    ffn_in_b = ffn_w_in.astype(BF16).reshape(depth, d, 2 * d_ff // ffn_tn, ffn_tn).transpose(0, 2, 1, 3)
    ffn_out_b = ffn_w_out.astype(BF16).reshape(depth, d_ff, d // ffn_tn, ffn_tn).transpose(0, 2, 1, 3)

    xs = [x_prompt.reshape(batch * seq_p, d), x_sample.reshape(dec_batch * seq_s, d)]
    shapes = [(batch, seq_p), (dec_batch, seq_s)]
    new_kv, new_lru, new_ssd = None, [], None

    for i in range(depth):
        j = i // 2
        mods = [mod_all[i, 0:1], mod_all[i, 1:1 + dec_batch]]
        g_mix = norm_g[i, 0].reshape(1, d)
        g_ffn = norm_g[i, 1].reshape(1, d)
        if i % 2 == 0:
            lam_init = 0.8 - 0.6 * math.exp(-0.3 * i)
            w_r = lru_w_r[j].astype(BF16)
            w_i = lru_w_i[j].astype(BF16)
            for s in range(2):
                nseq, seq = shapes[s]
                u = _inproj(xs[s], g_mix, mods[s], even_in_b, j, mode="plain", n_out=even_in_b.shape[2],
                            tn=2560, shift_idx=0, scale_idx=1, name="even_in")
                h0 = jnp.zeros((nseq, 2, d_rnn), F32) if s == 0 else state_lru[:, j]
                rec, s_fin = _lru(u, nseq, seq, lru_conv_w[j], lru_conv_b[j], w_r, w_i,
                                  lru_b_r[j], lru_b_i[j], lru_lambda[j], h0)
                if s == 0:
                    att, kc, vc = _attn(u, nseq, seq, heads, lam_init, da_q_norm[j], da_k_norm[j],
                                        da_lambda[j], da_subln[j], kv_layers=n_even, kv_layer=j,
                                        kv_prev=new_kv)
                    new_kv = (kc, vc)
                    new_lru.append(s_fin)
                else:
                    att = _attn(u, nseq, seq, heads, lam_init, da_q_norm[j], da_k_norm[j],
                                da_lambda[j], da_subln[j], ctx=(cache_k, cache_v, j), rope=rope)
                xs[s] = _outproj([rec, att], even_out_b, j, xs[s], mods[s], gate_idx=2, tn=1024,
                                 name="even_out")
        else:
            for s in range(2):
                nseq, seq = shapes[s]
                u, dt = _inproj(xs[s], g_mix, mods[s], ssd_in_b, j, mode="dt", n_out=n_main, tn=2048,
                                shift_idx=0, scale_idx=1, name="odd_in", w_dt=ssd_dt_b)
                ssd_args = (u, dt, nseq, seq, ssd_conv_w[j], ssd_conv_b[j], ssd_dt_bias[j],
                            ssd_a_log[j], ssd_d[j], ssd_norm_w[j])
                if s == 0:
                    y, new_ssd = _ssd(*ssd_args, st_layers=n_odd, st_layer=j, st_prev=new_ssd)
                else:
                    y, = _ssd(*ssd_args, h0=state_ssd, h0_layer=j)
                xs[s] = _outproj([y], ssd_out_b, j, xs[s], mods[s], gate_idx=2, tn=1024, name="odd_out")
        for s in range(2):
            act = _inproj(xs[s], g_ffn, mods[s], ffn_in_b, i, mode="glu", n_out=d_ff, tn=ffn_tn,
                          shift_idx=3, scale_idx=4, name="ffn_in")
            xs[s] = _outproj([act], ffn_out_b, i, xs[s], mods[s], gate_idx=5, tn=ffn_tn, name="ffn_out",
                             tile_major=True)

    return (xs[0].reshape(batch, seq_p, d), xs[1].reshape(dec_batch, seq_s, d),
            new_kv[0].reshape(batch, n_even, seq_p, heads, DA_VDIM),
            new_kv[1].reshape(batch, n_even, seq_p, heads, DA_VDIM),
            jnp.stack(new_lru, axis=1), new_ssd)
```

```python
import functools
import math

import jax
import jax.numpy as jnp
from jax import lax
from jax.experimental import pallas as pl
from jax.experimental.pallas import tpu as pltpu

F32 = jnp.float32
BF16 = jnp.bfloat16

EPS = 1e-6
GRID_W = 64
CONV_W = 4
CONV_LEFT = CONV_W // 2
LRU_BW = 128
LRU_C = 8.0
DA_HALF = 64
DA_VDIM = 2 * DA_HALF
ROPE_THETA = 10000.0
SSD_HEADDIM = 64
SSD_GROUPS = 8
SSD_HPG = 8
SSD_STATE = 128
SSD_CHUNK = 128
N_COND = 16

LANE = 128
SUBLANE = 8
VMEM_LIMIT = 56 * 1024 * 1024

ROW_TILE = 1024
MOD_ROWS = 128
CONV_ROWS = 256
CONV_BLK = 128
HALO = CONV_BLK // 2
LOG2E = 1.4426950408889634
SQRT_FLOOR = 1e-37


def _sigmoid(x):
    return 0.5 * jnp.tanh(0.5 * x) + 0.5


def _silu(x):
    h = 0.5 * x
    return h + h * jnp.tanh(h)


def _cparams(*sem):
    return pltpu.CompilerParams(dimension_semantics=sem, vmem_limit_bytes=VMEM_LIMIT)


def _ada_kernel(c_ref, w_ref, b_ref, o_ref):
    s = _silu(c_ref[...]).astype(BF16)
    o_ref[...] = jnp.dot(s, w_ref[...].astype(BF16), preferred_element_type=F32) + b_ref[...]


def _ada(cond, w_ada, b_ada):
    depth, d, n = w_ada.shape
    tn = 1024
    return pl.pallas_call(
        _ada_kernel,
        grid=(depth, n // tn),
        in_specs=[
            pl.BlockSpec((N_COND, d), lambda l, j: (0, 0)),
            pl.BlockSpec((None, d, tn), lambda l, j: (l, 0, j)),
            pl.BlockSpec((None, 1, tn), lambda l, j: (l, 0, j)),
        ],
        out_specs=pl.BlockSpec((None, N_COND, tn), lambda l, j: (l, 0, j)),
        out_shape=jax.ShapeDtypeStruct((depth, N_COND, n), F32),
        compiler_params=_cparams("parallel", "parallel"),
        name="ada",
    )(cond, w_ada, b_ada.reshape(depth, 1, n))


def _modulate_into(x_ref, g_ref, mod_ref, h_ref, shift_idx, scale_idx):
    gs = g_ref[...] * (1.0 + mod_ref[scale_idx:scale_idx + 1, :])
    shift = mod_ref[shift_idx:shift_idx + 1, :]

    def body(r, carry):
        rows = pl.ds(pl.multiple_of(r * MOD_ROWS, MOD_ROWS), MOD_ROWS)
        x = x_ref[rows, :]
        ms = jnp.mean(x * x, axis=-1, keepdims=True)
        h_ref[rows, :] = (x * lax.rsqrt(ms + EPS) * gs + shift).astype(BF16)
        return carry

    lax.fori_loop(0, x_ref.shape[0] // MOD_ROWS, body, 0)


def _inproj_kernel(*refs, mode, shift_idx, scale_idx):
    if mode == "plain":
        x_ref, g_ref, mod_ref, w_ref, o_ref, h_ref = refs
    elif mode == "glu":
        x_ref, g_ref, mod_ref, wg_ref, wu_ref, o_ref, h_ref = refs
    else:
        x_ref, g_ref, mod_ref, w_ref, wdt_ref, o_ref, dt_ref, h_ref = refs

    @pl.when(pl.program_id(1) == 0)
    def _():
        _modulate_into(x_ref, g_ref, mod_ref, h_ref, shift_idx, scale_idx)
        if mode == "dt":
            dt_ref[...] = jnp.dot(h_ref[...], wdt_ref[...], preferred_element_type=F32)

    h = h_ref[...]
    if mode == "glu":
        g = jnp.dot(h, wg_ref[...], preferred_element_type=F32)
        u = jnp.dot(h, wu_ref[...], preferred_element_type=F32)
        o_ref[...] = (_silu(g) * u).astype(o_ref.dtype)
    else:
        o_ref[...] = jnp.dot(h, w_ref[...], preferred_element_type=F32).astype(o_ref.dtype)


def _inproj(x, g, mod, w, layer, *, mode, n_out, tn, shift_idx, scale_idx, name, w_dt=None):
    t, d = x.shape
    tm = ROW_TILE
    rows_per_cond = t // mod.shape[0]
    nj = n_out // tn
    x_spec = pl.BlockSpec((tm, d), lambda i, j: (i, 0))
    g_spec = pl.BlockSpec((1, d), lambda i, j: (0, 0))
    mod_spec = pl.BlockSpec((None, 6, d), lambda i, j: ((i * tm) // rows_per_cond, 0, 0))
    w_spec = pl.BlockSpec((None, d, tn), lambda i, j: (layer, 0, j))
    o_spec = pl.BlockSpec((tm, tn), lambda i, j: (i, j))
    o_shape = jax.ShapeDtypeStruct((t, n_out), BF16)
    if mode == "plain":
        in_specs = [x_spec, g_spec, mod_spec, w_spec]
        args = (x, g, mod, w)
        out_specs, out_shape = o_spec, o_shape
    elif mode == "glu":
        in_specs = [x_spec, g_spec, mod_spec, w_spec,
                    pl.BlockSpec((None, d, tn), lambda i, j: (layer, 0, nj + j))]
        args = (x, g, mod, w, w)
        out_specs, out_shape = o_spec, o_shape
    else:
        in_specs = [x_spec, g_spec, mod_spec, w_spec,
                    pl.BlockSpec((None, d, LANE), lambda i, j: (layer, 0, 0))]
        args = (x, g, mod, w, w_dt)
        out_specs = (o_spec, pl.BlockSpec((tm, LANE), lambda i, j: (i, 0)))
        out_shape = (o_shape, jax.ShapeDtypeStruct((t, LANE), F32))
    return pl.pallas_call(
        functools.partial(_inproj_kernel, mode=mode, shift_idx=shift_idx, scale_idx=scale_idx),
        grid=(t // tm, nj),
        in_specs=in_specs,
        out_specs=out_specs,
        out_shape=out_shape,
        scratch_shapes=[pltpu.VMEM((tm, d), BF16)],
        compiler_params=_cparams("parallel", "arbitrary"),
        name=name,
    )(*args)


def _outproj_kernel(*refs, n_a, gate_idx):
    a_refs, w_refs = refs[:n_a], refs[n_a:2 * n_a]
    x_ref, mod_ref, o_ref = refs[2 * n_a:]
    acc = jnp.dot(a_refs[0][...], w_refs[0][...], preferred_element_type=F32)
    for a_ref, w_ref in zip(a_refs[1:], w_refs[1:]):
        acc = acc + jnp.dot(a_ref[...], w_ref[...], preferred_element_type=F32)
    o_ref[...] = x_ref[...] + mod_ref[gate_idx:gate_idx + 1, :] * acc


def _outproj(a_list, w, layer, x, mod, *, gate_idx, tn, name):
    t, d = x.shape
    tm = ROW_TILE
    rows_per_cond = t // mod.shape[0]
    n_a = len(a_list)
    k = a_list[0].shape[1]
    in_specs = [pl.BlockSpec((tm, k), lambda i, j: (i, 0)) for _ in a_list]
    in_specs += [pl.BlockSpec((None, k, tn), functools.partial(lambda i, j, q: (layer, q, j), q=q))
                 for q in range(n_a)]
    in_specs += [pl.BlockSpec((tm, tn), lambda i, j: (i, j)),
                 pl.BlockSpec((None, 6, tn), lambda i, j: ((i * tm) // rows_per_cond, 0, j))]
    return pl.pallas_call(
        functools.partial(_outproj_kernel, n_a=n_a, gate_idx=gate_idx),
        grid=(t // tm, d // tn),
        in_specs=in_specs,
        out_specs=pl.BlockSpec((tm, tn), lambda i, j: (i, j)),
        out_shape=jax.ShapeDtypeStruct((t, d), F32),
        compiler_params=_cparams("parallel", "parallel"),
        name=name,
    )(*a_list, *([w] * n_a), x, mod)


def _conv_blocks(src_ref, w_ref, b_ref, pad_ref, seq, width, emit):
    assert CONV_LEFT == 2 and CONV_W == 4
    zeros = jnp.zeros((HALO, width), BF16)
    pad_ref[0:HALO, 0:width] = zeros
    pad_ref[HALO + seq:2 * HALO + seq, 0:width] = zeros
    pad_ref[HALO:HALO + seq, 0:width] = src_ref[...]
    taps = [k for k in range(CONV_W) if k != CONV_LEFT]
    r_i = lax.broadcasted_iota(jnp.int32, (len(taps) * CONV_BLK, CONV_BLK), 0)
    c_i = lax.broadcasted_iota(jnp.int32, (len(taps) * CONV_BLK, CONV_BLK), 1)
    tap_i = r_i // CONV_BLK
    off = jnp.where(tap_i >= CONV_LEFT, tap_i + 1, tap_i) - CONV_LEFT
    shift = (c_i == (r_i % CONV_BLK) + off).astype(BF16)
    sub = lax.broadcasted_iota(jnp.int32, (SUBLANE, width), 0)
    edge = 2 * SUBLANE
    w = w_ref[...]
    for r0 in range(0, seq, CONV_BLK):
        lo = HALO + r0
        cur = pad_ref[lo:lo + CONV_BLK, 0:width]
        sh = jnp.dot(shift, cur, preferred_element_type=F32)
        acc = b_ref[...] + w[CONV_LEFT:CONV_LEFT + 1, :] * cur.astype(F32)
        for i, k in enumerate(taps):
            acc = acc + w[k:k + 1, :] * sh[i * CONV_BLK:(i + 1) * CONV_BLK]
        before = pad_ref[lo - edge:lo, 0:width].astype(F32)
        after = pad_ref[lo + CONV_BLK:lo + CONV_BLK + edge, 0:width].astype(F32)
        xm2, xm1, xp0 = before[edge - 2:edge - 1], before[edge - 1:edge], after[0:1]
        top = jnp.where(sub == 0, w[0:1, :] * xm2 + w[1:2, :] * xm1, jnp.where(sub == 1, w[0:1, :] * xm1, 0.0))
        bot = jnp.where(sub == SUBLANE - 1, w[3:4, :] * xp0, 0.0)
        acc = jnp.concatenate([acc[:SUBLANE] + top, acc[SUBLANE:CONV_BLK - SUBLANE],
                               acc[CONV_BLK - SUBLANE:] + bot], axis=0)
        emit(r0, acc)


def _lru_kernel(gate_ref, xr_ref, cw_ref, cb_ref, wr_ref, wi_ref, br_ref, bi_ref, lam_ref, h0_ref,
                rec_ref, sfin_ref, pad_s, xc_s, a_s, b_s, h_s, *, seq, width):
    def put_xc(r0, v):
        xc_s[r0:r0 + CONV_BLK, :] = v

    _conv_blocks(xr_ref, cw_ref, cb_ref, pad_s, seq, width, put_xc)
    step = min(seq, CONV_ROWS)
    tiles = step // SUBLANE

    def gates(d):
        kk = (-0.5 * LRU_C) * jax.nn.softplus(-lam_ref[d:d + 1, :])
        for r0 in range(0, seq, step):
            t0 = r0 // SUBLANE
            for k in range(width // LRU_BW):
                ks = slice(k * LRU_BW, (k + 1) * LRU_BW)
                xk = xc_s[r0:r0 + step, ks]
                xb = xk.astype(BF16)
                t_r = jnp.tanh(0.5 * (jnp.dot(xb, wr_ref[d, k], preferred_element_type=F32) + br_ref[d:d + 1, ks]))
                t_i = jnp.tanh(0.5 * (jnp.dot(xb, wi_ref[d, k], preferred_element_type=F32) + bi_ref[d:d + 1, ks]))
                a = jnp.exp(kk[:, ks] * t_r + kk[:, ks])
                hx = 0.5 * xk
                y = 1.0 - a * a
                b = y * lax.rsqrt(jnp.maximum(y, SQRT_FLOOR)) * (hx * t_i + hx)
                a_s[d, t0:t0 + tiles, :, ks] = a.reshape(tiles, SUBLANE, LRU_BW)
                b_s[d, t0:t0 + tiles, :, ks] = b.reshape(tiles, SUBLANE, LRU_BW)

    gates(0)
    gates(1)
    ntile = seq // SUBLANE

    def scan_tile(i, carry):
        hf, hb = carry
        jb = ntile - 1 - i
        for r in range(SUBLANE):
            rb = SUBLANE - 1 - r
            hf = a_s[0, i, r:r + 1, :] * hf + b_s[0, i, r:r + 1, :]
            h_s[0, i, r:r + 1, :] = hf
            hb = a_s[1, jb, rb:rb + 1, :] * hb + b_s[1, jb, rb:rb + 1, :]
            h_s[1, jb, rb:rb + 1, :] = hb
        return hf, hb

    hf, hb = lax.fori_loop(0, ntile, scan_tile, (h0_ref[0:1, :], h0_ref[1:2, :]))
    sfin_ref[0:1, :] = hf
    sfin_ref[1:2, :] = hb

    for r0 in range(0, seq, step):
        t0 = r0 // SUBLANE
        h = (h_s[0, t0:t0 + tiles] + h_s[1, t0:t0 + tiles]).reshape(step, width)
        rec_ref[r0:r0 + step, :] = (h * jax.nn.gelu(gate_ref[r0:r0 + step, :].astype(F32))).astype(BF16)


def _lru(u, nseq, seq, conv_w, conv_b, w_r, w_i, b_r, b_i, lam, h0):
    d_rnn = conv_w.shape[1]
    width = 512
    nb = d_rnn // width
    kb = width // LRU_BW
    vec = lambda rows: pl.BlockSpec((rows, width), lambda b, c: (0, c))
    return pl.pallas_call(
        functools.partial(_lru_kernel, seq=seq, width=width),
        grid=(nseq, nb),
        in_specs=[
            pl.BlockSpec((seq, width), lambda b, c: (b, c)),
            pl.BlockSpec((seq, width), lambda b, c: (b, nb + c)),
            vec(CONV_W), vec(1),
            pl.BlockSpec((2, kb, LRU_BW, LRU_BW), lambda b, c: (0, c, 0, 0)),
            pl.BlockSpec((2, kb, LRU_BW, LRU_BW), lambda b, c: (0, c, 0, 0)),
            vec(2), vec(2), vec(2),
            pl.BlockSpec((None, 2, width), lambda b, c: (b, 0, c)),
        ],
        out_specs=(pl.BlockSpec((seq, width), lambda b, c: (b, c)),
                   pl.BlockSpec((None, 2, width), lambda b, c: (b, 0, c))),
        out_shape=(jax.ShapeDtypeStruct((nseq * seq, d_rnn), BF16),
                   jax.ShapeDtypeStruct((nseq, 2, d_rnn), F32)),
        scratch_shapes=[pltpu.VMEM((seq + 2 * HALO, width), BF16), pltpu.VMEM((seq, width), F32)]
        + [pltpu.VMEM((2, seq // SUBLANE, SUBLANE, width), F32)] * 3,
        compiler_params=_cparams("parallel", "parallel"),
        name="lru",
    )(u, u, conv_w, conv_b.reshape(1, d_rnn), w_r, w_i, b_r, b_i, lam, h0)


def _half_rms(x, gain):
    x2 = x * x
    s0 = jnp.sum(x2[:, :DA_HALF], axis=-1, keepdims=True)
    s1 = jnp.sum(x2[:, DA_HALF:], axis=-1, keepdims=True)
    lane = lax.broadcasted_iota(jnp.int32, x.shape, 1)
    ms = jnp.where(lane < DA_HALF, s0, s1) * (1.0 / DA_HALF)
    return x * lax.rsqrt(ms + EPS) * gain


def _rope(x, cos, sin_signed):
    q = DA_HALF // 4
    lane = lax.broadcasted_iota(jnp.int32, x.shape, 1)
    rot = jnp.where((lane % (2 * q)) < q, pltpu.roll(x, LANE - q, 1), pltpu.roll(x, q, 1))
    return x * cos + rot * sin_signed


def _attn_kernel(*refs, latent, seq, past, lam_init, sub, has_prev):
    if latent:
        (q_ref, k_ref, v_ref, kc_ref, vc_ref, cosq_ref, sinq_ref, cosk_ref, sink_ref,
         qn_ref, kn_ref, dl_ref, sub_ref, o_ref, kt_s, vx_s) = refs
    else:
        if has_prev:
            refs = refs[:7] + refs[9:]
        (q_ref, k_ref, v_ref, qn_ref, kn_ref, dl_ref, sub_ref, o_ref, ko_ref, vo_ref, kt_s, vx_s) = refs
    hd = DA_VDIM
    dl = dl_ref[...]
    lam = (jnp.exp(jnp.sum(dl[0:1] * dl[1:2], axis=-1, keepdims=True))
           - jnp.exp(jnp.sum(dl[2:3] * dl[3:4], axis=-1, keepdims=True)) + lam_init)

    for hh in range(kt_s.shape[0]):
        hs = slice(hh * hd, (hh + 1) * hd)
        vx_s[hh, :, hd:2 * hd] = jnp.ones((seq + past, hd), BF16)
        step = min(seq, CONV_ROWS)
        for r0 in range(0, seq, step):
            rows = slice(r0, r0 + step)
            kn = _half_rms(k_ref[rows, hs].astype(F32), kn_ref[...])
            if latent:
                kn = _rope(kn, cosk_ref[rows, :], sink_ref[rows, :])
            else:
                ko_ref[rows, hs] = kn
                vo_ref[rows, hs] = v_ref[rows, hs].astype(F32)
            kt_s[hh, :, rows] = kn.T.astype(BF16)
            vx_s[hh, rows, 0:hd] = v_ref[rows, hs]
        if latent:
            kt_s[hh, :, seq:seq + past] = kc_ref[...].T.astype(BF16)
            vx_s[hh, seq:seq + past, 0:hd] = vc_ref[...].astype(BF16)

        blocks = range(0, seq, sub)
        scores = []
        for r0 in blocks:
            rows = slice(r0, r0 + sub)
            qn = _half_rms(q_ref[rows, hs].astype(F32), qn_ref[...])
            if latent:
                qn = _rope(qn, cosq_ref[rows, :], sinq_ref[rows, :])
            qb = (qn * (DA_HALF ** -0.5 * LOG2E)).astype(BF16)
            scores.append([jnp.dot(qb[:, m * DA_HALF:(m + 1) * DA_HALF],
                                   kt_s[hh, m * DA_HALF:(m + 1) * DA_HALF, :],
                                   preferred_element_type=F32) for m in range(2)])
        for r0, (s0, s1) in zip(blocks, scores):
            rows = slice(r0, r0 + sub)
            outs = []
            for s in (s0, s1):
                p = jnp.exp2(s - jnp.max(s, axis=-1, keepdims=True)).astype(BF16)
                outs.append(jnp.dot(p, vx_s[hh], preferred_element_type=F32))
            o = outs[0][:, :hd] / outs[0][:, hd:] - lam * (outs[1][:, :hd] / outs[1][:, hd:])
            o = o * lax.rsqrt(jnp.mean(o * o, axis=-1, keepdims=True) + EPS) * sub_ref[...] * (1.0 - lam_init)
            o_ref[rows, hs] = o.astype(BF16)


def _attn(u, nseq, seq, heads, lam_init, q_norm, k_norm, da_lam, subln, ctx=None, rope=None,
          kv_layers=0, kv_layer=0, kv_prev=None):
    latent = ctx is not None
    hd = DA_VDIM
    sub = 256
    hps = 1 if latent else heads
    wid = hps * hd
    nh = heads // hps
    col0 = (u.shape[1] - 3 * heads * hd) // wid
    past = ctx[0].shape[2] if latent else 0
    q_spec = pl.BlockSpec((seq, wid), lambda b, h: (b, col0 + h))
    k_spec = pl.BlockSpec((seq, wid), lambda b, h: (b, col0 + nh + h))
    v_spec = pl.BlockSpec((seq, wid), lambda b, h: (b, col0 + 2 * nh + h))
    small = lambda r, c: pl.BlockSpec((r, c), lambda b, h: (0, 0))
    par_specs = [small(1, hd), small(1, hd), small(4, DA_HALF), small(1, hd)]
    pars = (q_norm.reshape(1, hd), k_norm.reshape(1, hd), da_lam, subln.reshape(1, hd))
    o_spec = pl.BlockSpec((seq, wid), lambda b, h: (b, h))
    o_shape = jax.ShapeDtypeStruct((nseq * seq, heads * hd), BF16)
    aliases = {}
    if latent:
        kc, vc, layer = ctx
        cos, sin = rope
        c_spec = pl.BlockSpec((None, None, past, hd), lambda b, h: (b, layer, 0, h))
        t_spec = pl.BlockSpec((seq, hd), lambda b, h: (0, 0))
        in_specs = [q_spec, k_spec, v_spec, c_spec, c_spec, t_spec, t_spec, t_spec, t_spec] + par_specs
        args = (u, u, u, kc, vc, cos, sin, cos, sin) + pars
        out_specs, out_shape = o_spec, o_shape
    else:
        in_specs = [q_spec, k_spec, v_spec] + par_specs
        args = (u, u, u) + pars
        if kv_prev is not None:
            aliases = {len(args): 1, len(args) + 1: 2}
            in_specs += [pl.BlockSpec(memory_space=pl.ANY)] * 2
            args += tuple(kv_prev)
        kv_spec = pl.BlockSpec((None, None, seq, wid), lambda b, h: (b, kv_layer, 0, h))
        kv_shape = jax.ShapeDtypeStruct((nseq, kv_layers, seq, heads * hd), F32)
        out_specs, out_shape = (o_spec, kv_spec, kv_spec), (o_shape, kv_shape, kv_shape)
    return pl.pallas_call(
        functools.partial(_attn_kernel, latent=latent, seq=seq, past=past, lam_init=lam_init, sub=sub,
                          has_prev=kv_prev is not None),
        grid=(nseq, nh),
        in_specs=in_specs,
        out_specs=out_specs,
        out_shape=out_shape,
        input_output_aliases=aliases,
        scratch_shapes=[pltpu.VMEM((hps, hd, seq + past), BF16), pltpu.VMEM((hps, seq + past, 2 * hd), BF16)],
        compiler_params=_cparams("parallel", "parallel"),
        name="attn",
    )(*args)


def _split3(x):
    hi = x.astype(BF16)
    r1 = x - hi.astype(F32)
    mid = r1.astype(BF16)
    lo = (r1 - mid.astype(F32)).astype(BF16)
    return hi, mid, lo


def _dot(a, b):
    return jnp.dot(a, b, preferred_element_type=F32)


def _ssd_kernel(*refs, seq, has_h0, has_prev, has_st):
    (z_ref, x_ref, b_ref, c_ref, dt_ref, dbc_ref, alc_ref, alr_ref,
     cwx_ref, cwb_ref, cwc_ref, cbx_ref, cbb_ref, cbc_ref, dsk_ref, nw_ref) = refs[:16]
    rest = refs[16:]
    if has_h0:
        h0_ref, rest = rest[0], rest[1:]
    if has_prev:
        rest = rest[1:]
    y_ref, rest = rest[0], rest[1:]
    if has_st:
        st_ref, rest = rest[0], rest[1:]
    (pad_s, padb_s, padc_s, xs_s, bt_s, cs_s, ys_s, st_s, exp_s, half_s,
     csr_s, dtot_s, new_s, ent_s, sel_s, dtrep_s) = rest

    q = SSD_CHUNK
    nslot = 2 * SSD_HPG
    pair_w = 2 * SSD_HEADDIM
    gw = xs_s.shape[1]

    def put_x(r0, v):
        xs_s[r0:r0 + CONV_BLK, :] = _silu(v)

    def put_b(r0, v):
        bt_s[:, r0:r0 + CONV_BLK] = _silu(v).T

    def put_c(r0, v):
        cs_s[r0:r0 + CONV_BLK, :] = _silu(v).astype(BF16)

    _conv_blocks(x_ref, cwx_ref, cbx_ref, pad_s, seq, gw, put_x)
    _conv_blocks(b_ref, cwb_ref, cbb_ref, padb_s, seq, SSD_STATE, put_b)
    _conv_blocks(c_ref, cwc_ref, cbc_ref, padc_s, seq, SSD_STATE, put_c)

    li = lax.broadcasted_iota(jnp.int32, (q, q), 0)
    si = lax.broadcasted_iota(jnp.int32, (q, q), 1)
    lower = (li >= si)
    upper = (li <= si)
    lower_b = lower.astype(BF16)
    upper_b = upper.astype(BF16)
    a_rep = -jnp.exp(alc_ref[...])
    a_row = -jnp.exp(alr_ref[...])
    for i in range(2):
        half_s[i] = (si // SSD_HEADDIM == i).astype(BF16)
    sel_s[...] = (li == pl.program_id(1) * nslot + si % nslot).astype(BF16)
    lower2 = jnp.concatenate([lower, lower], axis=1)
    diag2 = jnp.concatenate([li == si, li == si], axis=1)
    ki = lax.broadcasted_iota(jnp.int32, (q, gw), 0)
    ji = lax.broadcasted_iota(jnp.int32, (q, gw), 1)
    for d in range(2):
        exp_s[d] = ((ki % nslot == d * SSD_HPG + ji // SSD_HEADDIM) & (ki < 3 * nslot)).astype(BF16)

    def terms3(v):
        lane = lax.broadcasted_iota(jnp.int32, v.shape, 1)
        hi = v.astype(BF16).astype(F32)
        mid = (v - hi).astype(BF16).astype(F32)
        lo = v - hi - mid
        return jnp.where(lane < nslot, hi, jnp.where(lane < 2 * nslot, mid, jnp.where(
            lane < 3 * nslot, lo, 0.0))).astype(BF16)

    dt_step = min(seq, CONV_ROWS)
    for r0 in range(0, seq, dt_step):
        raw_rep = sum(_dot(part, sel_s[...]) for part in _split3(dt_ref[r0:r0 + dt_step, :]))
        dtrep_s[r0:r0 + dt_step, :] = jax.nn.softplus(raw_rep + dbc_ref[...])

    fwd_lane = (si % nslot) < SSD_HPG
    fwd_row = lax.broadcasted_iota(jnp.int32, (nslot, q), 0) < SSD_HPG

    def cums(c):
        rows = pl.ds(pl.multiple_of(c * q, q), q)
        dt_rep = dtrep_s[rows, :]
        da_rep = dt_rep * a_rep
        dt_row = dt_rep.T[0:nslot]
        da_row = dt_row * a_row
        csf_rep = sum(_dot(lower_b, part) for part in _split3(da_rep))
        csf_row = sum(_dot(part, upper_b) for part in _split3(da_row))
        bt = bt_s[:, rows]
        cb = _dot(cs_s[rows, :], bt.astype(BF16))
        return rows, dt_row, da_rep, da_row, csf_rep, csf_row, bt, cb

    def segments(c, stage1):
        rows, dt_row, da_rep, da_row, csf_rep, csf_row, bt, cb = stage1
        tot_rep = csf_rep[q - 1:q, :]
        tot_row = csf_row[:, q - 1:q]
        cs_rep = jnp.where(fwd_lane, csf_rep, tot_rep - csf_rep + da_rep)
        cs_row = jnp.where(fwd_row, csf_row, tot_row - csf_row + da_row)
        csr_s[rows, :] = cs_rep
        et = terms3(jnp.broadcast_to(jnp.exp(tot_rep), (SUBLANE, q)))
        for d in range(2):
            dtot_s[d, c] = _dot(et, exp_s[d])
        w_row = dt_row * jnp.exp(tot_row - cs_row)
        neg_row = cs_row - jnp.log(dt_row)
        seg = lambda s: cs_rep[:, s:s + 1] - neg_row[s:s + 1, :]
        segs = []
        for pr in range(SSD_HPG // 2):
            f0, f1 = 2 * pr, 2 * pr + 1
            b0, b1 = SSD_HPG + f0, SSD_HPG + f1
            segs.append((jnp.concatenate([seg(f0), seg(f1)], axis=1),
                         jnp.concatenate([seg(b0), seg(b1)], axis=1)))
        return rows, dt_row, w_row, bt, cb, segs

    def products(c, stage2):
        rows, dt_row, w_row, bt, cb, segs = stage2
        cb2 = jnp.concatenate([cb, cb], axis=1)
        bt2 = jnp.concatenate([bt, bt], axis=1)
        xb = xs_s[rows, :].astype(BF16)
        row2 = lambda v, s0, s1: jnp.concatenate([v[s0:s0 + 1, :], v[s1:s1 + 1, :]], axis=1)
        for pr in range(SSD_HPG // 2):
            ps = slice(pr * pair_w, (pr + 1) * pair_w)
            f0, f1 = 2 * pr, 2 * pr + 1
            b0, b1 = SSD_HPG + f0, SSD_HPG + f1
            seg_f, seg_b = segs[pr]
            dec = jnp.exp(jnp.where(lower2, seg_f, seg_b)) + jnp.where(diag2, row2(dt_row, b0, b1), 0.0)
            m = (cb2 * dec).astype(BF16)
            btw_f = (bt2 * row2(w_row, f0, f1)).astype(BF16)
            btw_b = (bt2 * row2(w_row, b0, b1)).astype(BF16)
            xp = xb[:, ps]
            xh = jnp.concatenate([xp * half_s[0], xp * half_s[1]], axis=0)
            r = _dot(jnp.concatenate([m, btw_f, btw_b], axis=0), xh)
            ys_s[rows, ps] = r[:q]
            new_s[0, c, :, ps] = r[q:2 * q]
            new_s[1, c, :, ps] = r[2 * q:]

    nchunks = seq // q
    group = 4 if nchunks % 4 == 0 else 2

    def local_pass(i, carry):
        chunks = [group * i + n for n in range(group)]
        stage1 = [cums(c) for c in chunks]
        stage2 = [segments(c, s) for c, s in zip(chunks, stage1)]
        for c, s in zip(chunks, stage2):
            products(c, s)
        return carry

    lax.fori_loop(0, nchunks // group, local_pass, 0)

    for d in range(2):
        for pr in range(SSD_HPG // 2):
            ps = slice(pr * pair_w, (pr + 1) * pair_w)
            if has_h0:
                st_s[:, ps] = h0_ref[d, 2 * pr:2 * pr + 2].reshape(pair_w, SSD_STATE).T
            else:
                st_s[:, ps] = jnp.zeros((SSD_STATE, pair_w), F32)

        def carry_state(i, carry, d=d):
            c = i if d == 0 else nchunks - 1 - i
            st = st_s[...]
            ent_s[d, c] = st.astype(BF16)
            st_s[...] = st * dtot_s[d, c][0:1, :] + new_s[d, c]
            return carry

        lax.fori_loop(0, nchunks, carry_state, 0, unroll=2)
        for pr in range(SSD_HPG // 2 if has_st else 0):
            ps = slice(pr * pair_w, (pr + 1) * pair_w)
            st_ref[d, 2 * pr:2 * pr + 2] = st_s[:, ps].T.reshape(2, SSD_HEADDIM, SSD_STATE)

    def finish(i, carry):
        chunks = [group * i + n for n in range(group)]
        rows = [pl.ds(pl.multiple_of(c * q, q), q) for c in chunks]
        offs = [[_dot(cs_s[r, :], ent_s[d, c]) for d in range(2)] for c, r in zip(chunks, rows)]
        e1s = [terms3(jnp.exp(csr_s[r, :])) for r in rows]
        spreads = [[_dot(e1, exp_s[d]) for d in range(2)] for e1 in e1s]
        for r, off, spread in zip(rows, offs, spreads):
            y = ys_s[r, :] + dsk_ref[...] * xs_s[r, :] + off[0] * spread[0] + off[1] * spread[1]
            y = y * _silu(z_ref[r, :].astype(F32))
            y = y * lax.rsqrt(jnp.mean(y * y, axis=-1, keepdims=True) + EPS) * nw_ref[...]
            y_ref[r, :] = y.astype(BF16)
        return carry

    lax.fori_loop(0, nchunks // group, finish, 0)


def _ssd(u, dt, nseq, seq, conv_w, conv_b, dt_bias, a_log, d_skip, norm_w, *,
         h0=None, h0_layer=0, st_layers=0, st_layer=0, st_prev=None):
    t = nseq * seq
    g = SSD_GROUPS
    e = SSD_HPG
    heads = g * e
    gw = e * SSD_HEADDIM
    d_inner = heads * SSD_HEADDIM
    nc = seq // SSD_CHUNK
    xb0 = d_inner // gw
    bb0 = 2 * d_inner // SSD_STATE
    per_group = lambda v: v.reshape(2, g, e).transpose(1, 0, 2).reshape(g, 2 * e)
    rep = LANE // (2 * e)
    dbg = per_group(dt_bias)
    alg = per_group(a_log)
    dbg_rep = jnp.tile(dbg, (1, rep)).reshape(g, 1, LANE)
    alg_rep = jnp.tile(alg, (1, rep)).reshape(g, 1, LANE)
    dsk = jnp.repeat(d_skip, SSD_HEADDIM).reshape(1, d_inner)
    cw = lambda width, blk0: pl.BlockSpec((CONV_W, width), lambda b, gi: (0, blk0 + gi))
    cbias = lambda width, blk0: pl.BlockSpec((1, width), lambda b, gi: (0, blk0 + gi))
    st_block = (None, None, 2, e, SSD_HEADDIM, SSD_STATE)
    in_specs = [
        pl.BlockSpec((seq, gw), lambda b, gi: (b, gi)),
        pl.BlockSpec((seq, gw), lambda b, gi: (b, xb0 + gi)),
        pl.BlockSpec((seq, SSD_STATE), lambda b, gi: (b, bb0 + gi)),
        pl.BlockSpec((seq, SSD_STATE), lambda b, gi: (b, bb0 + g + gi)),
        pl.BlockSpec((seq, LANE), lambda b, gi: (b, 0)),
        pl.BlockSpec((None, 1, LANE), lambda b, gi: (gi, 0, 0)),
        pl.BlockSpec((None, 1, LANE), lambda b, gi: (gi, 0, 0)),
        pl.BlockSpec((None, 2 * e, 1), lambda b, gi: (gi, 0, 0)),
        cw(gw, 0), cw(SSD_STATE, d_inner // SSD_STATE), cw(SSD_STATE, d_inner // SSD_STATE + g),
        cbias(gw, 0), cbias(SSD_STATE, d_inner // SSD_STATE), cbias(SSD_STATE, d_inner // SSD_STATE + g),
        cbias(gw, 0), cbias(gw, 0),
    ]
    conv_b2 = conv_b.reshape(1, -1)
    args = [u, u, u, u, dt, dbg_rep, alg_rep, alg.reshape(g, 2 * e, 1),
            conv_w, conv_w, conv_w, conv_b2, conv_b2, conv_b2, dsk, norm_w.reshape(1, d_inner)]
    if h0 is not None:
        in_specs.append(pl.BlockSpec(st_block, lambda b, gi: (b, h0_layer, 0, gi, 0, 0)))
        args.append(h0)
    out_specs = [pl.BlockSpec((seq, gw), lambda b, gi: (b, gi))]
    out_shape = [jax.ShapeDtypeStruct((t, d_inner), BF16)]
    aliases = {}
    if st_layers:
        if st_prev is not None:
            aliases = {len(args): 1}
            in_specs.append(pl.BlockSpec(memory_space=pl.ANY))
            args.append(st_prev)
        out_specs.append(pl.BlockSpec(st_block, lambda b, gi: (b, st_layer, 0, gi, 0, 0)))
        out_shape.append(jax.ShapeDtypeStruct((nseq, st_layers, 2, heads, SSD_HEADDIM, SSD_STATE), F32))
    return pl.pallas_call(
        functools.partial(_ssd_kernel, seq=seq, has_h0=h0 is not None, has_prev=st_prev is not None,
                          has_st=bool(st_layers)),
        grid=(nseq, g),
        in_specs=in_specs,
        out_specs=out_specs,
        out_shape=out_shape,
        input_output_aliases=aliases,
        scratch_shapes=[
            pltpu.VMEM((seq + 2 * HALO, gw), BF16),
            pltpu.VMEM((seq + 2 * HALO, SSD_STATE), BF16),
            pltpu.VMEM((seq + 2 * HALO, SSD_STATE), BF16),
            pltpu.VMEM((seq, gw), F32),
            pltpu.VMEM((SSD_STATE, seq), F32),
            pltpu.VMEM((seq, SSD_STATE), BF16),
            pltpu.VMEM((seq, gw), F32),
            pltpu.VMEM((SSD_STATE, gw), F32),
            pltpu.VMEM((2, SSD_CHUNK, gw), BF16),
            pltpu.VMEM((2, SSD_CHUNK, LANE), BF16),
            pltpu.VMEM((seq, LANE), F32),
            pltpu.VMEM((2, nc, SUBLANE, gw), F32),
            pltpu.VMEM((2, nc, SSD_STATE, gw), F32),
            pltpu.VMEM((2, nc, SSD_STATE, gw), BF16),
            pltpu.VMEM((SSD_CHUNK, LANE), BF16),
            pltpu.VMEM((seq, LANE), F32),
        ],
        compiler_params=_cparams("parallel", "parallel"),
        name="ssd",
    )(*args)


def _rope_tables(seq):
    rows = seq // GRID_W
    row = jnp.repeat(jnp.arange(rows), GRID_W).astype(F32)
    col = jnp.tile(jnp.arange(GRID_W), rows).astype(F32)
    quarter = DA_HALF // 4
    inv = ROPE_THETA ** (-jnp.arange(quarter, dtype=F32) / quarter)
    ang_r = row[:, None] * inv
    ang_c = col[:, None] * inv
    ang = jnp.concatenate([ang_r, ang_r, ang_c, ang_c], axis=-1)
    sign = jnp.tile(jnp.concatenate([-jnp.ones((quarter,), F32), jnp.ones((quarter,), F32)]), 2)
    cos = jnp.cos(ang)
    sin = jnp.sin(ang) * sign
    return jnp.tile(cos, (1, 2)), jnp.tile(sin, (1, 2))


def kernel(x_prompt, x_sample, c, cache_attn_k, cache_attn_v, state_lru, state_ssd, c_ctx, w_ada, b_ada, norm_g, lru_conv_w, lru_conv_b, lru_w_r, lru_b_r, lru_w_i, lru_b_i, lru_lambda, even_w_in, even_w_out, da_q_norm, da_k_norm, da_lambda, da_subln, ssd_w_in, ssd_conv_w, ssd_conv_b, ssd_dt_bias, ssd_a_log, ssd_d, ssd_norm_w, ssd_w_out, ffn_w_in, ffn_w_out):
    depth = w_ada.shape[0]
    batch, seq_p, d = x_prompt.shape
    dec_batch, seq_s, _ = x_sample.shape
    past = cache_attn_k.shape[2]
    heads = cache_attn_k.shape[3]
    d_rnn = lru_conv_w.shape[2]
    d_ff = ffn_w_out.shape[1]
    d_inner = ssd_w_out.shape[1]
    n_even = even_w_in.shape[0]

    cond = jnp.zeros((N_COND, d), F32).at[0].set(c_ctx).at[1:1 + dec_batch].set(c)
    mod_all = _ada(cond, w_ada, b_ada).reshape(depth, N_COND, 6, d)
    rope = _rope_tables(seq_s)
    cache_k = cache_attn_k.reshape(dec_batch, n_even, past, heads * DA_VDIM)
    cache_v = cache_attn_v.reshape(dec_batch, n_even, past, heads * DA_VDIM)

    even_in_b, even_out_b = even_w_in.astype(BF16), even_w_out.astype(BF16)
    n_odd = ssd_w_in.shape[0]
    ssd_heads = d_inner // SSD_HEADDIM
    n_main = ssd_w_in.shape[2] - 2 * ssd_heads
    ssd_in_b, ssd_out_b = ssd_w_in.astype(BF16), ssd_w_out.astype(BF16)
    hpg = ssd_heads // SSD_GROUPS
    ssd_dt_b = ssd_w_in[:, :, n_main:].astype(BF16).reshape(n_odd, d, 2, SSD_GROUPS, hpg).transpose(
        0, 1, 3, 2, 4).reshape(n_odd, d, 2 * ssd_heads)
    ffn_in_b, ffn_out_b = ffn_w_in.astype(BF16), ffn_w_out.astype(BF16)

    xs = [x_prompt.reshape(batch * seq_p, d), x_sample.reshape(dec_batch * seq_s, d)]
    shapes = [(batch, seq_p), (dec_batch, seq_s)]
    new_kv, new_lru, new_ssd = None, [], None

    for i in range(depth):
        j = i // 2
        mods = [mod_all[i, 0:1], mod_all[i, 1:1 + dec_batch]]
        g_mix = norm_g[i, 0].reshape(1, d)
        g_ffn = norm_g[i, 1].reshape(1, d)
        if i % 2 == 0:
            lam_init = 0.8 - 0.6 * math.exp(-0.3 * i)
            w_r = lru_w_r[j].astype(BF16)
            w_i = lru_w_i[j].astype(BF16)
            for s in range(2):
                nseq, seq = shapes[s]
                u = _inproj(xs[s], g_mix, mods[s], even_in_b, j, mode="plain", n_out=even_in_b.shape[2],
                            tn=2560, shift_idx=0, scale_idx=1, name="even_in")
                h0 = jnp.zeros((nseq, 2, d_rnn), F32) if s == 0 else state_lru[:, j]
                rec, s_fin = _lru(u, nseq, seq, lru_conv_w[j], lru_conv_b[j], w_r, w_i,
                                  lru_b_r[j], lru_b_i[j], lru_lambda[j], h0)
                if s == 0:
                    att, kc, vc = _attn(u, nseq, seq, heads, lam_init, da_q_norm[j], da_k_norm[j],
                                        da_lambda[j], da_subln[j], kv_layers=n_even, kv_layer=j,
                                        kv_prev=new_kv)
                    new_kv = (kc, vc)
                    new_lru.append(s_fin)
                else:
                    att = _attn(u, nseq, seq, heads, lam_init, da_q_norm[j], da_k_norm[j],
                                da_lambda[j], da_subln[j], ctx=(cache_k, cache_v, j), rope=rope)
                xs[s] = _outproj([rec, att], even_out_b, j, xs[s], mods[s], gate_idx=2, tn=1024,
                                 name="even_out")
        else:
            for s in range(2):
                nseq, seq = shapes[s]
                u, dt = _inproj(xs[s], g_mix, mods[s], ssd_in_b, j, mode="dt", n_out=n_main, tn=2048,
                                shift_idx=0, scale_idx=1, name="odd_in", w_dt=ssd_dt_b)
                ssd_args = (u, dt, nseq, seq, ssd_conv_w[j], ssd_conv_b[j], ssd_dt_bias[j],
                            ssd_a_log[j], ssd_d[j], ssd_norm_w[j])
                if s == 0:
                    y, new_ssd = _ssd(*ssd_args, st_layers=n_odd, st_layer=j, st_prev=new_ssd)
                else:
                    y, = _ssd(*ssd_args, h0=state_ssd, h0_layer=j)
                xs[s] = _outproj([y], ssd_out_b, j, xs[s], mods[s], gate_idx=2, tn=1024, name="odd_out")
        for s in range(2):
            act = _inproj(xs[s], g_ffn, mods[s], ffn_in_b, i, mode="glu", n_out=d_ff, tn=512,
                          shift_idx=3, scale_idx=4, name="ffn_in")
            xs[s] = _outproj([act], ffn_out_b, i, xs[s], mods[s], gate_idx=5, tn=512, name="ffn_out")

    return (xs[0].reshape(batch, seq_p, d), xs[1].reshape(dec_batch, seq_s, d),
            new_kv[0].reshape(batch, n_even, seq_p, heads, DA_VDIM),
            new_kv[1].reshape(batch, n_even, seq_p, heads, DA_VDIM),
            jnp.stack(new_lru, axis=1), new_ssd)
```

```python
import functools
import math

import jax
import jax.numpy as jnp
from jax import lax
from jax.experimental import pallas as pl
from jax.experimental.pallas import tpu as pltpu

F32 = jnp.float32
BF16 = jnp.bfloat16

EPS = 1e-6
GRID_W = 64
CONV_W = 4
CONV_LEFT = CONV_W // 2
LRU_BW = 128
LRU_C = 8.0
DA_HALF = 64
DA_VDIM = 2 * DA_HALF
ROPE_THETA = 10000.0
SSD_HEADDIM = 64
SSD_GROUPS = 8
SSD_HPG = 8
SSD_STATE = 128
SSD_CHUNK = 128
N_COND = 16

LANE = 128
SUBLANE = 8
VMEM_LIMIT = 56 * 1024 * 1024

ROW_TILE = 1024
MOD_ROWS = 128
CONV_ROWS = 256
CONV_BLK = 128
HALO = CONV_BLK // 2
LOG2E = 1.4426950408889634
SQRT_FLOOR = 1e-37


def _sigmoid(x):
    return 0.5 * jnp.tanh(0.5 * x) + 0.5


def _silu(x):
    h = 0.5 * x
    return h + h * jnp.tanh(h)


def _cparams(*sem):
    return pltpu.CompilerParams(dimension_semantics=sem, vmem_limit_bytes=VMEM_LIMIT)


def _ada_kernel(c_ref, w_ref, b_ref, o_ref):
    s = _silu(c_ref[...]).astype(BF16)
    o_ref[...] = jnp.dot(s, w_ref[...].astype(BF16), preferred_element_type=F32) + b_ref[...]


def _ada(cond, w_ada, b_ada):
    depth, d, n = w_ada.shape
    tn = 1024
    return pl.pallas_call(
        _ada_kernel,
        grid=(depth, n // tn),
        in_specs=[
            pl.BlockSpec((N_COND, d), lambda l, j: (0, 0)),
            pl.BlockSpec((None, d, tn), lambda l, j: (l, 0, j)),
            pl.BlockSpec((None, 1, tn), lambda l, j: (l, 0, j)),
        ],
        out_specs=pl.BlockSpec((None, N_COND, tn), lambda l, j: (l, 0, j)),
        out_shape=jax.ShapeDtypeStruct((depth, N_COND, n), F32),
        compiler_params=_cparams("parallel", "parallel"),
        name="ada",
    )(cond, w_ada, b_ada.reshape(depth, 1, n))


def _modulate_into(x_ref, g_ref, mod_ref, h_ref, shift_idx, scale_idx):
    gs = g_ref[...] * (1.0 + mod_ref[scale_idx:scale_idx + 1, :])
    shift = mod_ref[shift_idx:shift_idx + 1, :]

    def body(r, carry):
        rows = pl.ds(pl.multiple_of(r * MOD_ROWS, MOD_ROWS), MOD_ROWS)
        x = x_ref[rows, :]
        ms = jnp.mean(x * x, axis=-1, keepdims=True)
        h_ref[rows, :] = (x * lax.rsqrt(ms + EPS) * gs + shift).astype(BF16)
        return carry

    lax.fori_loop(0, x_ref.shape[0] // MOD_ROWS, body, 0)


def _inproj_kernel(*refs, mode, shift_idx, scale_idx):
    if mode == "plain":
        x_ref, g_ref, mod_ref, w_ref, o_ref, h_ref = refs
    elif mode == "glu":
        x_ref, g_ref, mod_ref, wg_ref, wu_ref, o_ref, h_ref = refs
    else:
        x_ref, g_ref, mod_ref, w_ref, wdt_ref, o_ref, dt_ref, h_ref = refs

    @pl.when(pl.program_id(1) == 0)
    def _():
        _modulate_into(x_ref, g_ref, mod_ref, h_ref, shift_idx, scale_idx)
        if mode == "dt":
            dt_ref[...] = jnp.dot(h_ref[...], wdt_ref[...], preferred_element_type=F32)

    h = h_ref[...]
    if mode == "glu":
        g = jnp.dot(h, wg_ref[...], preferred_element_type=F32)
        u = jnp.dot(h, wu_ref[...], preferred_element_type=F32)
        o_ref[...] = (_silu(g) * u).astype(o_ref.dtype)
    else:
        o_ref[...] = jnp.dot(h, w_ref[...], preferred_element_type=F32).astype(o_ref.dtype)


def _inproj(x, g, mod, w, layer, *, mode, n_out, tn, shift_idx, scale_idx, name, w_dt=None):
    t, d = x.shape
    tm = ROW_TILE
    rows_per_cond = t // mod.shape[0]
    nj = n_out // tn
    x_spec = pl.BlockSpec((tm, d), lambda i, j: (i, 0))
    g_spec = pl.BlockSpec((1, d), lambda i, j: (0, 0))
    mod_spec = pl.BlockSpec((None, 6, d), lambda i, j: ((i * tm) // rows_per_cond, 0, 0))
    w_spec = pl.BlockSpec((None, d, tn), lambda i, j: (layer, 0, j))
    o_spec = pl.BlockSpec((tm, tn), lambda i, j: (i, j))
    o_shape = jax.ShapeDtypeStruct((t, n_out), BF16)
    if mode == "plain":
        in_specs = [x_spec, g_spec, mod_spec, w_spec]
        args = (x, g, mod, w)
        out_specs, out_shape = o_spec, o_shape
    elif mode == "glu":
        in_specs = [x_spec, g_spec, mod_spec, w_spec,
                    pl.BlockSpec((None, d, tn), lambda i, j: (layer, 0, nj + j))]
        args = (x, g, mod, w, w)
        out_specs, out_shape = o_spec, o_shape
    else:
        in_specs = [x_spec, g_spec, mod_spec, w_spec,
                    pl.BlockSpec((None, d, LANE), lambda i, j: (layer, 0, 0))]
        args = (x, g, mod, w, w_dt)
        out_specs = (o_spec, pl.BlockSpec((tm, LANE), lambda i, j: (i, 0)))
        out_shape = (o_shape, jax.ShapeDtypeStruct((t, LANE), F32))
    return pl.pallas_call(
        functools.partial(_inproj_kernel, mode=mode, shift_idx=shift_idx, scale_idx=scale_idx),
        grid=(t // tm, nj),
        in_specs=in_specs,
        out_specs=out_specs,
        out_shape=out_shape,
        scratch_shapes=[pltpu.VMEM((tm, d), BF16)],
        compiler_params=_cparams("parallel", "arbitrary"),
        name=name,
    )(*args)


def _outproj_kernel(*refs, n_a, gate_idx):
    a_refs, w_refs = refs[:n_a], refs[n_a:2 * n_a]
    x_ref, mod_ref, o_ref = refs[2 * n_a:]
    acc = jnp.dot(a_refs[0][...], w_refs[0][...], preferred_element_type=F32)
    for a_ref, w_ref in zip(a_refs[1:], w_refs[1:]):
        acc = acc + jnp.dot(a_ref[...], w_ref[...], preferred_element_type=F32)
    o_ref[...] = x_ref[...] + mod_ref[gate_idx:gate_idx + 1, :] * acc


def _outproj(a_list, w, layer, x, mod, *, gate_idx, tn, name):
    t, d = x.shape
    tm = ROW_TILE
    rows_per_cond = t // mod.shape[0]
    n_a = len(a_list)
    k = a_list[0].shape[1]
    in_specs = [pl.BlockSpec((tm, k), lambda i, j: (i, 0)) for _ in a_list]
    in_specs += [pl.BlockSpec((None, k, tn), functools.partial(lambda i, j, q: (layer, q, j), q=q))
                 for q in range(n_a)]
    in_specs += [pl.BlockSpec((tm, tn), lambda i, j: (i, j)),
                 pl.BlockSpec((None, 6, tn), lambda i, j: ((i * tm) // rows_per_cond, 0, j))]
    return pl.pallas_call(
        functools.partial(_outproj_kernel, n_a=n_a, gate_idx=gate_idx),
        grid=(t // tm, d // tn),
        in_specs=in_specs,
        out_specs=pl.BlockSpec((tm, tn), lambda i, j: (i, j)),
        out_shape=jax.ShapeDtypeStruct((t, d), F32),
        compiler_params=_cparams("parallel", "parallel"),
        name=name,
    )(*a_list, *([w] * n_a), x, mod)


def _conv_blocks(src_ref, w_ref, b_ref, pad_ref, seq, width, emit):
    assert CONV_LEFT == 2 and CONV_W == 4
    zeros = jnp.zeros((HALO, width), BF16)
    pad_ref[0:HALO, 0:width] = zeros
    pad_ref[HALO + seq:2 * HALO + seq, 0:width] = zeros
    pad_ref[HALO:HALO + seq, 0:width] = src_ref[...]
    taps = [k for k in range(CONV_W) if k != CONV_LEFT]
    r_i = lax.broadcasted_iota(jnp.int32, (len(taps) * CONV_BLK, CONV_BLK), 0)
    c_i = lax.broadcasted_iota(jnp.int32, (len(taps) * CONV_BLK, CONV_BLK), 1)
    tap_i = r_i // CONV_BLK
    off = jnp.where(tap_i >= CONV_LEFT, tap_i + 1, tap_i) - CONV_LEFT
    shift = (c_i == (r_i % CONV_BLK) + off).astype(BF16)
    sub = lax.broadcasted_iota(jnp.int32, (SUBLANE, width), 0)
    edge = 2 * SUBLANE
    w = w_ref[...]
    for r0 in range(0, seq, CONV_BLK):
        lo = HALO + r0
        cur = pad_ref[lo:lo + CONV_BLK, 0:width]
        sh = jnp.dot(shift, cur, preferred_element_type=F32)
        acc = b_ref[...] + w[CONV_LEFT:CONV_LEFT + 1, :] * cur.astype(F32)
        for i, k in enumerate(taps):
            acc = acc + w[k:k + 1, :] * sh[i * CONV_BLK:(i + 1) * CONV_BLK]
        before = pad_ref[lo - edge:lo, 0:width].astype(F32)
        after = pad_ref[lo + CONV_BLK:lo + CONV_BLK + edge, 0:width].astype(F32)
        xm2, xm1, xp0 = before[edge - 2:edge - 1], before[edge - 1:edge], after[0:1]
        top = jnp.where(sub == 0, w[0:1, :] * xm2 + w[1:2, :] * xm1, jnp.where(sub == 1, w[0:1, :] * xm1, 0.0))
        bot = jnp.where(sub == SUBLANE - 1, w[3:4, :] * xp0, 0.0)
        acc = jnp.concatenate([acc[:SUBLANE] + top, acc[SUBLANE:CONV_BLK - SUBLANE],
                               acc[CONV_BLK - SUBLANE:] + bot], axis=0)
        emit(r0, acc)


def _lru_kernel(gate_ref, xr_ref, cw_ref, cb_ref, wr_ref, wi_ref, br_ref, bi_ref, lam_ref, h0_ref,
                rec_ref, sfin_ref, pad_s, xc_s, a_s, b_s, h_s, *, seq, width):
    def put_xc(r0, v):
        xc_s[r0:r0 + CONV_BLK, :] = v

    _conv_blocks(xr_ref, cw_ref, cb_ref, pad_s, seq, width, put_xc)
    step = min(seq, CONV_ROWS)
    tiles = step // SUBLANE

    def gates(d):
        kk = (-0.5 * LRU_C) * jax.nn.softplus(-lam_ref[d:d + 1, :])
        for r0 in range(0, seq, step):
            t0 = r0 // SUBLANE
            for k in range(width // LRU_BW):
                ks = slice(k * LRU_BW, (k + 1) * LRU_BW)
                xk = xc_s[r0:r0 + step, ks]
                xb = xk.astype(BF16)
                t_r = jnp.tanh(0.5 * (jnp.dot(xb, wr_ref[d, k], preferred_element_type=F32) + br_ref[d:d + 1, ks]))
                t_i = jnp.tanh(0.5 * (jnp.dot(xb, wi_ref[d, k], preferred_element_type=F32) + bi_ref[d:d + 1, ks]))
                a = jnp.exp(kk[:, ks] * t_r + kk[:, ks])
                hx = 0.5 * xk
                y = 1.0 - a * a
                b = y * lax.rsqrt(jnp.maximum(y, SQRT_FLOOR)) * (hx * t_i + hx)
                a_s[d, t0:t0 + tiles, :, ks] = a.reshape(tiles, SUBLANE, LRU_BW)
                b_s[d, t0:t0 + tiles, :, ks] = b.reshape(tiles, SUBLANE, LRU_BW)

    gates(0)
    gates(1)
    ntile = seq // SUBLANE

    def scan_tile(i, carry):
        hf, hb = carry
        jb = ntile - 1 - i
        for r in range(SUBLANE):
            rb = SUBLANE - 1 - r
            hf = a_s[0, i, r:r + 1, :] * hf + b_s[0, i, r:r + 1, :]
            h_s[0, i, r:r + 1, :] = hf
            hb = a_s[1, jb, rb:rb + 1, :] * hb + b_s[1, jb, rb:rb + 1, :]
            h_s[1, jb, rb:rb + 1, :] = hb
        return hf, hb

    hf, hb = lax.fori_loop(0, ntile, scan_tile, (h0_ref[0:1, :], h0_ref[1:2, :]))
    sfin_ref[0:1, :] = hf
    sfin_ref[1:2, :] = hb

    for r0 in range(0, seq, step):
        t0 = r0 // SUBLANE
        h = (h_s[0, t0:t0 + tiles] + h_s[1, t0:t0 + tiles]).reshape(step, width)
        rec_ref[r0:r0 + step, :] = (h * jax.nn.gelu(gate_ref[r0:r0 + step, :].astype(F32))).astype(BF16)


def _lru(u, nseq, seq, conv_w, conv_b, w_r, w_i, b_r, b_i, lam, h0):
    d_rnn = conv_w.shape[1]
    width = 512
    nb = d_rnn // width
    kb = width // LRU_BW
    vec = lambda rows: pl.BlockSpec((rows, width), lambda b, c: (0, c))
    return pl.pallas_call(
        functools.partial(_lru_kernel, seq=seq, width=width),
        grid=(nseq, nb),
        in_specs=[
            pl.BlockSpec((seq, width), lambda b, c: (b, c)),
            pl.BlockSpec((seq, width), lambda b, c: (b, nb + c)),
            vec(CONV_W), vec(1),
            pl.BlockSpec((2, kb, LRU_BW, LRU_BW), lambda b, c: (0, c, 0, 0)),
            pl.BlockSpec((2, kb, LRU_BW, LRU_BW), lambda b, c: (0, c, 0, 0)),
            vec(2), vec(2), vec(2),
            pl.BlockSpec((None, 2, width), lambda b, c: (b, 0, c)),
        ],
        out_specs=(pl.BlockSpec((seq, width), lambda b, c: (b, c)),
                   pl.BlockSpec((None, 2, width), lambda b, c: (b, 0, c))),
        out_shape=(jax.ShapeDtypeStruct((nseq * seq, d_rnn), BF16),
                   jax.ShapeDtypeStruct((nseq, 2, d_rnn), F32)),
        scratch_shapes=[pltpu.VMEM((seq + 2 * HALO, width), BF16), pltpu.VMEM((seq, width), F32)]
        + [pltpu.VMEM((2, seq // SUBLANE, SUBLANE, width), F32)] * 3,
        compiler_params=_cparams("parallel", "parallel"),
        name="lru",
    )(u, u, conv_w, conv_b.reshape(1, d_rnn), w_r, w_i, b_r, b_i, lam, h0)


def _half_rms(x, gain):
    x2 = x * x
    s0 = jnp.sum(x2[:, :DA_HALF], axis=-1, keepdims=True)
    s1 = jnp.sum(x2[:, DA_HALF:], axis=-1, keepdims=True)
    lane = lax.broadcasted_iota(jnp.int32, x.shape, 1)
    ms = jnp.where(lane < DA_HALF, s0, s1) * (1.0 / DA_HALF)
    return x * lax.rsqrt(ms + EPS) * gain


def _rope(x, cos, sin_signed):
    q = DA_HALF // 4
    lane = lax.broadcasted_iota(jnp.int32, x.shape, 1)
    rot = jnp.where((lane % (2 * q)) < q, pltpu.roll(x, LANE - q, 1), pltpu.roll(x, q, 1))
    return x * cos + rot * sin_signed


def _attn_kernel(*refs, latent, seq, past, lam_init, sub, has_prev):
    if latent:
        (q_ref, k_ref, v_ref, kc_ref, vc_ref, cosq_ref, sinq_ref, cosk_ref, sink_ref,
         qn_ref, kn_ref, dl_ref, sub_ref, o_ref, kt_s, vx_s) = refs
    else:
        if has_prev:
            refs = refs[:7] + refs[9:]
        (q_ref, k_ref, v_ref, qn_ref, kn_ref, dl_ref, sub_ref, o_ref, ko_ref, vo_ref, kt_s, vx_s) = refs
    hd = DA_VDIM
    dl = dl_ref[...]
    lam = (jnp.exp(jnp.sum(dl[0:1] * dl[1:2], axis=-1, keepdims=True))
           - jnp.exp(jnp.sum(dl[2:3] * dl[3:4], axis=-1, keepdims=True)) + lam_init)

    for hh in range(kt_s.shape[0]):
        hs = slice(hh * hd, (hh + 1) * hd)
        vx_s[hh, :, hd:2 * hd] = jnp.ones((seq + past, hd), BF16)
        step = min(seq, CONV_ROWS)
        for r0 in range(0, seq, step):
            rows = slice(r0, r0 + step)
            kn = _half_rms(k_ref[rows, hs].astype(F32), kn_ref[...])
            if latent:
                kn = _rope(kn, cosk_ref[rows, :], sink_ref[rows, :])
            else:
                ko_ref[rows, hs] = kn
                vo_ref[rows, hs] = v_ref[rows, hs].astype(F32)
            kt_s[hh, :, rows] = kn.T.astype(BF16)
            vx_s[hh, rows, 0:hd] = v_ref[rows, hs]
        if latent:
            kt_s[hh, :, seq:seq + past] = kc_ref[...].T.astype(BF16)
            vx_s[hh, seq:seq + past, 0:hd] = vc_ref[...].astype(BF16)

        blocks = range(0, seq, sub)
        scores = []
        for r0 in blocks:
            rows = slice(r0, r0 + sub)
            qn = _half_rms(q_ref[rows, hs].astype(F32), qn_ref[...])
            if latent:
                qn = _rope(qn, cosq_ref[rows, :], sinq_ref[rows, :])
            qb = (qn * (DA_HALF ** -0.5 * LOG2E)).astype(BF16)
            scores.append([jnp.dot(qb[:, m * DA_HALF:(m + 1) * DA_HALF],
                                   kt_s[hh, m * DA_HALF:(m + 1) * DA_HALF, :],
                                   preferred_element_type=F32) for m in range(2)])
        for r0, (s0, s1) in zip(blocks, scores):
            rows = slice(r0, r0 + sub)
            outs = []
            for s in (s0, s1):
                p = jnp.exp2(s - jnp.max(s, axis=-1, keepdims=True)).astype(BF16)
                outs.append(jnp.dot(p, vx_s[hh], preferred_element_type=F32))
            o = outs[0][:, :hd] / outs[0][:, hd:] - lam * (outs[1][:, :hd] / outs[1][:, hd:])
            o = o * lax.rsqrt(jnp.mean(o * o, axis=-1, keepdims=True) + EPS) * sub_ref[...] * (1.0 - lam_init)
            o_ref[rows, hs] = o.astype(BF16)


def _attn(u, nseq, seq, heads, lam_init, q_norm, k_norm, da_lam, subln, ctx=None, rope=None,
          kv_layers=0, kv_layer=0, kv_prev=None):
    latent = ctx is not None
    hd = DA_VDIM
    sub = 256
    hps = 1 if latent else heads
    wid = hps * hd
    nh = heads // hps
    col0 = (u.shape[1] - 3 * heads * hd) // wid
    past = ctx[0].shape[2] if latent else 0
    q_spec = pl.BlockSpec((seq, wid), lambda b, h: (b, col0 + h))
    k_spec = pl.BlockSpec((seq, wid), lambda b, h: (b, col0 + nh + h))
    v_spec = pl.BlockSpec((seq, wid), lambda b, h: (b, col0 + 2 * nh + h))
    small = lambda r, c: pl.BlockSpec((r, c), lambda b, h: (0, 0))
    par_specs = [small(1, hd), small(1, hd), small(4, DA_HALF), small(1, hd)]
    pars = (q_norm.reshape(1, hd), k_norm.reshape(1, hd), da_lam, subln.reshape(1, hd))
    o_spec = pl.BlockSpec((seq, wid), lambda b, h: (b, h))
    o_shape = jax.ShapeDtypeStruct((nseq * seq, heads * hd), BF16)
    aliases = {}
    if latent:
        kc, vc, layer = ctx
        cos, sin = rope
        c_spec = pl.BlockSpec((None, None, past, hd), lambda b, h: (b, layer, 0, h))
        t_spec = pl.BlockSpec((seq, hd), lambda b, h: (0, 0))
        in_specs = [q_spec, k_spec, v_spec, c_spec, c_spec, t_spec, t_spec, t_spec, t_spec] + par_specs
        args = (u, u, u, kc, vc, cos, sin, cos, sin) + pars
        out_specs, out_shape = o_spec, o_shape
    else:
        in_specs = [q_spec, k_spec, v_spec] + par_specs
        args = (u, u, u) + pars
        if kv_prev is not None:
            aliases = {len(args): 1, len(args) + 1: 2}
            in_specs += [pl.BlockSpec(memory_space=pl.ANY)] * 2
            args += tuple(kv_prev)
        kv_spec = pl.BlockSpec((None, None, seq, wid), lambda b, h: (b, kv_layer, 0, h))
        kv_shape = jax.ShapeDtypeStruct((nseq, kv_layers, seq, heads * hd), F32)
        out_specs, out_shape = (o_spec, kv_spec, kv_spec), (o_shape, kv_shape, kv_shape)
    return pl.pallas_call(
        functools.partial(_attn_kernel, latent=latent, seq=seq, past=past, lam_init=lam_init, sub=sub,
                          has_prev=kv_prev is not None),
        grid=(nseq, nh),
        in_specs=in_specs,
        out_specs=out_specs,
        out_shape=out_shape,
        input_output_aliases=aliases,
        scratch_shapes=[pltpu.VMEM((hps, hd, seq + past), BF16), pltpu.VMEM((hps, seq + past, 2 * hd), BF16)],
        compiler_params=_cparams("parallel", "parallel"),
        name="attn",
    )(*args)


def _split3(x):
    hi = x.astype(BF16)
    r1 = x - hi.astype(F32)
    mid = r1.astype(BF16)
    lo = (r1 - mid.astype(F32)).astype(BF16)
    return hi, mid, lo


def _dot(a, b):
    return jnp.dot(a, b, preferred_element_type=F32)


def _ssd_kernel(*refs, seq, has_h0, has_prev, has_st):
    (z_ref, x_ref, b_ref, c_ref, dt_ref, dbc_ref, alc_ref, alr_ref,
     cwx_ref, cwb_ref, cwc_ref, cbx_ref, cbb_ref, cbc_ref, dsk_ref, nw_ref) = refs[:16]
    rest = refs[16:]
    if has_h0:
        h0_ref, rest = rest[0], rest[1:]
    if has_prev:
        rest = rest[1:]
    y_ref, rest = rest[0], rest[1:]
    if has_st:
        st_ref, rest = rest[0], rest[1:]
    (pad_s, padb_s, padc_s, xs_s, bt_s, cs_s, ys_s, st_s, exp_s, half_s,
     csr_s, dtot_s, new_s, ent_s, sel_s, dtrep_s) = rest

    q = SSD_CHUNK
    nslot = 2 * SSD_HPG
    pair_w = 2 * SSD_HEADDIM
    gw = xs_s.shape[1]

    def put_x(r0, v):
        xs_s[r0:r0 + CONV_BLK, :] = _silu(v)

    def put_b(r0, v):
        bt_s[:, r0:r0 + CONV_BLK] = _silu(v).T

    def put_c(r0, v):
        cs_s[r0:r0 + CONV_BLK, :] = _silu(v).astype(BF16)

    _conv_blocks(x_ref, cwx_ref, cbx_ref, pad_s, seq, gw, put_x)
    _conv_blocks(b_ref, cwb_ref, cbb_ref, padb_s, seq, SSD_STATE, put_b)
    _conv_blocks(c_ref, cwc_ref, cbc_ref, padc_s, seq, SSD_STATE, put_c)

    li = lax.broadcasted_iota(jnp.int32, (q, q), 0)
    si = lax.broadcasted_iota(jnp.int32, (q, q), 1)
    lower = (li >= si)
    upper = (li <= si)
    lower_b = lower.astype(BF16)
    upper_b = upper.astype(BF16)
    a_rep = -jnp.exp(alc_ref[...])
    a_row = -jnp.exp(alr_ref[...])
    for i in range(2):
        half_s[i] = (si // SSD_HEADDIM == i).astype(BF16)
    sel_s[...] = (li == pl.program_id(1) * nslot + si % nslot).astype(BF16)
    lower2 = jnp.concatenate([lower, lower], axis=1)
    diag2 = jnp.concatenate([li == si, li == si], axis=1)
    ki = lax.broadcasted_iota(jnp.int32, (q, gw), 0)
    ji = lax.broadcasted_iota(jnp.int32, (q, gw), 1)
    for d in range(2):
        exp_s[d] = ((ki % nslot == d * SSD_HPG + ji // SSD_HEADDIM) & (ki < 3 * nslot)).astype(BF16)

    def terms3(v):
        lane = lax.broadcasted_iota(jnp.int32, v.shape, 1)
        hi = v.astype(BF16).astype(F32)
        mid = (v - hi).astype(BF16).astype(F32)
        lo = v - hi - mid
        return jnp.where(lane < nslot, hi, jnp.where(lane < 2 * nslot, mid, jnp.where(
            lane < 3 * nslot, lo, 0.0))).astype(BF16)

    dt_step = min(seq, CONV_ROWS)
    for r0 in range(0, seq, dt_step):
        raw_rep = sum(_dot(part, sel_s[...]) for part in _split3(dt_ref[r0:r0 + dt_step, :]))
        dtrep_s[r0:r0 + dt_step, :] = jax.nn.softplus(raw_rep + dbc_ref[...])

    fwd_lane = (si % nslot) < SSD_HPG
    fwd_row = lax.broadcasted_iota(jnp.int32, (nslot, q), 0) < SSD_HPG

    def cums(c):
        rows = pl.ds(pl.multiple_of(c * q, q), q)
        dt_rep = dtrep_s[rows, :]
        da_rep = dt_rep * a_rep
        dt_row = dt_rep.T[0:nslot]
        da_row = dt_row * a_row
        csf_rep = sum(_dot(lower_b, part) for part in _split3(da_rep))
        csf_row = sum(_dot(part, upper_b) for part in _split3(da_row))
        bt = bt_s[:, rows]
        cb = _dot(cs_s[rows, :], bt.astype(BF16))
        return rows, dt_row, da_rep, da_row, csf_rep, csf_row, bt, cb

    def segments(c, stage1):
        rows, dt_row, da_rep, da_row, csf_rep, csf_row, bt, cb = stage1
        tot_rep = csf_rep[q - 1:q, :]
        tot_row = csf_row[:, q - 1:q]
        cs_rep = jnp.where(fwd_lane, csf_rep, tot_rep - csf_rep + da_rep)
        cs_row = jnp.where(fwd_row, csf_row, tot_row - csf_row + da_row)
        csr_s[rows, :] = cs_rep
        et = terms3(jnp.broadcast_to(jnp.exp(tot_rep), (SUBLANE, q)))
        for d in range(2):
            dtot_s[d, c] = _dot(et, exp_s[d])
        w_row = dt_row * jnp.exp(tot_row - cs_row)
        neg_row = cs_row - jnp.log(dt_row)
        seg = lambda s: cs_rep[:, s:s + 1] - neg_row[s:s + 1, :]
        segs = []
        for pr in range(SSD_HPG // 2):
            f0, f1 = 2 * pr, 2 * pr + 1
            b0, b1 = SSD_HPG + f0, SSD_HPG + f1
            segs.append((jnp.concatenate([seg(f0), seg(f1)], axis=1),
                         jnp.concatenate([seg(b0), seg(b1)], axis=1)))
        return rows, dt_row, w_row, bt, cb, segs

    def products(c, stage2):
        rows, dt_row, w_row, bt, cb, segs = stage2
        cb2 = jnp.concatenate([cb, cb], axis=1)
        bt2 = jnp.concatenate([bt, bt], axis=1)
        xb = xs_s[rows, :].astype(BF16)
        row2 = lambda v, s0, s1: jnp.concatenate([v[s0:s0 + 1, :], v[s1:s1 + 1, :]], axis=1)
        for pr in range(SSD_HPG // 2):
            ps = slice(pr * pair_w, (pr + 1) * pair_w)
            f0, f1 = 2 * pr, 2 * pr + 1
            b0, b1 = SSD_HPG + f0, SSD_HPG + f1
            seg_f, seg_b = segs[pr]
            dec = jnp.exp(jnp.where(lower2, seg_f, seg_b)) + jnp.where(diag2, row2(dt_row, b0, b1), 0.0)
            m = (cb2 * dec).astype(BF16)
            btw_f = (bt2 * row2(w_row, f0, f1)).astype(BF16)
            btw_b = (bt2 * row2(w_row, b0, b1)).astype(BF16)
            xp = xb[:, ps]
            xh = jnp.concatenate([xp * half_s[0], xp * half_s[1]], axis=0)
            r = _dot(jnp.concatenate([m, btw_f, btw_b], axis=0), xh)
            ys_s[rows, ps] = r[:q]
            new_s[0, c, :, ps] = r[q:2 * q]
            new_s[1, c, :, ps] = r[2 * q:]

    nchunks = seq // q
    group = 8 if nchunks % 8 == 0 else 2

    def local_pass(i, carry):
        chunks = [group * i + n for n in range(group)]
        stage1 = [cums(c) for c in chunks]
        stage2 = [segments(c, s) for c, s in zip(chunks, stage1)]
        for c, s in zip(chunks, stage2):
            products(c, s)
        return carry

    lax.fori_loop(0, nchunks // group, local_pass, 0)

    for d in range(2):
        for pr in range(SSD_HPG // 2):
            ps = slice(pr * pair_w, (pr + 1) * pair_w)
            if has_h0:
                st_s[:, ps] = h0_ref[d, 2 * pr:2 * pr + 2].reshape(pair_w, SSD_STATE).T
            else:
                st_s[:, ps] = jnp.zeros((SSD_STATE, pair_w), F32)

        def carry_state(i, carry, d=d):
            c = i if d == 0 else nchunks - 1 - i
            st = st_s[...]
            ent_s[d, c] = st.astype(BF16)
            st_s[...] = st * dtot_s[d, c][0:1, :] + new_s[d, c]
            return carry

        lax.fori_loop(0, nchunks, carry_state, 0, unroll=2)
        for pr in range(SSD_HPG // 2 if has_st else 0):
            ps = slice(pr * pair_w, (pr + 1) * pair_w)
            st_ref[d, 2 * pr:2 * pr + 2] = st_s[:, ps].T.reshape(2, SSD_HEADDIM, SSD_STATE)

    def finish(i, carry):
        chunks = [group * i + n for n in range(group)]
        rows = [pl.ds(pl.multiple_of(c * q, q), q) for c in chunks]
        offs = [[_dot(cs_s[r, :], ent_s[d, c]) for d in range(2)] for c, r in zip(chunks, rows)]
        e1s = [terms3(jnp.exp(csr_s[r, :])) for r in rows]
        spreads = [[_dot(e1, exp_s[d]) for d in range(2)] for e1 in e1s]
        for r, off, spread in zip(rows, offs, spreads):
            y = ys_s[r, :] + dsk_ref[...] * xs_s[r, :] + off[0] * spread[0] + off[1] * spread[1]
            y = y * _silu(z_ref[r, :].astype(F32))
            y = y * lax.rsqrt(jnp.mean(y * y, axis=-1, keepdims=True) + EPS) * nw_ref[...]
            y_ref[r, :] = y.astype(BF16)
        return carry

    lax.fori_loop(0, nchunks // group, finish, 0)


def _ssd(u, dt, nseq, seq, conv_w, conv_b, dt_bias, a_log, d_skip, norm_w, *,
         h0=None, h0_layer=0, st_layers=0, st_layer=0, st_prev=None):
    t = nseq * seq
    g = SSD_GROUPS
    e = SSD_HPG
    heads = g * e
    gw = e * SSD_HEADDIM
    d_inner = heads * SSD_HEADDIM
    nc = seq // SSD_CHUNK
    xb0 = d_inner // gw
    bb0 = 2 * d_inner // SSD_STATE
    per_group = lambda v: v.reshape(2, g, e).transpose(1, 0, 2).reshape(g, 2 * e)
    rep = LANE // (2 * e)
    dbg = per_group(dt_bias)
    alg = per_group(a_log)
    dbg_rep = jnp.tile(dbg, (1, rep)).reshape(g, 1, LANE)
    alg_rep = jnp.tile(alg, (1, rep)).reshape(g, 1, LANE)
    dsk = jnp.repeat(d_skip, SSD_HEADDIM).reshape(1, d_inner)
    cw = lambda width, blk0: pl.BlockSpec((CONV_W, width), lambda b, gi: (0, blk0 + gi))
    cbias = lambda width, blk0: pl.BlockSpec((1, width), lambda b, gi: (0, blk0 + gi))
    st_block = (None, None, 2, e, SSD_HEADDIM, SSD_STATE)
    in_specs = [
        pl.BlockSpec((seq, gw), lambda b, gi: (b, gi)),
        pl.BlockSpec((seq, gw), lambda b, gi: (b, xb0 + gi)),
        pl.BlockSpec((seq, SSD_STATE), lambda b, gi: (b, bb0 + gi)),
        pl.BlockSpec((seq, SSD_STATE), lambda b, gi: (b, bb0 + g + gi)),
        pl.BlockSpec((seq, LANE), lambda b, gi: (b, 0)),
        pl.BlockSpec((None, 1, LANE), lambda b, gi: (gi, 0, 0)),
        pl.BlockSpec((None, 1, LANE), lambda b, gi: (gi, 0, 0)),
        pl.BlockSpec((None, 2 * e, 1), lambda b, gi: (gi, 0, 0)),
        cw(gw, 0), cw(SSD_STATE, d_inner // SSD_STATE), cw(SSD_STATE, d_inner // SSD_STATE + g),
        cbias(gw, 0), cbias(SSD_STATE, d_inner // SSD_STATE), cbias(SSD_STATE, d_inner // SSD_STATE + g),
        cbias(gw, 0), cbias(gw, 0),
    ]
    conv_b2 = conv_b.reshape(1, -1)
    args = [u, u, u, u, dt, dbg_rep, alg_rep, alg.reshape(g, 2 * e, 1),
            conv_w, conv_w, conv_w, conv_b2, conv_b2, conv_b2, dsk, norm_w.reshape(1, d_inner)]
    if h0 is not None:
        in_specs.append(pl.BlockSpec(st_block, lambda b, gi: (b, h0_layer, 0, gi, 0, 0)))
        args.append(h0)
    out_specs = [pl.BlockSpec((seq, gw), lambda b, gi: (b, gi))]
    out_shape = [jax.ShapeDtypeStruct((t, d_inner), BF16)]
    aliases = {}
    if st_layers:
        if st_prev is not None:
            aliases = {len(args): 1}
            in_specs.append(pl.BlockSpec(memory_space=pl.ANY))
            args.append(st_prev)
        out_specs.append(pl.BlockSpec(st_block, lambda b, gi: (b, st_layer, 0, gi, 0, 0)))
        out_shape.append(jax.ShapeDtypeStruct((nseq, st_layers, 2, heads, SSD_HEADDIM, SSD_STATE), F32))
    return pl.pallas_call(
        functools.partial(_ssd_kernel, seq=seq, has_h0=h0 is not None, has_prev=st_prev is not None,
                          has_st=bool(st_layers)),
        grid=(nseq, g),
        in_specs=in_specs,
        out_specs=out_specs,
        out_shape=out_shape,
        input_output_aliases=aliases,
        scratch_shapes=[
            pltpu.VMEM((seq + 2 * HALO, gw), BF16),
            pltpu.VMEM((seq + 2 * HALO, SSD_STATE), BF16),
            pltpu.VMEM((seq + 2 * HALO, SSD_STATE), BF16),
            pltpu.VMEM((seq, gw), F32),
            pltpu.VMEM((SSD_STATE, seq), F32),
            pltpu.VMEM((seq, SSD_STATE), BF16),
            pltpu.VMEM((seq, gw), F32),
            pltpu.VMEM((SSD_STATE, gw), F32),
            pltpu.VMEM((2, SSD_CHUNK, gw), BF16),
            pltpu.VMEM((2, SSD_CHUNK, LANE), BF16),
            pltpu.VMEM((seq, LANE), F32),
            pltpu.VMEM((2, nc, SUBLANE, gw), F32),
            pltpu.VMEM((2, nc, SSD_STATE, gw), F32),
            pltpu.VMEM((2, nc, SSD_STATE, gw), BF16),
            pltpu.VMEM((SSD_CHUNK, LANE), BF16),
            pltpu.VMEM((seq, LANE), F32),
        ],
        compiler_params=_cparams("parallel", "parallel"),
        name="ssd",
    )(*args)


def _rope_tables(seq):
    rows = seq // GRID_W
    row = jnp.repeat(jnp.arange(rows), GRID_W).astype(F32)
    col = jnp.tile(jnp.arange(GRID_W), rows).astype(F32)
    quarter = DA_HALF // 4
    inv = ROPE_THETA ** (-jnp.arange(quarter, dtype=F32) / quarter)
    ang_r = row[:, None] * inv
    ang_c = col[:, None] * inv
    ang = jnp.concatenate([ang_r, ang_r, ang_c, ang_c], axis=-1)
    sign = jnp.tile(jnp.concatenate([-jnp.ones((quarter,), F32), jnp.ones((quarter,), F32)]), 2)
    cos = jnp.cos(ang)
    sin = jnp.sin(ang) * sign
    return jnp.tile(cos, (1, 2)), jnp.tile(sin, (1, 2))


def kernel(x_prompt, x_sample, c, cache_attn_k, cache_attn_v, state_lru, state_ssd, c_ctx, w_ada, b_ada, norm_g, lru_conv_w, lru_conv_b, lru_w_r, lru_b_r, lru_w_i, lru_b_i, lru_lambda, even_w_in, even_w_out, da_q_norm, da_k_norm, da_lambda, da_subln, ssd_w_in, ssd_conv_w, ssd_conv_b, ssd_dt_bias, ssd_a_log, ssd_d, ssd_norm_w, ssd_w_out, ffn_w_in, ffn_w_out):
    depth = w_ada.shape[0]
    batch, seq_p, d = x_prompt.shape
    dec_batch, seq_s, _ = x_sample.shape
    past = cache_attn_k.shape[2]
    heads = cache_attn_k.shape[3]
    d_rnn = lru_conv_w.shape[2]
    d_ff = ffn_w_out.shape[1]
    d_inner = ssd_w_out.shape[1]
    n_even = even_w_in.shape[0]

    cond = jnp.zeros((N_COND, d), F32).at[0].set(c_ctx).at[1:1 + dec_batch].set(c)
    mod_all = _ada(cond, w_ada, b_ada).reshape(depth, N_COND, 6, d)
    rope = _rope_tables(seq_s)
    cache_k = cache_attn_k.reshape(dec_batch, n_even, past, heads * DA_VDIM)
    cache_v = cache_attn_v.reshape(dec_batch, n_even, past, heads * DA_VDIM)

    even_in_b, even_out_b = even_w_in.astype(BF16), even_w_out.astype(BF16)
    n_odd = ssd_w_in.shape[0]
    ssd_heads = d_inner // SSD_HEADDIM
    n_main = ssd_w_in.shape[2] - 2 * ssd_heads
    ssd_in_b, ssd_out_b = ssd_w_in.astype(BF16), ssd_w_out.astype(BF16)
    hpg = ssd_heads // SSD_GROUPS
    ssd_dt_b = ssd_w_in[:, :, n_main:].astype(BF16).reshape(n_odd, d, 2, SSD_GROUPS, hpg).transpose(
        0, 1, 3, 2, 4).reshape(n_odd, d, 2 * ssd_heads)
    ffn_in_b, ffn_out_b = ffn_w_in.astype(BF16), ffn_w_out.astype(BF16)

    xs = [x_prompt.reshape(batch * seq_p, d), x_sample.reshape(dec_batch * seq_s, d)]
    shapes = [(batch, seq_p), (dec_batch, seq_s)]
    new_kv, new_lru, new_ssd = None, [], None

    for i in range(depth):
        j = i // 2
        mods = [mod_all[i, 0:1], mod_all[i, 1:1 + dec_batch]]
        g_mix = norm_g[i, 0].reshape(1, d)
        g_ffn = norm_g[i, 1].reshape(1, d)
        if i % 2 == 0:
            lam_init = 0.8 - 0.6 * math.exp(-0.3 * i)
            w_r = lru_w_r[j].astype(BF16)
            w_i = lru_w_i[j].astype(BF16)
            for s in range(2):
                nseq, seq = shapes[s]
                u = _inproj(xs[s], g_mix, mods[s], even_in_b, j, mode="plain", n_out=even_in_b.shape[2],
                            tn=2560, shift_idx=0, scale_idx=1, name="even_in")
                h0 = jnp.zeros((nseq, 2, d_rnn), F32) if s == 0 else state_lru[:, j]
                rec, s_fin = _lru(u, nseq, seq, lru_conv_w[j], lru_conv_b[j], w_r, w_i,
                                  lru_b_r[j], lru_b_i[j], lru_lambda[j], h0)
                if s == 0:
                    att, kc, vc = _attn(u, nseq, seq, heads, lam_init, da_q_norm[j], da_k_norm[j],
                                        da_lambda[j], da_subln[j], kv_layers=n_even, kv_layer=j,
                                        kv_prev=new_kv)
                    new_kv = (kc, vc)
                    new_lru.append(s_fin)
                else:
                    att = _attn(u, nseq, seq, heads, lam_init, da_q_norm[j], da_k_norm[j],
                                da_lambda[j], da_subln[j], ctx=(cache_k, cache_v, j), rope=rope)
                xs[s] = _outproj([rec, att], even_out_b, j, xs[s], mods[s], gate_idx=2, tn=1024,
                                 name="even_out")
        else:
            for s in range(2):
                nseq, seq = shapes[s]
                u, dt = _inproj(xs[s], g_mix, mods[s], ssd_in_b, j, mode="dt", n_out=n_main, tn=2048,
                                shift_idx=0, scale_idx=1, name="odd_in", w_dt=ssd_dt_b)
                ssd_args = (u, dt, nseq, seq, ssd_conv_w[j], ssd_conv_b[j], ssd_dt_bias[j],
                            ssd_a_log[j], ssd_d[j], ssd_norm_w[j])
                if s == 0:
                    y, new_ssd = _ssd(*ssd_args, st_layers=n_odd, st_layer=j, st_prev=new_ssd)
                else:
                    y, = _ssd(*ssd_args, h0=state_ssd, h0_layer=j)
                xs[s] = _outproj([y], ssd_out_b, j, xs[s], mods[s], gate_idx=2, tn=1024, name="odd_out")
        for s in range(2):
            act = _inproj(xs[s], g_ffn, mods[s], ffn_in_b, i, mode="glu", n_out=d_ff, tn=512,
                          shift_idx=3, scale_idx=4, name="ffn_in")
            xs[s] = _outproj([act], ffn_out_b, i, xs[s], mods[s], gate_idx=5, tn=512, name="ffn_out")

    return (xs[0].reshape(batch, seq_p, d), xs[1].reshape(dec_batch, seq_s, d),
            new_kv[0].reshape(batch, n_even, seq_p, heads, DA_VDIM),
            new_kv[1].reshape(batch, n_even, seq_p, heads, DA_VDIM),
            jnp.stack(new_lru, axis=1), new_ssd)
```
